```python
import jax, jax.numpy as jnp
from jax import lax
import numpy as np

D_MODEL = 2048
BATCH = 4
SEQ = 4096
DEPTH = 2

N_A_LAYERS = DEPTH // 2
N_B_LAYERS = DEPTH - N_A_LAYERS
N_DENSE_LAYERS = (DEPTH + 1) // 2
N_MOE_LAYERS = DEPTH // 2

GLA_HEADS = 4
GLA_DK = D_MODEL // 2
GLA_DV = D_MODEL
GLA_DK_HEAD = GLA_DK // GLA_HEADS
GLA_DV_HEAD = GLA_DV // GLA_HEADS
GLA_GATE_RANK = 16
GLA_TAU = 16.0
GLA_CHUNK = 64

MOBA_HEADS = 16
MOBA_KV_HEADS = 4
MOBA_HEAD_DIM = D_MODEL // MOBA_HEADS
MOBA_BLOCK = 256
MOBA_TOPK = 3
MOBA_QGROUP = 128
MOBA_GROUPS_PER_STEP = 32

FFN_DENSE = 5504
N_EXPERTS = 8
TOP_K = 2
FFN_EXPERT = 7168
MOE_GROUP = 1024

LN_EPS = 1e-5
RMS_EPS = 1e-6
DEEPNORM_ALPHA = (2 * DEPTH) ** 0.25
DEEPNORM_BETA = (8 * DEPTH) ** -0.25

kernel_name = "yoco_gla_moba_moe_deepnorm"

F32 = jnp.float32


def layer_norm(x, g, b):
    xf = x.astype(F32)
    mu = jnp.mean(xf, -1, keepdims=True)
    var = jnp.mean(jnp.square(xf - mu), -1, keepdims=True)
    return ((xf - mu) * lax.rsqrt(var + LN_EPS) * g + b).astype(x.dtype)


def post_norm(x, y, g, b):
    return layer_norm(DEEPNORM_ALPHA * x + y, g, b)


def group_rows(seg, num_segments, group, multiple):
    n_asg = seg.shape[0]
    cap = -(-(n_asg + num_segments * group) // multiple) * multiple
    counts = jax.ops.segment_sum(jnp.ones_like(seg), seg, num_segments=num_segments)
    padded = (counts + group - 1) // group * group
    pad_end = jnp.cumsum(padded)
    pad_start = pad_end - padded
    start = jnp.cumsum(counts) - counts
    order = jnp.argsort(seg)
    sorted_seg = seg[order]
    dest_sorted = pad_start[sorted_seg] + jnp.arange(n_asg, dtype=seg.dtype) - start[sorted_seg]
    dest = jnp.zeros_like(seg).at[order].set(dest_sorted)
    group_start = jnp.arange(cap // group, dtype=seg.dtype) * group
    group_seg = jnp.minimum(jnp.searchsorted(pad_end, group_start, side="right"), num_segments - 1)
    group_used = group_start < pad_end[-1]
    return dest, group_seg, group_used, cap


def gla_mixer(h, w_in, w_gate_up, b_gate, g_norm, w_o):
    bsz, s, _ = h.shape
    n_chunk = -(-s // GLA_CHUNK)
    pad = n_chunk * GLA_CHUNK - s
    proj = h @ w_in
    q, k, v, r, z = jnp.split(
        proj, [GLA_DK, 2 * GLA_DK, 2 * GLA_DK + GLA_DV, 2 * GLA_DK + 2 * GLA_DV], axis=-1)
    log_a = jax.nn.log_sigmoid((z @ w_gate_up + b_gate).astype(F32)) / GLA_TAU

    def chunked(t, hd):
        t = jnp.pad(t.astype(F32), ((0, 0), (0, pad), (0, 0)))
        return t.reshape(bsz, n_chunk, GLA_CHUNK, GLA_HEADS, hd)

    q_c = chunked(q, GLA_DK_HEAD) * (GLA_DK_HEAD ** -0.5)
    k_c = chunked(k, GLA_DK_HEAD)
    v_c = chunked(v, GLA_DV_HEAD)
    b_cum = jnp.cumsum(chunked(log_a, GLA_DK_HEAD), axis=2)
    b_last = b_cum[:, :, -1:]
    q_dec = q_c * jnp.exp(b_cum)
    k_inv = k_c * jnp.exp(-b_cum)
    k_end = k_c * jnp.exp(b_last - b_cum)
    causal = jnp.tril(jnp.ones((GLA_CHUNK, GLA_CHUNK), bool))
    att = jnp.einsum("bncht,bnmht->bnhcm", q_dec, k_inv)
    att = jnp.where(causal, att, 0.0)
    o_intra = jnp.einsum("bnhcm,bnmhv->bnchv", att, v_c)
    decay = jnp.exp(b_last[:, :, 0])

    def scan_step(state, xs):
        q_n, k_n, v_n, d_n = xs
        o_n = jnp.einsum("bcht,bhtv->bchv", q_n, state)
        state = d_n[..., None] * state + jnp.einsum("bcht,bchv->bhtv", k_n, v_n)
        return state, o_n

    mv = lambda t: jnp.moveaxis(t, 1, 0)
    state0 = jnp.zeros((bsz, GLA_HEADS, GLA_DK_HEAD, GLA_DV_HEAD), F32)
    _, o_inter = lax.scan(scan_step, state0, (mv(q_dec), mv(k_end), mv(v_c), mv(decay)))
    o = (o_intra + jnp.moveaxis(o_inter, 0, 1)).reshape(bsz, n_chunk * GLA_CHUNK, GLA_HEADS, GLA_DV_HEAD)[:, :s]
    o = o * lax.rsqrt(jnp.mean(o * o, -1, keepdims=True) + RMS_EPS) * g_norm
    o = (o.reshape(bsz, s, GLA_DV) * jax.nn.silu(r.astype(F32))).astype(h.dtype)
    return o @ w_o


def moba_shared_kv(h, w_kv):
    bsz, s, _ = h.shape
    n_blk = -(-s // MOBA_BLOCK)
    pad = n_blk * MOBA_BLOCK - s
    kv = jnp.pad(h @ w_kv, ((0, 0), (0, pad), (0, 0)))
    kv = kv.reshape(bsz, n_blk, MOBA_BLOCK, 2, MOBA_KV_HEADS, MOBA_HEAD_DIM)
    k_blk = kv[:, :, :, 0].transpose(0, 3, 1, 2, 4)
    v_blk = kv[:, :, :, 1].transpose(0, 3, 1, 2, 4)
    count = jnp.clip(s - jnp.arange(n_blk) * MOBA_BLOCK, 1, MOBA_BLOCK).astype(F32)
    k_mean = k_blk.astype(F32).sum(3) / count[None, None, :, None]
    return k_blk, v_blk, k_mean


def grouped_block_attention(q_rows, k_seg, v_seg, group_seg, group_used):
    cap, hd = q_rows.shape
    n_steps = cap // (MOBA_QGROUP * MOBA_GROUPS_PER_STEP)
    qs = q_rows.reshape(n_steps, MOBA_GROUPS_PER_STEP, MOBA_QGROUP, hd)
    segs = group_seg.reshape(n_steps, MOBA_GROUPS_PER_STEP)
    used = group_used.reshape(n_steps, MOBA_GROUPS_PER_STEP)

    def step(args):
        qg, sg, ug = args

        def compute():
            kg = k_seg[sg].astype(F32)
            vg = v_seg[sg].astype(F32)
            sc = jnp.einsum("pqd,pkd->pqk", qg.astype(F32), kg)
            m = sc.max(-1)
            p = jnp.exp(sc - m[..., None])
            return jnp.einsum("pqk,pkd->pqd", p, vg), m, p.sum(-1)

        def skip():
            zr = jnp.zeros(qg.shape[:-1], F32)
            return jnp.zeros(qg.shape, F32), zr, zr

        return lax.cond(jnp.any(ug), compute, skip)

    o, m, l = lax.map(step, (qs, segs, used))
    return o.reshape(cap, hd), m.reshape(cap), l.reshape(cap)


def moba_mixer(h, w_q, w_o, k_blk, v_blk, k_mean):
    bsz, s, _ = h.shape
    n_blk = k_blk.shape[2]
    s_pad = n_blk * MOBA_BLOCK
    rep = MOBA_HEADS // MOBA_KV_HEADS
    hd = MOBA_HEAD_DIM
    q = (h @ w_q).reshape(bsz, s, MOBA_KV_HEADS, rep, hd) * (hd ** -0.5)

    gate = jnp.einsum("bsgrd,bgnd->bsgrn", q.astype(F32), k_mean)
    q_blk = jnp.arange(s) // MOBA_BLOCK
    past = jnp.arange(n_blk)[None, :] < q_blk[:, None]
    gate = jnp.where(past[None, :, None, None, :], gate, -jnp.inf)
    top_k = min(MOBA_TOPK, n_blk)
    _, sel = lax.top_k(gate, top_k)
    valid = sel < q_blk[None, :, None, None, None]

    q_pad = jnp.pad(q, ((0, 0), (0, s_pad - s), (0, 0), (0, 0), (0, 0)))
    q_pad = q_pad.reshape(bsz, n_blk, MOBA_BLOCK, MOBA_KV_HEADS, rep, hd)
    s_own = jnp.einsum("bnqgrd,bgnkd->bnqgrk", q_pad.astype(F32), k_blk.astype(F32))
    causal = jnp.tril(jnp.ones((MOBA_BLOCK, MOBA_BLOCK), bool))
    s_own = jnp.where(causal[None, None, :, None, None, :], s_own, -jnp.inf)
    m_own = s_own.max(-1)
    p_own = jnp.exp(s_own - m_own[..., None])
    l_own = p_own.sum(-1).reshape(bsz, s_pad, MOBA_KV_HEADS, rep)[:, :s]
    o_own = jnp.einsum("bnqgrk,bgnkd->bnqgrd", p_own, v_blk.astype(F32))
    o_own = o_own.reshape(bsz, s_pad, MOBA_KV_HEADS, rep, hd)[:, :s]
    m_own = m_own.reshape(bsz, s_pad, MOBA_KV_HEADS, rep)[:, :s]

    n_seg = bsz * MOBA_KV_HEADS * n_blk
    b_idx = jnp.arange(bsz)[:, None, None, None, None]
    g_idx = jnp.arange(MOBA_KV_HEADS)[None, None, :, None, None]
    seg = ((b_idx * MOBA_KV_HEADS + g_idx) * n_blk + sel).reshape(-1).astype(jnp.int32)
    dest, group_seg, group_used, cap = group_rows(
        seg, n_seg, MOBA_QGROUP, MOBA_QGROUP * MOBA_GROUPS_PER_STEP)
    q_rows = jnp.zeros((cap, hd), q.dtype).at[dest].set(jnp.repeat(q.reshape(-1, hd), top_k, axis=0))
    k_seg = k_blk.reshape(n_seg, MOBA_BLOCK, hd)
    v_seg = v_blk.reshape(n_seg, MOBA_BLOCK, hd)
    o_rows, m_rows, l_rows = grouped_block_attention(q_rows, k_seg, v_seg, group_seg, group_used)
    sel_shape = (bsz, s, MOBA_KV_HEADS, rep, top_k)
    o_sel = o_rows[dest].reshape(sel_shape + (hd,))
    m_sel = jnp.where(valid, m_rows[dest].reshape(sel_shape), -jnp.inf)
    l_sel = l_rows[dest].reshape(sel_shape)

    m_tot = jnp.maximum(m_own, m_sel.max(-1))
    w_own = jnp.exp(m_own - m_tot)
    w_sel = jnp.exp(m_sel - m_tot[..., None])
    num = o_own * w_own[..., None] + jnp.einsum("bsgrt,bsgrtd->bsgrd", w_sel, o_sel)
    den = l_own * w_own + (l_sel * w_sel).sum(-1)
    out = (num / den[..., None]).astype(h.dtype).reshape(bsz, s, MOBA_HEADS * hd)
    return out @ w_o


def swiglu(h, w_gu, w_down):
    g, u = jnp.split(h @ w_gu, 2, axis=-1)
    return (jax.nn.silu(g) * u) @ w_down


def moe_swiglu(h, w_router, w_gu, w_down):
    bsz, s, d = h.shape
    xt = h.reshape(-1, d)
    n_tok = xt.shape[0]
    logits = (xt @ w_router).astype(F32)
    top_val, top_idx = lax.top_k(logits, TOP_K)
    gates = jax.nn.softmax(top_val, axis=-1)
    seg = top_idx.reshape(-1).astype(jnp.int32)
    dest, group_seg, group_used, cap = group_rows(seg, N_EXPERTS, MOE_GROUP, MOE_GROUP)
    rows = jnp.zeros((cap, d), xt.dtype).at[dest].set(jnp.repeat(xt, TOP_K, axis=0))
    out_dtype = jnp.result_type(xt, w_down)

    def step(args):
        xg, e, used = args

        def compute():
            g, u = jnp.split(xg @ w_gu[e], 2, axis=-1)
            return ((jax.nn.silu(g) * u) @ w_down[e]).astype(out_dtype)

        return lax.cond(used, compute, lambda: jnp.zeros(xg.shape, out_dtype))

    y_rows = lax.map(step, (rows.reshape(-1, MOE_GROUP, d), group_seg, group_used)).reshape(cap, d)
    y = (y_rows[dest].reshape(n_tok, TOP_K, d).astype(F32) * gates[..., None]).sum(1)
    return y.astype(h.dtype).reshape(bsz, s, d)


def setup_inputs(seed: int = 0) -> dict:
    key = jax.random.key(seed)
    ks = jax.random.split(key, 20)

    def dense(k, shape, fan_in, scale=1.0):
        return jax.random.normal(k, shape, F32) * (scale * fan_in ** -0.5)

    def gain(k, shape):
        return 1.0 + 0.02 * jax.random.normal(k, shape, F32)

    def bias(k, shape, scale=0.02):
        return scale * jax.random.normal(k, shape, F32)

    gla_in_width = 2 * GLA_DK + 2 * GLA_DV + GLA_GATE_RANK
    return {
        "x": jax.random.normal(ks[0], (BATCH, SEQ, D_MODEL), F32),
        "w_in_a": dense(ks[1], (N_A_LAYERS, D_MODEL, gla_in_width), D_MODEL),
        "w_gate_up_a": dense(ks[2], (N_A_LAYERS, GLA_GATE_RANK, GLA_DK), GLA_GATE_RANK),
        "b_gate_a": bias(ks[3], (N_A_LAYERS, GLA_DK), 0.1),
        "g_norm_a": gain(ks[4], (N_A_LAYERS, GLA_DV_HEAD)),
        "w_o_a": dense(ks[5], (N_A_LAYERS, GLA_DV, D_MODEL), GLA_DV, DEEPNORM_BETA),
        "w_kv_shared": dense(ks[6], (D_MODEL, 2 * MOBA_KV_HEADS * MOBA_HEAD_DIM), D_MODEL),
        "w_q_b": dense(ks[7], (N_B_LAYERS, D_MODEL, MOBA_HEADS * MOBA_HEAD_DIM), D_MODEL),
        "w_o_b": dense(ks[8], (N_B_LAYERS, MOBA_HEADS * MOBA_HEAD_DIM, D_MODEL), MOBA_HEADS * MOBA_HEAD_DIM, DEEPNORM_BETA),
        "ln_mix_g": gain(ks[9], (DEPTH, D_MODEL)),
        "ln_mix_b": bias(ks[10], (DEPTH, D_MODEL)),
        "w_gu_dense": dense(ks[11], (N_DENSE_LAYERS, D_MODEL, 2 * FFN_DENSE), D_MODEL),
        "w_down_dense": dense(ks[12], (N_DENSE_LAYERS, FFN_DENSE, D_MODEL), FFN_DENSE, DEEPNORM_BETA),
        "w_router": dense(ks[13], (N_MOE_LAYERS, D_MODEL, N_EXPERTS), D_MODEL),
        "w_gu_moe": dense(ks[14], (N_MOE_LAYERS, N_EXPERTS, D_MODEL, 2 * FFN_EXPERT), D_MODEL),
        "w_down_moe": dense(ks[15], (N_MOE_LAYERS, N_EXPERTS, FFN_EXPERT, D_MODEL), FFN_EXPERT, DEEPNORM_BETA),
        "ln_ffn_g": gain(ks[16], (DEPTH, D_MODEL)),
        "ln_ffn_b": bias(ks[17], (DEPTH, D_MODEL)),
    }


def reference(x, w_in_a, w_gate_up_a, b_gate_a, g_norm_a, w_o_a, w_kv_shared, w_q_b, w_o_b,
              ln_mix_g, ln_mix_b, w_gu_dense, w_down_dense, w_router, w_gu_moe, w_down_moe,
              ln_ffn_g, ln_ffn_b):
    h = x
    shared = None
    for i in range(DEPTH):
        if i < N_A_LAYERS:
            y = gla_mixer(h, w_in_a[i], w_gate_up_a[i], b_gate_a[i], g_norm_a[i], w_o_a[i])
        else:
            if shared is None:
                shared = moba_shared_kv(h, w_kv_shared)
            j = i - N_A_LAYERS
            y = moba_mixer(h, w_q_b[j], w_o_b[j], *shared)
        h = post_norm(h, y, ln_mix_g[i], ln_mix_b[i])
        if i % 2 == 0:
            y = swiglu(h, w_gu_dense[i // 2], w_down_dense[i // 2])
        else:
            y = moe_swiglu(h, w_router[i // 2], w_gu_moe[i // 2], w_down_moe[i // 2])
        h = post_norm(h, y, ln_ffn_g[i], ln_ffn_b[i])
    return h
```

```python
import functools
import math

import jax
import jax.numpy as jnp
from jax import lax
from jax.experimental import pallas as pl
from jax.experimental.pallas import tpu as pltpu

F32 = jnp.float32
BF16 = jnp.bfloat16

DEPTH = 2
GLA_HEADS = 4
GLA_GATE_RANK = 16
GLA_TAU = 16.0
GLA_CHUNK = 64
MOBA_HEADS = 16
MOBA_KV_HEADS = 4
MOBA_BLOCK = 256
MOBA_TOPK = 3
N_EXPERTS = 8
TOP_K = 2
LN_EPS = 1e-5
RMS_EPS = 1e-6
DEEPNORM_ALPHA = (2 * DEPTH) ** 0.25

LANES = 128
VMEM_LIMIT_BYTES = 56 * 1024 * 1024

GLA_STEP_TOKENS = 512
MOE_TILE = 1024
MOE_SUB = 256
GATHER_TILE = 512


def _cparams(sem):
    return pltpu.CompilerParams(dimension_semantics=sem, vmem_limit_bytes=VMEM_LIMIT_BYTES)


def _pick(n, prefs):
    for p in prefs:
        if n % p == 0:
            return p
    return n


def _mm_body(x_ref, w_ref, o_ref, *, n_scaled, scale):
    acc = jnp.dot(x_ref[...], w_ref[...], preferred_element_type=F32)
    if n_scaled:
        acc = acc * jnp.where(pl.program_id(0) < n_scaled, scale, 1.0).astype(F32)
    o_ref[...] = acc.astype(o_ref.dtype)


def matmul(x, w, *, tm, tn, out_dtype, n_scaled=0, scale=1.0):
    m, k = x.shape
    n = w.shape[1]
    return pl.pallas_call(
        functools.partial(_mm_body, n_scaled=n_scaled, scale=scale),
        grid=(n // tn, m // tm),
        in_specs=[pl.BlockSpec((tm, k), lambda j, i: (i, 0)),
                  pl.BlockSpec((k, tn), lambda j, i: (0, j))],
        out_specs=pl.BlockSpec((tm, tn), lambda j, i: (i, j)),
        out_shape=jax.ShapeDtypeStruct((m, n), out_dtype),
        compiler_params=_cparams(("parallel", "parallel")),
        name="matmul",
    )(x, w)


def _layer_norm_rows(t, g, b):
    mu = jnp.mean(t, axis=-1, keepdims=True)
    d = t - mu
    var = jnp.mean(d * d, axis=-1, keepdims=True)
    return d * lax.rsqrt(var + LN_EPS) * g + b


def _mm_ln_body(x_ref, w_ref, res_ref, g_ref, b_ref, o_ref, ob_ref, acc_ref, *, nk):
    part = jnp.dot(x_ref[...], w_ref[...], preferred_element_type=F32)

    def finish(y):
        out = _layer_norm_rows(DEEPNORM_ALPHA * res_ref[...] + y, g_ref[...], b_ref[...])
        o_ref[...] = out
        ob_ref[...] = out.astype(BF16)

    if nk == 1:
        finish(part)
        return
    k = pl.program_id(1)

    @pl.when(k == 0)
    def _():
        acc_ref[...] = part

    @pl.when(jnp.logical_and(k > 0, k < nk - 1))
    def _():
        acc_ref[...] += part

    @pl.when(k == nk - 1)
    def _():
        finish(acc_ref[...] + part)


def matmul_post_norm(x, w, res, g, b, *, tm, tk):
    m, k = x.shape
    d = w.shape[1]
    nk = k // tk
    return pl.pallas_call(
        functools.partial(_mm_ln_body, nk=nk),
        grid=(m // tm, nk),
        in_specs=[pl.BlockSpec((tm, tk), lambda i, kk: (i, kk)),
                  pl.BlockSpec((tk, d), lambda i, kk: (kk, 0)),
                  pl.BlockSpec((tm, d), lambda i, kk: (i, 0)),
                  pl.BlockSpec((1, d), lambda i, kk: (0, 0)),
                  pl.BlockSpec((1, d), lambda i, kk: (0, 0))],
        out_specs=[pl.BlockSpec((tm, d), lambda i, kk: (i, 0)),
                   pl.BlockSpec((tm, d), lambda i, kk: (i, 0))],
        out_shape=[jax.ShapeDtypeStruct((m, d), F32), jax.ShapeDtypeStruct((m, d), BF16)],
        scratch_shapes=[pltpu.VMEM((tm, d), F32)],
        compiler_params=_cparams(("parallel", "arbitrary")),
        name="matmul_post_norm",
    )(x, w, res, g.reshape(1, d), b.reshape(1, d))


def _silu(x):
    return x / (1.0 + jnp.exp(-x))


def _swiglu_body(x_ref, wg_ref, wu_ref, o_ref):
    x = x_ref[...]
    g = jnp.dot(x, wg_ref[...], preferred_element_type=F32)
    u = jnp.dot(x, wu_ref[...], preferred_element_type=F32)
    o_ref[...] = (_silu(g) * u).astype(o_ref.dtype)


def swiglu_up(x, wg, wu, *, tm, tf):
    m, k = x.shape
    f = wg.shape[1]
    return pl.pallas_call(
        _swiglu_body,
        grid=(f // tf, m // tm),
        in_specs=[pl.BlockSpec((tm, k), lambda j, i: (i, 0)),
                  pl.BlockSpec((k, tf), lambda j, i: (0, j)),
                  pl.BlockSpec((k, tf), lambda j, i: (0, j))],
        out_specs=pl.BlockSpec((tm, tf), lambda j, i: (i, j)),
        out_shape=jax.ShapeDtypeStruct((m, f), BF16),
        compiler_params=_cparams(("parallel", "parallel")),
        name="swiglu_up",
    )(x, wg, wu)


def _gla_body(q_ref, k_ref, v_ref, r_ref, z_ref, wgu_ref, bg_ref, gn_ref, o_ref, state_ref,
              *, chunk, n_chunks, dk):
    @pl.when(pl.program_id(2) == 0)
    def _():
        state_ref[...] = jnp.zeros_like(state_ref)

    row = lax.broadcasted_iota(jnp.int32, (chunk, chunk), 0)
    col = lax.broadcasted_iota(jnp.int32, (chunk, chunk), 1)
    causal = col <= row
    tril = jnp.where(causal, 1.0, 0.0).astype(BF16)
    nt = (((1,), (1,)), ((), ()))
    tn = (((0,), (0,)), ((), ()))

    def body(c, carry):
        sl = pl.ds(pl.multiple_of(c * chunk, chunk), chunk)
        gate = jnp.dot(z_ref[0, sl, :].astype(BF16), wgu_ref[0], preferred_element_type=F32) + bg_ref[0]
        log_a = (jnp.minimum(gate, 0.0) - jnp.log(1.0 + jnp.exp(-jnp.abs(gate)))) / GLA_TAU
        la_hi = log_a.astype(BF16)
        la_lo = (log_a - la_hi.astype(F32)).astype(BF16)
        b_cum = (jnp.dot(tril, la_hi, preferred_element_type=F32)
                 + jnp.dot(tril, la_lo, preferred_element_type=F32))
        b_last = b_cum[chunk - 1:chunk, :]
        q = q_ref[0, sl, :] * (dk ** -0.5)
        k = k_ref[0, sl, :]
        q_dec = (q * jnp.exp(b_cum)).astype(BF16)
        k_inv = (k * jnp.exp(-b_cum)).astype(BF16)
        k_end = (k * jnp.exp(b_last - b_cum)).astype(BF16)
        v = v_ref[0, sl, :].astype(BF16)
        att = lax.dot_general(q_dec, k_inv, nt, preferred_element_type=F32)
        att = jnp.where(causal, att, 0.0).astype(BF16)
        o = jnp.dot(att, v, preferred_element_type=F32)
        state_t = state_ref[...]
        o = o + lax.dot_general(q_dec, state_t.astype(BF16), nt, preferred_element_type=F32)
        state_ref[...] = (jnp.exp(b_last) * state_t
                          + lax.dot_general(v, k_end, tn, preferred_element_type=F32))
        o = o * lax.rsqrt(jnp.mean(o * o, axis=-1, keepdims=True) + RMS_EPS) * gn_ref[...]
        o_ref[0, sl, :] = (o * _silu(r_ref[0, sl, :])).astype(o_ref.dtype)
        return carry

    lax.fori_loop(0, n_chunks, body, 0)


def gla_core(proj, z, wgu, bg, gn, *, heads, dk, dv, step):
    bsz, s, _ = proj.shape
    zr = z.shape[-1]
    k_off = heads
    v_off = 2 * heads * dk // dv
    r_off = v_off + heads
    return pl.pallas_call(
        functools.partial(_gla_body, chunk=GLA_CHUNK, n_chunks=step // GLA_CHUNK, dk=dk),
        grid=(bsz, heads, s // step),
        in_specs=[pl.BlockSpec((1, step, dk), lambda b, h, t: (b, t, h)),
                  pl.BlockSpec((1, step, dk), lambda b, h, t: (b, t, k_off + h)),
                  pl.BlockSpec((1, step, dv), lambda b, h, t: (b, t, v_off + h)),
                  pl.BlockSpec((1, step, dv), lambda b, h, t: (b, t, r_off + h)),
                  pl.BlockSpec((1, step, zr), lambda b, h, t: (b, t, 0)),
                  pl.BlockSpec((1, zr, dk), lambda b, h, t: (h, 0, 0)),
                  pl.BlockSpec((1, 1, dk), lambda b, h, t: (h, 0, 0)),
                  pl.BlockSpec((1, dv), lambda b, h, t: (0, 0))],
        out_specs=pl.BlockSpec((1, step, dv), lambda b, h, t: (b, t, h)),
        out_shape=jax.ShapeDtypeStruct((bsz, s, heads * dv), BF16),
        scratch_shapes=[pltpu.VMEM((dv, dk), F32)],
        compiler_params=_cparams(("parallel", "parallel", "arbitrary")),
        name="gla_core",
    )(proj, proj, proj, proj, z, wgu, bg, gn)


def _moba_body(q_ref, k_ref, v_ref, o_ref, kmean_ref, *, blk, n_blk, rep, hd, topk):
    i = pl.program_id(2)

    @pl.when(i == 0)
    def _():
        for j in range(n_blk):
            kj = k_ref[0, j * blk:(j + 1) * blk, :].astype(F32)
            kmean_ref[j:j + 1, :] = jnp.sum(kj, axis=0, keepdims=True) / blk

    nt = (((1,), (1,)), ((), ()))
    neg_inf = -jnp.inf
    lane = lax.broadcasted_iota(jnp.int32, (blk, n_blk), 1)
    past = lane < i
    row = lax.broadcasted_iota(jnp.int32, (blk, blk), 0)
    col = lax.broadcasted_iota(jnp.int32, (blk, blk), 1)
    causal = col <= row
    own = pl.ds(pl.multiple_of(i * blk, blk), blk)
    k_own = k_ref[0, own, :]
    v_own = v_ref[0, own, :]
    kmean = kmean_ref[...].astype(BF16)

    for r in range(rep):
        q = q_ref[0, :, r * hd:(r + 1) * hd]
        gate = lax.dot_general(q, kmean, nt, preferred_element_type=F32)
        gate = jnp.where(past, gate, neg_inf)
        beaten = jnp.zeros((blk, n_blk), jnp.int32)
        for c in range(n_blk):
            gc = gate[:, c:c + 1]
            wins = jnp.where(gc > gate, 1, jnp.where(gc == gate, jnp.where(c < lane, 1, 0), 0))
            beaten = beaten + wins
        chosen = jnp.where(past, jnp.where(beaten < topk, 1.0, 0.0), 0.0)

        s = lax.dot_general(q, k_own, nt, preferred_element_type=F32)
        s = jnp.where(causal, s, neg_inf)
        m0 = jnp.max(s, axis=-1, keepdims=True)
        p = jnp.exp(s - m0)
        l0 = jnp.sum(p, axis=-1, keepdims=True)
        acc0 = jnp.dot(p.astype(BF16), v_own, preferred_element_type=F32)

        def body(j, carry):
            m, l, acc = carry
            sl = pl.ds(pl.multiple_of(j * blk, blk), blk)
            sj = lax.dot_general(q, k_ref[0, sl, :], nt, preferred_element_type=F32)
            picked = jnp.sum(jnp.where(lane == j, chosen, 0.0), axis=-1, keepdims=True)
            sj = jnp.where(picked > 0.5, sj, neg_inf)
            m_new = jnp.maximum(m, jnp.max(sj, axis=-1, keepdims=True))
            a = jnp.exp(m - m_new)
            pj = jnp.exp(sj - m_new)
            l = a * l + jnp.sum(pj, axis=-1, keepdims=True)
            acc = a * acc + jnp.dot(pj.astype(BF16), v_ref[0, sl, :], preferred_element_type=F32)
            return m_new, l, acc

        _, l, acc = lax.fori_loop(0, i, body, (m0, l0, acc0))
        o_ref[0, :, r * hd:(r + 1) * hd] = (acc / l).astype(o_ref.dtype)


def moba_attention(qkv, *, heads, kv_heads, hd, blk, topk):
    bsz, s, _ = qkv.shape
    rep = heads // kv_heads
    n_blk = s // blk
    k_off = heads
    v_off = heads + kv_heads
    return pl.pallas_call(
        functools.partial(_moba_body, blk=blk, n_blk=n_blk, rep=rep, hd=hd, topk=topk),
        grid=(bsz, kv_heads, n_blk),
        in_specs=[pl.BlockSpec((1, blk, rep * hd), lambda b, g, i: (b, i, g)),
                  pl.BlockSpec((1, s, hd), lambda b, g, i: (b, 0, k_off + g)),
                  pl.BlockSpec((1, s, hd), lambda b, g, i: (b, 0, v_off + g))],
        out_specs=pl.BlockSpec((1, blk, rep * hd), lambda b, g, i: (b, i, g)),
        out_shape=jax.ShapeDtypeStruct((bsz, s, heads * hd), BF16),
        scratch_shapes=[pltpu.VMEM((n_blk, hd), F32)],
        compiler_params=_cparams(("parallel", "parallel", "arbitrary")),
        name="moba_attention",
    )(qkv, qkv, qkv)


def _router_body(x_ref, w_ref, o_ref, *, n_experts):
    logits = jnp.dot(x_ref[...], w_ref[...], preferred_element_type=F32)
    lane = lax.broadcasted_iota(jnp.int32, logits.shape, 1)
    lane_f = lane.astype(F32)
    big = float(logits.shape[1])
    lg = jnp.where(lane < n_experts, logits, -jnp.inf)
    m1 = jnp.max(lg, axis=-1, keepdims=True)
    i1 = jnp.min(jnp.where(lg == m1, lane_f, big), axis=-1, keepdims=True)
    lg2 = jnp.where(lane_f == i1, -jnp.inf, lg)
    m2 = jnp.max(lg2, axis=-1, keepdims=True)
    i2 = jnp.min(jnp.where(lg2 == m2, lane_f, big), axis=-1, keepdims=True)
    t = jnp.exp(m2 - m1)
    g1 = 1.0 / (1.0 + t)
    g2 = t / (1.0 + t)
    o_ref[...] = jnp.where(lane == 0, i1,
                           jnp.where(lane == 1, i2, jnp.where(lane == 2, g1, jnp.where(lane == 3, g2, 0.0))))


def router_top2(x, w_pad, *, tm, n_experts):
    m, k = x.shape
    n = w_pad.shape[1]
    return pl.pallas_call(
        functools.partial(_router_body, n_experts=n_experts),
        grid=(m // tm,),
        in_specs=[pl.BlockSpec((tm, k), lambda i: (i, 0)),
                  pl.BlockSpec((k, n), lambda i: (0, 0))],
        out_specs=pl.BlockSpec((tm, n), lambda i: (i, 0)),
        out_shape=jax.ShapeDtypeStruct((m, n), F32),
        compiler_params=_cparams(("parallel",)),
        name="router_top2",
    )(x, w_pad)


def _row_copy(src_hbm, dst_vmem, sem, src_row, dst_row):
    return pltpu.make_async_copy(src_hbm.at[pl.ds(src_row, 1), :], dst_vmem.at[pl.ds(dst_row, 1), :], sem)


def _gather_body(src_ref, nused_ref, h_hbm, o_ref, buf, sem, *, tm):
    i = pl.program_id(0)
    used = i * tm < nused_ref[0]

    @pl.when(jnp.logical_not(used))
    def _():
        o_ref[...] = jnp.zeros_like(o_ref)

    @pl.when(used)
    def _():
        def issue(r, c):
            _row_copy(h_hbm, buf, sem, src_ref[i * tm + r], r).start()
            return c

        lax.fori_loop(0, tm, issue, 0)

        def drain(r, c):
            _row_copy(h_hbm, buf, sem, 0, r).wait()
            return c

        lax.fori_loop(0, tm, drain, 0)
        o_ref[...] = buf[...].astype(o_ref.dtype)


def gather_rows(h, src_tok, n_used_rows, *, cap, tm):
    d = h.shape[1]

    def out_map(i, src, nused):
        return (i, 0)

    return pl.pallas_call(
        functools.partial(_gather_body, tm=tm),
        grid_spec=pltpu.PrefetchScalarGridSpec(
            num_scalar_prefetch=2,
            grid=(cap // tm,),
            in_specs=[pl.BlockSpec(memory_space=pl.ANY)],
            out_specs=pl.BlockSpec((tm, d), out_map),
            scratch_shapes=[pltpu.VMEM((tm, d), F32), pltpu.SemaphoreType.DMA(())]),
        out_shape=jax.ShapeDtypeStruct((cap, d), BF16),
        compiler_params=_cparams(("arbitrary",)),
        name="gather_rows",
    )(src_tok, n_used_rows, h)


def _grouped_tile(i, n_rows, tm):
    return jnp.minimum(i, (n_rows[0] - 1) // tm)


def _grouped_fresh(i, sub_e, ratio):
    prev = sub_e[(jnp.maximum(i, 1) - 1) * ratio]
    return jnp.logical_or(i == 0, sub_e[i * ratio] != prev)


def _moe_up_body(sub_e, sub_n, n_rows, x_ref, wg_ref, wu_ref, o_ref, wgb_ref, wub_ref, *, tm, sub):
    i = pl.program_id(1)
    ratio = tm // sub
    used = i * tm < n_rows[0]

    @pl.when(jnp.logical_and(used, _grouped_fresh(i, sub_e, ratio)))
    def _():
        wgb_ref[...] = wg_ref[0].astype(BF16)
        wub_ref[...] = wu_ref[0].astype(BF16)

    for s in range(ratio):
        live = jnp.logical_and(used, sub_n[i * ratio + s] > 0)

        @pl.when(live)
        def _():
            x = x_ref[s * sub:(s + 1) * sub, :]
            g = jnp.dot(x, wgb_ref[...], preferred_element_type=F32)
            u = jnp.dot(x, wub_ref[...], preferred_element_type=F32)
            o_ref[s * sub:(s + 1) * sub, :] = (_silu(g) * u).astype(o_ref.dtype)

        @pl.when(jnp.logical_not(live))
        def _():
            o_ref[s * sub:(s + 1) * sub, :] = jnp.zeros((sub, o_ref.shape[1]), o_ref.dtype)


def moe_swiglu_up(xs, w_gu, sub_e, sub_n, n_rows, *, tm, tf, sub):
    cap, d = xs.shape
    f = w_gu.shape[2] // 2
    nf = f // tf
    ratio = tm // sub

    def x_map(j, i, se, sn, nr):
        return (_grouped_tile(i, nr, tm), 0)

    def wg_map(j, i, se, sn, nr):
        return (se[_grouped_tile(i, nr, tm) * ratio], 0, j)

    def wu_map(j, i, se, sn, nr):
        return (se[_grouped_tile(i, nr, tm) * ratio], 0, nf + j)

    def o_map(j, i, se, sn, nr):
        return (i, j)

    return pl.pallas_call(
        functools.partial(_moe_up_body, tm=tm, sub=sub),
        grid_spec=pltpu.PrefetchScalarGridSpec(
            num_scalar_prefetch=3,
            grid=(nf, cap // tm),
            in_specs=[pl.BlockSpec((tm, d), x_map),
                      pl.BlockSpec((1, d, tf), wg_map),
                      pl.BlockSpec((1, d, tf), wu_map)],
            out_specs=pl.BlockSpec((tm, tf), o_map),
            scratch_shapes=[pltpu.VMEM((d, tf), BF16), pltpu.VMEM((d, tf), BF16)]),
        out_shape=jax.ShapeDtypeStruct((cap, f), BF16),
        compiler_params=_cparams(("arbitrary", "arbitrary")),
        name="moe_swiglu_up",
    )(sub_e, sub_n, n_rows, xs, w_gu, w_gu)


def _moe_down_body(sub_e, sub_n, n_rows, x_ref, w_ref, o_ref, wb_ref, *, tm, sub):
    i = pl.program_id(1)
    ratio = tm // sub
    used = i * tm < n_rows[0]

    @pl.when(jnp.logical_and(used, _grouped_fresh(i, sub_e, ratio)))
    def _():
        wb_ref[...] = w_ref[0].astype(BF16)

    for s in range(ratio):
        live = jnp.logical_and(used, sub_n[i * ratio + s] > 0)

        @pl.when(live)
        def _():
            o_ref[s * sub:(s + 1) * sub, :] = jnp.dot(
                x_ref[s * sub:(s + 1) * sub, :], wb_ref[...], preferred_element_type=F32)

        @pl.when(jnp.logical_not(live))
        def _():
            o_ref[s * sub:(s + 1) * sub, :] = jnp.zeros((sub, o_ref.shape[1]), o_ref.dtype)


def moe_down(act, w_down, sub_e, sub_n, n_rows, *, tm, tn, sub):
    cap, f = act.shape
    d = w_down.shape[2]
    ratio = tm // sub

    def x_map(j, i, se, sn, nr):
        return (_grouped_tile(i, nr, tm), 0)

    def w_map(j, i, se, sn, nr):
        return (se[_grouped_tile(i, nr, tm) * ratio], 0, j)

    def o_map(j, i, se, sn, nr):
        return (i, j)

    return pl.pallas_call(
        functools.partial(_moe_down_body, tm=tm, sub=sub),
        grid_spec=pltpu.PrefetchScalarGridSpec(
            num_scalar_prefetch=3,
            grid=(d // tn, cap // tm),
            in_specs=[pl.BlockSpec((tm, f), x_map),
                      pl.BlockSpec((1, f, tn), w_map)],
            out_specs=pl.BlockSpec((tm, tn), o_map),
            scratch_shapes=[pltpu.VMEM((f, tn), BF16)]),
        out_shape=jax.ShapeDtypeStruct((cap, d), F32),
        compiler_params=_cparams(("arbitrary", "arbitrary")),
        name="moe_down",
    )(sub_e, sub_n, n_rows, act, w_down)


def _combine_body(dest_ref, y_hbm, gates_ref, res_ref, g_ref, b_ref, o_ref, buf_a, buf_b, sem, *, tm):
    i = pl.program_id(0)

    def issue(t, c):
        _row_copy(y_hbm, buf_a, sem.at[0], dest_ref[2 * (i * tm + t)], t).start()
        _row_copy(y_hbm, buf_b, sem.at[1], dest_ref[2 * (i * tm + t) + 1], t).start()
        return c

    lax.fori_loop(0, tm, issue, 0)

    def drain(t, c):
        _row_copy(y_hbm, buf_a, sem.at[0], 0, t).wait()
        _row_copy(y_hbm, buf_b, sem.at[1], 0, t).wait()
        return c

    lax.fori_loop(0, tm, drain, 0)
    gates = gates_ref[...]
    y = buf_a[...] * gates[:, 2:3] + buf_b[...] * gates[:, 3:4]
    o_ref[...] = _layer_norm_rows(DEEPNORM_ALPHA * res_ref[...] + y, g_ref[...], b_ref[...])


def moe_combine_post_norm(y_rows, dest, routing, res, g, b, *, tm):
    n_tok, d = res.shape
    lanes = routing.shape[1]
    return pl.pallas_call(
        functools.partial(_combine_body, tm=tm),
        grid_spec=pltpu.PrefetchScalarGridSpec(
            num_scalar_prefetch=1,
            grid=(n_tok // tm,),
            in_specs=[pl.BlockSpec(memory_space=pl.ANY),
                      pl.BlockSpec((tm, lanes), lambda i, dst: (i, 0)),
                      pl.BlockSpec((tm, d), lambda i, dst: (i, 0)),
                      pl.BlockSpec((1, d), lambda i, dst: (0, 0)),
                      pl.BlockSpec((1, d), lambda i, dst: (0, 0))],
            out_specs=pl.BlockSpec((tm, d), lambda i, dst: (i, 0)),
            scratch_shapes=[pltpu.VMEM((tm, d), F32), pltpu.VMEM((tm, d), F32),
                            pltpu.SemaphoreType.DMA((2,))]),
        out_shape=jax.ShapeDtypeStruct((n_tok, d), F32),
        compiler_params=_cparams(("arbitrary",)),
        name="moe_combine_post_norm",
    )(dest, y_rows, routing, res, g.reshape(1, d), b.reshape(1, d))


def _group_rows(seg, n_seg, tile, sub):
    n_asg = seg.shape[0]
    cap = n_asg + n_seg * tile
    onehot = (seg[:, None] == jnp.arange(n_seg, dtype=jnp.int32)[None, :]).astype(jnp.int32)
    csum = jnp.cumsum(onehot, axis=0)
    rank = jnp.sum(csum * onehot, axis=1) - 1
    counts = csum[-1]
    padded = (counts + tile - 1) // tile * tile
    pad_end = jnp.cumsum(padded)
    pad_start = pad_end - padded
    dest = (pad_start[seg] + rank).astype(jnp.int32)
    src = jnp.zeros((cap,), jnp.int32).at[dest].set(jnp.arange(n_asg, dtype=jnp.int32) // TOP_K)
    sub_start = jnp.arange(cap // sub, dtype=jnp.int32) * sub
    sub_e = jnp.minimum(jnp.searchsorted(pad_end, sub_start, side="right"), n_seg - 1).astype(jnp.int32)
    sub_n = jnp.clip(counts[sub_e] - (sub_start - pad_start[sub_e]), 0, sub).astype(jnp.int32)
    n_rows = pad_end[-1:].astype(jnp.int32)
    return dest, src, sub_e, sub_n, n_rows, cap


def _gla_layer(h, hb, w_in, w_gate_up, b_gate, g_norm, w_o, ln_g, ln_b):
    bsz, s, d = h.shape
    dk = (d // 2) // GLA_HEADS
    dv = d // GLA_HEADS
    n_main = 2 * GLA_HEADS * dk + 2 * GLA_HEADS * dv
    x2 = hb.reshape(bsz * s, d)
    proj = matmul(x2, w_in[:, :n_main].astype(BF16), tm=_pick(bsz * s, (1024, 512, 256, 128, 64)),
                  tn=_pick(n_main, (1024, 512, 256, 128)), out_dtype=F32)
    w_z = jnp.pad(w_in[:, n_main:], ((0, 0), (0, LANES - GLA_GATE_RANK))).astype(BF16)
    z = matmul(x2, w_z, tm=_pick(bsz * s, (1024, 512, 256, 128, 64)), tn=LANES, out_dtype=F32)
    wgu = jnp.pad(w_gate_up, ((0, LANES - GLA_GATE_RANK), (0, 0))).astype(BF16)
    wgu = wgu.reshape(LANES, GLA_HEADS, dk).transpose(1, 0, 2)
    og = gla_core(proj.reshape(bsz, s, n_main), z.reshape(bsz, s, LANES), wgu,
                  b_gate.reshape(GLA_HEADS, 1, dk), g_norm.reshape(1, dv),
                  heads=GLA_HEADS, dk=dk, dv=dv, step=min(GLA_STEP_TOKENS, s))
    out, outb = matmul_post_norm(og.reshape(bsz * s, d), w_o.astype(BF16), h.reshape(bsz * s, d), ln_g, ln_b,
                                 tm=_pick(bsz * s, (256, 128, 64)), tk=d)
    return out.reshape(bsz, s, d), outb.reshape(bsz, s, d)


def _moba_layer(h, hb, w_q, w_kv, w_o, ln_g, ln_b):
    bsz, s, d = h.shape
    hd = d // MOBA_HEADS
    w_all = jnp.concatenate([w_q, w_kv], axis=1).astype(BF16)
    n_all = w_all.shape[1]
    tn = _pick(math.gcd(d, n_all), (1024, 512, 256, 128))
    qkv = matmul(hb.reshape(bsz * s, d), w_all, tm=_pick(bsz * s, (1024, 512, 256, 128, 64)), tn=tn,
                 out_dtype=BF16, n_scaled=d // tn, scale=hd ** -0.5)
    att = moba_attention(qkv.reshape(bsz, s, n_all), heads=MOBA_HEADS, kv_heads=MOBA_KV_HEADS, hd=hd,
                         blk=MOBA_BLOCK, topk=MOBA_TOPK)
    out, outb = matmul_post_norm(att.reshape(bsz * s, d), w_o.astype(BF16), h.reshape(bsz * s, d), ln_g, ln_b,
                                 tm=_pick(bsz * s, (256, 128, 64)), tk=d)
    return out.reshape(bsz, s, d), outb.reshape(bsz, s, d)


def _dense_ffn(h, hb, w_gu, w_down, ln_g, ln_b):
    bsz, s, d = h.shape
    f = w_down.shape[0]
    f_pad = -(-f // 512) * 512
    wg = jnp.pad(w_gu[:, :f], ((0, 0), (0, f_pad - f))).astype(BF16)
    wu = jnp.pad(w_gu[:, f:], ((0, 0), (0, f_pad - f))).astype(BF16)
    wd = jnp.pad(w_down, ((0, f_pad - f), (0, 0))).astype(BF16)
    act = swiglu_up(hb.reshape(bsz * s, d), wg, wu, tm=_pick(bsz * s, (1024, 512, 256, 128, 64)), tf=512)
    tk = _pick(f_pad, (1408, 1024, 512))
    out, outb = matmul_post_norm(act, wd, h.reshape(bsz * s, d), ln_g, ln_b,
                                 tm=_pick(bsz * s, (512, 256, 128, 64)), tk=tk)
    return out.reshape(bsz, s, d), outb.reshape(bsz, s, d)


def _moe_ffn(h, hb, w_router, w_gu, w_down, ln_g, ln_b):
    bsz, s, d = h.shape
    n_tok = bsz * s
    h2 = h.reshape(n_tok, d)
    w_r = jnp.pad(w_router, ((0, 0), (0, LANES - N_EXPERTS))).astype(BF16)
    routing = router_top2(hb.reshape(n_tok, d), w_r, tm=_pick(n_tok, (1024, 512, 256, 128, 64)),
                          n_experts=N_EXPERTS)
    seg = routing[:, :TOP_K].astype(jnp.int32).reshape(-1)
    tile = min(MOE_TILE, n_tok)
    sub = min(MOE_SUB, tile)
    dest, src, sub_e, sub_n, n_rows, cap = _group_rows(seg, N_EXPERTS, tile, sub)
    xs = gather_rows(h2, src, n_rows, cap=cap, tm=min(GATHER_TILE, tile))
    f = w_down.shape[1]
    act = moe_swiglu_up(xs, w_gu, sub_e, sub_n, n_rows, tm=tile, tf=_pick(f, (512, 256, 128)), sub=sub)
    y_rows = moe_down(act, w_down, sub_e, sub_n, n_rows, tm=sub, tn=_pick(d, (512, 256, 128)), sub=sub)
    out = moe_combine_post_norm(y_rows, dest, routing, h2, ln_g, ln_b, tm=_pick(n_tok, (256, 128, 64)))
    return out.reshape(bsz, s, d)


def kernel(x, w_in_a, w_gate_up_a, b_gate_a, g_norm_a, w_o_a, w_kv_shared, w_q_b, w_o_b,
           ln_mix_g, ln_mix_b, w_gu_dense, w_down_dense, w_router, w_gu_moe, w_down_moe,
           ln_ffn_g, ln_ffn_b):
    h, hb = _gla_layer(x, x.astype(BF16), w_in_a[0], w_gate_up_a[0], b_gate_a[0], g_norm_a[0], w_o_a[0],
                       ln_mix_g[0], ln_mix_b[0])
    h, hb = _dense_ffn(h, hb, w_gu_dense[0], w_down_dense[0], ln_ffn_g[0], ln_ffn_b[0])
    h, hb = _moba_layer(h, hb, w_q_b[0], w_kv_shared, w_o_b[0], ln_mix_g[1], ln_mix_b[1])
    return _moe_ffn(h, hb, w_router[0], w_gu_moe[0], w_down_moe[0], ln_ffn_g[1], ln_ffn_b[1])
```

```python
import functools
import math

import jax
import jax.numpy as jnp
from jax import lax
from jax.experimental import pallas as pl
from jax.experimental.pallas import tpu as pltpu

F32 = jnp.float32
BF16 = jnp.bfloat16

DEPTH = 2
GLA_HEADS = 4
GLA_GATE_RANK = 16
GLA_TAU = 16.0
GLA_CHUNK = 64
MOBA_HEADS = 16
MOBA_KV_HEADS = 4
MOBA_BLOCK = 256
MOBA_TOPK = 3
N_EXPERTS = 8
TOP_K = 2
LN_EPS = 1e-5
RMS_EPS = 1e-6
DEEPNORM_ALPHA = (2 * DEPTH) ** 0.25

LANES = 128
VMEM_LIMIT_BYTES = 56 * 1024 * 1024

GLA_STEP_TOKENS = 512
MOE_TILE = 1024
MOE_SUB = 256
GATHER_TILE = 512


def _cparams(sem):
    return pltpu.CompilerParams(dimension_semantics=sem, vmem_limit_bytes=VMEM_LIMIT_BYTES)


def _pick(n, prefs):
    for p in prefs:
        if n % p == 0:
            return p
    return n


def _mm_body(x_ref, w_ref, o_ref, *, n_scaled, scale):
    acc = jnp.dot(x_ref[...], w_ref[...], preferred_element_type=F32)
    if n_scaled:
        acc = acc * jnp.where(pl.program_id(0) < n_scaled, scale, 1.0).astype(F32)
    o_ref[...] = acc.astype(o_ref.dtype)


def matmul(x, w, *, tm, tn, out_dtype, n_scaled=0, scale=1.0):
    m, k = x.shape
    n = w.shape[1]
    return pl.pallas_call(
        functools.partial(_mm_body, n_scaled=n_scaled, scale=scale),
        grid=(n // tn, m // tm),
        in_specs=[pl.BlockSpec((tm, k), lambda j, i: (i, 0)),
                  pl.BlockSpec((k, tn), lambda j, i: (0, j))],
        out_specs=pl.BlockSpec((tm, tn), lambda j, i: (i, j)),
        out_shape=jax.ShapeDtypeStruct((m, n), out_dtype),
        compiler_params=_cparams(("parallel", "parallel")),
        name="matmul",
    )(x, w)


def _layer_norm_rows(t, g, b):
    mu = jnp.mean(t, axis=-1, keepdims=True)
    d = t - mu
    var = jnp.mean(d * d, axis=-1, keepdims=True)
    return d * lax.rsqrt(var + LN_EPS) * g + b


def _mm_ln_body(x_ref, w_ref, res_ref, g_ref, b_ref, o_ref, ob_ref, *rest, nk, packed):
    op_ref = rest[0] if packed else None
    acc_ref = rest[-1]
    part = jnp.dot(x_ref[...], w_ref[...], preferred_element_type=F32)

    def finish(y):
        out = _layer_norm_rows(DEEPNORM_ALPHA * res_ref[...] + y, g_ref[...], b_ref[...])
        o_ref[...] = out
        ob_ref[...] = out.astype(BF16)
        if packed:
            op_ref[...] = _pack_bf16_pairs(out)

    if nk == 1:
        finish(part)
        return
    k = pl.program_id(1)

    @pl.when(k == 0)
    def _():
        acc_ref[...] = part

    @pl.when(jnp.logical_and(k > 0, k < nk - 1))
    def _():
        acc_ref[...] += part

    @pl.when(k == nk - 1)
    def _():
        finish(acc_ref[...] + part)


def matmul_post_norm(x, w, res, g, b, *, tm, tk, packed=False):
    m, k = x.shape
    d = w.shape[1]
    nk = k // tk
    out_specs = [pl.BlockSpec((tm, d), lambda i, kk: (i, 0)), pl.BlockSpec((tm, d), lambda i, kk: (i, 0))]
    out_shape = [jax.ShapeDtypeStruct((m, d), F32), jax.ShapeDtypeStruct((m, d), BF16)]
    if packed:
        out_specs.append(pl.BlockSpec((tm, d // 2), lambda i, kk: (i, 0)))
        out_shape.append(jax.ShapeDtypeStruct((m, d // 2), jnp.uint32))
    return pl.pallas_call(
        functools.partial(_mm_ln_body, nk=nk, packed=packed),
        grid=(m // tm, nk),
        in_specs=[pl.BlockSpec((tm, tk), lambda i, kk: (i, kk)),
                  pl.BlockSpec((tk, d), lambda i, kk: (kk, 0)),
                  pl.BlockSpec((tm, d), lambda i, kk: (i, 0)),
                  pl.BlockSpec((1, d), lambda i, kk: (0, 0)),
                  pl.BlockSpec((1, d), lambda i, kk: (0, 0))],
        out_specs=out_specs,
        out_shape=out_shape,
        scratch_shapes=[pltpu.VMEM((tm, d), F32)],
        compiler_params=_cparams(("parallel", "arbitrary")),
        name="matmul_post_norm",
    )(x, w, res, g.reshape(1, d), b.reshape(1, d))


def _silu(x):
    return x / (1.0 + jnp.exp(-x))


def _swiglu_body(x_ref, wg_ref, wu_ref, o_ref):
    x = x_ref[...]
    g = jnp.dot(x, wg_ref[...], preferred_element_type=F32)
    u = jnp.dot(x, wu_ref[...], preferred_element_type=F32)
    o_ref[...] = (_silu(g) * u).astype(o_ref.dtype)


def swiglu_up(x, wg, wu, *, tm, tf):
    m, k = x.shape
    f = wg.shape[1]
    return pl.pallas_call(
        _swiglu_body,
        grid=(f // tf, m // tm),
        in_specs=[pl.BlockSpec((tm, k), lambda j, i: (i, 0)),
                  pl.BlockSpec((k, tf), lambda j, i: (0, j)),
                  pl.BlockSpec((k, tf), lambda j, i: (0, j))],
        out_specs=pl.BlockSpec((tm, tf), lambda j, i: (i, j)),
        out_shape=jax.ShapeDtypeStruct((m, f), BF16),
        compiler_params=_cparams(("parallel", "parallel")),
        name="swiglu_up",
    )(x, wg, wu)


def _gla_body(q_ref, k_ref, v_ref, r_ref, z_ref, wgu_ref, bg_ref, gn_ref, o_ref, state_ref,
              *, chunk, n_chunks, dk):
    @pl.when(pl.program_id(2) == 0)
    def _():
        state_ref[...] = jnp.zeros_like(state_ref)

    row = lax.broadcasted_iota(jnp.int32, (chunk, chunk), 0)
    col = lax.broadcasted_iota(jnp.int32, (chunk, chunk), 1)
    causal = col <= row
    tril = jnp.where(causal, 1.0, 0.0).astype(BF16)
    nt = (((1,), (1,)), ((), ()))
    tn = (((0,), (0,)), ((), ()))

    def body(c, carry):
        sl = pl.ds(pl.multiple_of(c * chunk, chunk), chunk)
        gate = jnp.dot(z_ref[0, sl, :].astype(BF16), wgu_ref[0], preferred_element_type=F32) + bg_ref[0]
        log_a = (jnp.minimum(gate, 0.0) - jnp.log(1.0 + jnp.exp(-jnp.abs(gate)))) / GLA_TAU
        la_hi = log_a.astype(BF16)
        la_lo = (log_a - la_hi.astype(F32)).astype(BF16)
        b_cum = (jnp.dot(tril, la_hi, preferred_element_type=F32)
                 + jnp.dot(tril, la_lo, preferred_element_type=F32))
        b_last = b_cum[chunk - 1:chunk, :]
        q = q_ref[0, sl, :] * (dk ** -0.5)
        k = k_ref[0, sl, :]
        q_dec = (q * jnp.exp(b_cum)).astype(BF16)
        k_inv = (k * jnp.exp(-b_cum)).astype(BF16)
        k_end = (k * jnp.exp(b_last - b_cum)).astype(BF16)
        v = v_ref[0, sl, :].astype(BF16)
        att = lax.dot_general(q_dec, k_inv, nt, preferred_element_type=F32)
        att = jnp.where(causal, att, 0.0).astype(BF16)
        o = jnp.dot(att, v, preferred_element_type=F32)
        state_t = state_ref[...]
        o = o + lax.dot_general(q_dec, state_t.astype(BF16), nt, preferred_element_type=F32)
        state_ref[...] = (jnp.exp(b_last) * state_t
                          + lax.dot_general(v, k_end, tn, preferred_element_type=F32))
        o = o * lax.rsqrt(jnp.mean(o * o, axis=-1, keepdims=True) + RMS_EPS) * gn_ref[...]
        o_ref[0, sl, :] = (o * _silu(r_ref[0, sl, :])).astype(o_ref.dtype)
        return carry

    lax.fori_loop(0, n_chunks, body, 0)


def gla_core(proj, z, wgu, bg, gn, *, heads, dk, dv, step):
    bsz, s, _ = proj.shape
    zr = z.shape[-1]
    k_off = heads
    v_off = 2 * heads * dk // dv
    r_off = v_off + heads
    return pl.pallas_call(
        functools.partial(_gla_body, chunk=GLA_CHUNK, n_chunks=step // GLA_CHUNK, dk=dk),
        grid=(bsz, heads, s // step),
        in_specs=[pl.BlockSpec((1, step, dk), lambda b, h, t: (b, t, h)),
                  pl.BlockSpec((1, step, dk), lambda b, h, t: (b, t, k_off + h)),
                  pl.BlockSpec((1, step, dv), lambda b, h, t: (b, t, v_off + h)),
                  pl.BlockSpec((1, step, dv), lambda b, h, t: (b, t, r_off + h)),
                  pl.BlockSpec((1, step, zr), lambda b, h, t: (b, t, 0)),
                  pl.BlockSpec((1, zr, dk), lambda b, h, t: (h, 0, 0)),
                  pl.BlockSpec((1, 1, dk), lambda b, h, t: (h, 0, 0)),
                  pl.BlockSpec((1, dv), lambda b, h, t: (0, 0))],
        out_specs=pl.BlockSpec((1, step, dv), lambda b, h, t: (b, t, h)),
        out_shape=jax.ShapeDtypeStruct((bsz, s, heads * dv), BF16),
        scratch_shapes=[pltpu.VMEM((dv, dk), F32)],
        compiler_params=_cparams(("parallel", "parallel", "arbitrary")),
        name="gla_core",
    )(proj, proj, proj, proj, z, wgu, bg, gn)


MASK_PENALTY = -3.0e38


def _moba_body(q_ref, k_ref, v_ref, o_ref, kmean_ref, kaug_ref, vt_ref, rhs_ref, m_ref, l_ref, acc_ref,
               *, blk, n_blk, rep, hd, topk):
    i = pl.program_id(2)
    nq = rep * blk

    @pl.when(i == 0)
    def _():
        lane = lax.broadcasted_iota(jnp.int32, (blk, hd), 1)
        for j in range(n_blk):
            kj = k_ref[0, j * blk:(j + 1) * blk, :]
            kaug_ref[j, :, :hd] = kj
            kaug_ref[j, :, hd:] = jnp.where(lane == j, 1.0, 0.0).astype(BF16)
            kmean_ref[j:j + 1, :] = jnp.sum(kj.astype(F32), axis=0, keepdims=True) / blk
            vt_ref[j] = v_ref[0, j * blk:(j + 1) * blk, :].astype(F32).T.astype(BF16)
        rhs_ref[hd + n_blk:, :] = jnp.zeros((hd - n_blk, nq), BF16)

    qt = jnp.concatenate([q_ref[0, :, r * hd:(r + 1) * hd].astype(F32).T for r in range(rep)], axis=1)
    qt = qt.astype(BF16)
    gate = jnp.dot(kmean_ref[...].astype(BF16), qt, preferred_element_type=F32)
    blk_id = lax.broadcasted_iota(jnp.int32, (n_blk, nq), 0)
    past = blk_id < i
    gate = jnp.where(past, gate, -jnp.inf)
    beaten = jnp.zeros((n_blk, nq), jnp.int32)
    for c in range(n_blk):
        gc = gate[c:c + 1, :]
        beaten = beaten + jnp.where(gc > gate, 1, jnp.where(gc == gate, jnp.where(c < blk_id, 1, 0), 0))
    penalty = jnp.where(past, jnp.where(beaten < topk, 0.0, MASK_PENALTY),
                        jnp.where(blk_id == i, 0.0, MASK_PENALTY))
    rhs_ref[:hd, :] = qt
    rhs_ref[hd:hd + n_blk, :] = penalty.astype(BF16)

    s = jnp.dot(kaug_ref[i], rhs_ref[...], preferred_element_type=F32)
    key = lax.broadcasted_iota(jnp.int32, (blk, nq), 0)
    qpos = lax.rem(lax.broadcasted_iota(jnp.int32, (blk, nq), 1), blk)
    s = jnp.where(key <= qpos, s, -jnp.inf)
    m0 = jnp.max(s, axis=0, keepdims=True)
    p = jnp.exp(s - m0)
    m_ref[...] = m0
    l_ref[...] = jnp.sum(p, axis=0, keepdims=True)
    acc_ref[...] = jnp.dot(vt_ref[i], p.astype(BF16), preferred_element_type=F32)

    def body(j, carry):
        sj = jnp.dot(kaug_ref[j], rhs_ref[...], preferred_element_type=F32)
        m_old = m_ref[...]
        m_new = jnp.maximum(m_old, jnp.max(sj, axis=0, keepdims=True))
        a = jnp.exp(m_old - m_new)
        pj = jnp.exp(sj - m_new)
        l_ref[...] = a * l_ref[...] + jnp.sum(pj, axis=0, keepdims=True)
        acc_ref[...] = a * acc_ref[...] + jnp.dot(vt_ref[j], pj.astype(BF16), preferred_element_type=F32)
        m_ref[...] = m_new
        return carry

    lax.fori_loop(0, i, body, 0)
    out = acc_ref[...] * (1.0 / l_ref[...])
    for r in range(rep):
        o_ref[0, :, r * hd:(r + 1) * hd] = out[:, r * blk:(r + 1) * blk].T.astype(o_ref.dtype)


def moba_attention(qkv, *, heads, kv_heads, hd, blk, topk):
    bsz, s, _ = qkv.shape
    rep = heads // kv_heads
    n_blk = s // blk
    nq = rep * blk
    assert n_blk <= hd and n_blk % 16 == 0
    k_off = heads
    v_off = heads + kv_heads
    return pl.pallas_call(
        functools.partial(_moba_body, blk=blk, n_blk=n_blk, rep=rep, hd=hd, topk=topk),
        grid=(bsz, kv_heads, n_blk),
        in_specs=[pl.BlockSpec((1, blk, rep * hd), lambda b, g, i: (b, i, g)),
                  pl.BlockSpec((1, s, hd), lambda b, g, i: (b, 0, k_off + g)),
                  pl.BlockSpec((1, s, hd), lambda b, g, i: (b, 0, v_off + g))],
        out_specs=pl.BlockSpec((1, blk, rep * hd), lambda b, g, i: (b, i, g)),
        out_shape=jax.ShapeDtypeStruct((bsz, s, heads * hd), BF16),
        scratch_shapes=[pltpu.VMEM((n_blk, hd), F32),
                        pltpu.VMEM((n_blk, blk, 2 * hd), BF16),
                        pltpu.VMEM((n_blk, hd, blk), BF16),
                        pltpu.VMEM((2 * hd, nq), BF16),
                        pltpu.VMEM((1, nq), F32),
                        pltpu.VMEM((1, nq), F32),
                        pltpu.VMEM((hd, nq), F32)],
        compiler_params=_cparams(("parallel", "parallel", "arbitrary")),
        name="moba_attention",
    )(qkv, qkv, qkv)


def _router_body(x_ref, w_ref, o_ref, *, n_experts):
    logits = jnp.dot(x_ref[...], w_ref[...], preferred_element_type=F32)
    lane = lax.broadcasted_iota(jnp.int32, logits.shape, 1)
    lane_f = lane.astype(F32)
    big = float(logits.shape[1])
    lg = jnp.where(lane < n_experts, logits, -jnp.inf)
    m1 = jnp.max(lg, axis=-1, keepdims=True)
    i1 = jnp.min(jnp.where(lg == m1, lane_f, big), axis=-1, keepdims=True)
    lg2 = jnp.where(lane_f == i1, -jnp.inf, lg)
    m2 = jnp.max(lg2, axis=-1, keepdims=True)
    i2 = jnp.min(jnp.where(lg2 == m2, lane_f, big), axis=-1, keepdims=True)
    t = jnp.exp(m2 - m1)
    g1 = 1.0 / (1.0 + t)
    g2 = t / (1.0 + t)
    o_ref[...] = jnp.where(lane == 0, i1,
                           jnp.where(lane == 1, i2, jnp.where(lane == 2, g1, jnp.where(lane == 3, g2, 0.0))))


def router_top2(x, w_pad, *, tm, n_experts):
    m, k = x.shape
    n = w_pad.shape[1]
    return pl.pallas_call(
        functools.partial(_router_body, n_experts=n_experts),
        grid=(m // tm,),
        in_specs=[pl.BlockSpec((tm, k), lambda i: (i, 0)),
                  pl.BlockSpec((k, n), lambda i: (0, 0))],
        out_specs=pl.BlockSpec((tm, n), lambda i: (i, 0)),
        out_shape=jax.ShapeDtypeStruct((m, n), F32),
        compiler_params=_cparams(("parallel",)),
        name="router_top2",
    )(x, w_pad)


def _row_copy(src_hbm, dst_vmem, sem, src_row, dst_row):
    return pltpu.make_async_copy(src_hbm.at[pl.ds(src_row, 1), :], dst_vmem.at[pl.ds(dst_row, 1), :], sem)


def _pack_bf16_pairs(x):
    half = x.shape[1] // 2
    lo = pltpu.bitcast(x[:, :half].astype(BF16).astype(F32), jnp.uint32)
    hi = pltpu.bitcast(x[:, half:].astype(BF16).astype(F32), jnp.uint32)
    return (hi & jnp.uint32(0xFFFF0000)) | (lo >> 16)


def _unpack_bf16_pairs(w):
    lo = pltpu.bitcast(w << 16, F32).astype(BF16)
    hi = pltpu.bitcast(w & jnp.uint32(0xFFFF0000), F32).astype(BF16)
    return jnp.concatenate([lo, hi], axis=1)


def _gather_body(src_ref, nrows_ref, hp_hbm, o_ref, buf, sem, *, tm, unroll):
    i = pl.program_id(0)

    def issue_tile(t, slot):
        def issue(g, c):
            for u in range(unroll):
                r = g * unroll + u
                _row_copy(hp_hbm, buf.at[slot], sem.at[slot], src_ref[t * tm + r], r).start()
            return c

        lax.fori_loop(0, tm // unroll, issue, 0)

    @pl.when(i == 0)
    def _():
        issue_tile(0, 0)

    nxt = i + 1

    @pl.when(jnp.logical_and(nxt < pl.num_programs(0), nxt * tm < nrows_ref[0]))
    def _():
        issue_tile(nxt, nxt % 2)

    used = i * tm < nrows_ref[0]

    @pl.when(jnp.logical_not(used))
    def _():
        o_ref[...] = jnp.zeros_like(o_ref)

    @pl.when(used)
    def _():
        slot = i % 2
        pltpu.make_async_copy(hp_hbm.at[pl.ds(0, tm), :], buf.at[slot], sem.at[slot]).wait()
        o_ref[...] = _unpack_bf16_pairs(buf[slot])


def gather_rows(hp, src_tok, n_used_rows, *, cap, tm):
    half = hp.shape[1]
    return pl.pallas_call(
        functools.partial(_gather_body, tm=tm, unroll=8),
        grid_spec=pltpu.PrefetchScalarGridSpec(
            num_scalar_prefetch=2,
            grid=(cap // tm,),
            in_specs=[pl.BlockSpec(memory_space=pl.ANY)],
            out_specs=pl.BlockSpec((tm, 2 * half), lambda i, src, nused: (i, 0)),
            scratch_shapes=[pltpu.VMEM((2, tm, half), jnp.uint32), pltpu.SemaphoreType.DMA((2,))]),
        out_shape=jax.ShapeDtypeStruct((cap, 2 * half), BF16),
        compiler_params=_cparams(("arbitrary",)),
        name="gather_rows",
    )(src_tok, n_used_rows, hp)


def _grouped_tile(i, n_rows, tm):
    return jnp.minimum(i, (n_rows[0] - 1) // tm)


def _grouped_fresh(i, sub_e, ratio):
    prev = sub_e[(jnp.maximum(i, 1) - 1) * ratio]
    return jnp.logical_or(i == 0, sub_e[i * ratio] != prev)


def _moe_up_body(sub_e, sub_n, n_rows, x_ref, wg_ref, wu_ref, o_ref, wgb_ref, wub_ref, *, tm, sub):
    i = pl.program_id(1)
    ratio = tm // sub
    used = i * tm < n_rows[0]

    @pl.when(jnp.logical_and(used, _grouped_fresh(i, sub_e, ratio)))
    def _():
        wgb_ref[...] = wg_ref[0].astype(BF16)
        wub_ref[...] = wu_ref[0].astype(BF16)

    for s in range(ratio):
        live = jnp.logical_and(used, sub_n[i * ratio + s] > 0)

        @pl.when(live)
        def _():
            x = x_ref[s * sub:(s + 1) * sub, :]
            g = jnp.dot(x, wgb_ref[...], preferred_element_type=F32)
            u = jnp.dot(x, wub_ref[...], preferred_element_type=F32)
            o_ref[s * sub:(s + 1) * sub, :] = (_silu(g) * u).astype(o_ref.dtype)

        @pl.when(jnp.logical_not(live))
        def _():
            o_ref[s * sub:(s + 1) * sub, :] = jnp.zeros((sub, o_ref.shape[1]), o_ref.dtype)


def moe_swiglu_up(xs, w_gu, sub_e, sub_n, n_rows, *, tm, tf, sub):
    cap, d = xs.shape
    f = w_gu.shape[2] // 2
    nf = f // tf
    ratio = tm // sub

    def x_map(j, i, se, sn, nr):
        return (_grouped_tile(i, nr, tm), 0)

    def wg_map(j, i, se, sn, nr):
        return (se[_grouped_tile(i, nr, tm) * ratio], 0, j)

    def wu_map(j, i, se, sn, nr):
        return (se[_grouped_tile(i, nr, tm) * ratio], 0, nf + j)

    def o_map(j, i, se, sn, nr):
        return (i, j)

    return pl.pallas_call(
        functools.partial(_moe_up_body, tm=tm, sub=sub),
        grid_spec=pltpu.PrefetchScalarGridSpec(
            num_scalar_prefetch=3,
            grid=(nf, cap // tm),
            in_specs=[pl.BlockSpec((tm, d), x_map),
                      pl.BlockSpec((1, d, tf), wg_map),
                      pl.BlockSpec((1, d, tf), wu_map)],
            out_specs=pl.BlockSpec((tm, tf), o_map),
            scratch_shapes=[pltpu.VMEM((d, tf), BF16), pltpu.VMEM((d, tf), BF16)]),
        out_shape=jax.ShapeDtypeStruct((cap, f), BF16),
        compiler_params=_cparams(("arbitrary", "arbitrary")),
        name="moe_swiglu_up",
    )(sub_e, sub_n, n_rows, xs, w_gu, w_gu)


def _moe_down_body(sub_e, sub_n, n_rows, x_ref, w_ref, o_ref, wb_ref, *, tm, sub):
    i = pl.program_id(1)
    ratio = tm // sub
    used = i * tm < n_rows[0]

    @pl.when(jnp.logical_and(used, _grouped_fresh(i, sub_e, ratio)))
    def _():
        wb_ref[...] = w_ref[0].astype(BF16)

    for s in range(ratio):
        live = jnp.logical_and(used, sub_n[i * ratio + s] > 0)

        @pl.when(live)
        def _():
            o_ref[s * sub:(s + 1) * sub, :] = jnp.dot(
                x_ref[s * sub:(s + 1) * sub, :], wb_ref[...], preferred_element_type=F32)

        @pl.when(jnp.logical_not(live))
        def _():
            o_ref[s * sub:(s + 1) * sub, :] = jnp.zeros((sub, o_ref.shape[1]), o_ref.dtype)


def moe_down(act, w_down, sub_e, sub_n, n_rows, *, tm, tn, sub):
    cap, f = act.shape
    d = w_down.shape[2]
    ratio = tm // sub

    def x_map(j, i, se, sn, nr):
        return (_grouped_tile(i, nr, tm), 0)

    def w_map(j, i, se, sn, nr):
        return (se[_grouped_tile(i, nr, tm) * ratio], 0, j)

    def o_map(j, i, se, sn, nr):
        return (i, j)

    return pl.pallas_call(
        functools.partial(_moe_down_body, tm=tm, sub=sub),
        grid_spec=pltpu.PrefetchScalarGridSpec(
            num_scalar_prefetch=3,
            grid=(d // tn, cap // tm),
            in_specs=[pl.BlockSpec((tm, f), x_map),
                      pl.BlockSpec((1, f, tn), w_map)],
            out_specs=pl.BlockSpec((tm, tn), o_map),
            scratch_shapes=[pltpu.VMEM((f, tn), BF16)]),
        out_shape=jax.ShapeDtypeStruct((cap, d), F32),
        compiler_params=_cparams(("arbitrary", "arbitrary")),
        name="moe_down",
    )(sub_e, sub_n, n_rows, act, w_down)


def _combine_body(dest_ref, y_hbm, gates_ref, res_ref, g_ref, b_ref, o_ref, buf, sem, *, tm, unroll):
    i = pl.program_id(0)

    def issue_tile(t, slot):
        def issue(g, c):
            for u in range(unroll):
                r = g * unroll + u
                for k in range(TOP_K):
                    _row_copy(y_hbm, buf.at[slot, k], sem.at[slot, k],
                              dest_ref[TOP_K * (t * tm + r) + k], r).start()
            return c

        lax.fori_loop(0, tm // unroll, issue, 0)

    @pl.when(i == 0)
    def _():
        issue_tile(0, 0)

    nxt = i + 1

    @pl.when(nxt < pl.num_programs(0))
    def _():
        issue_tile(nxt, nxt % 2)

    slot = i % 2
    gates = gates_ref[...]
    y = None
    for k in range(TOP_K):
        pltpu.make_async_copy(y_hbm.at[pl.ds(0, tm), :], buf.at[slot, k], sem.at[slot, k]).wait()
        term = buf[slot, k] * gates[:, TOP_K + k:TOP_K + k + 1]
        y = term if y is None else y + term
    o_ref[...] = _layer_norm_rows(DEEPNORM_ALPHA * res_ref[...] + y, g_ref[...], b_ref[...])


def moe_combine_post_norm(y_rows, dest, routing, res, g, b, *, tm):
    n_tok, d = res.shape
    lanes = routing.shape[1]
    return pl.pallas_call(
        functools.partial(_combine_body, tm=tm, unroll=4),
        grid_spec=pltpu.PrefetchScalarGridSpec(
            num_scalar_prefetch=1,
            grid=(n_tok // tm,),
            in_specs=[pl.BlockSpec(memory_space=pl.ANY),
                      pl.BlockSpec((tm, lanes), lambda i, dst: (i, 0)),
                      pl.BlockSpec((tm, d), lambda i, dst: (i, 0)),
                      pl.BlockSpec((1, d), lambda i, dst: (0, 0)),
                      pl.BlockSpec((1, d), lambda i, dst: (0, 0))],
            out_specs=pl.BlockSpec((tm, d), lambda i, dst: (i, 0)),
            scratch_shapes=[pltpu.VMEM((2, TOP_K, tm, d), F32), pltpu.SemaphoreType.DMA((2, TOP_K))]),
        out_shape=jax.ShapeDtypeStruct((n_tok, d), F32),
        compiler_params=_cparams(("arbitrary",)),
        name="moe_combine_post_norm",
    )(dest, y_rows, routing, res, g.reshape(1, d), b.reshape(1, d))


def _group_rows(seg, n_seg, tile, sub):
    n_asg = seg.shape[0]
    cap = n_asg + n_seg * tile
    onehot = (seg[:, None] == jnp.arange(n_seg, dtype=jnp.int32)[None, :]).astype(jnp.int32)
    csum = jnp.cumsum(onehot, axis=0)
    rank = jnp.sum(csum * onehot, axis=1) - 1
    counts = csum[-1]
    padded = (counts + tile - 1) // tile * tile
    pad_end = jnp.cumsum(padded)
    pad_start = pad_end - padded
    dest = (pad_start[seg] + rank).astype(jnp.int32)
    src = jnp.zeros((cap,), jnp.int32).at[dest].set(jnp.arange(n_asg, dtype=jnp.int32) // TOP_K)
    sub_start = jnp.arange(cap // sub, dtype=jnp.int32) * sub
    sub_e = jnp.minimum(jnp.searchsorted(pad_end, sub_start, side="right"), n_seg - 1).astype(jnp.int32)
    sub_n = jnp.clip(counts[sub_e] - (sub_start - pad_start[sub_e]), 0, sub).astype(jnp.int32)
    n_rows = pad_end[-1:].astype(jnp.int32)
    return dest, src, sub_e, sub_n, n_rows, cap


def _gla_layer(h, hb, w_in, w_gate_up, b_gate, g_norm, w_o, ln_g, ln_b):
    bsz, s, d = h.shape
    dk = (d // 2) // GLA_HEADS
    dv = d // GLA_HEADS
    n_main = 2 * GLA_HEADS * dk + 2 * GLA_HEADS * dv
    x2 = hb.reshape(bsz * s, d)
    proj = matmul(x2, w_in[:, :n_main].astype(BF16), tm=_pick(bsz * s, (1024, 512, 256, 128, 64)),
                  tn=_pick(n_main, (1024, 512, 256, 128)), out_dtype=F32)
    w_z = jnp.pad(w_in[:, n_main:], ((0, 0), (0, LANES - GLA_GATE_RANK))).astype(BF16)
    z = matmul(x2, w_z, tm=_pick(bsz * s, (1024, 512, 256, 128, 64)), tn=LANES, out_dtype=F32)
    wgu = jnp.pad(w_gate_up, ((0, LANES - GLA_GATE_RANK), (0, 0))).astype(BF16)
    wgu = wgu.reshape(LANES, GLA_HEADS, dk).transpose(1, 0, 2)
    og = gla_core(proj.reshape(bsz, s, n_main), z.reshape(bsz, s, LANES), wgu,
                  b_gate.reshape(GLA_HEADS, 1, dk), g_norm.reshape(1, dv),
                  heads=GLA_HEADS, dk=dk, dv=dv, step=min(GLA_STEP_TOKENS, s))
    out, outb = matmul_post_norm(og.reshape(bsz * s, d), w_o.astype(BF16), h.reshape(bsz * s, d), ln_g, ln_b,
                                 tm=_pick(bsz * s, (256, 128, 64)), tk=d)
    return out.reshape(bsz, s, d), outb.reshape(bsz, s, d)


def _moba_layer(h, hb, w_q, w_kv, w_o, ln_g, ln_b):
    bsz, s, d = h.shape
    hd = d // MOBA_HEADS
    w_all = jnp.concatenate([w_q, w_kv], axis=1).astype(BF16)
    n_all = w_all.shape[1]
    tn = _pick(math.gcd(d, n_all), (1024, 512, 256, 128))
    qkv = matmul(hb.reshape(bsz * s, d), w_all, tm=_pick(bsz * s, (1024, 512, 256, 128, 64)), tn=tn,
                 out_dtype=BF16, n_scaled=d // tn, scale=hd ** -0.5)
    att = moba_attention(qkv.reshape(bsz, s, n_all), heads=MOBA_HEADS, kv_heads=MOBA_KV_HEADS, hd=hd,
                         blk=MOBA_BLOCK, topk=MOBA_TOPK)
    out, outb, outp = matmul_post_norm(att.reshape(bsz * s, d), w_o.astype(BF16), h.reshape(bsz * s, d),
                                       ln_g, ln_b, tm=_pick(bsz * s, (256, 128, 64)), tk=d, packed=True)
    return out.reshape(bsz, s, d), outb.reshape(bsz, s, d), outp


def _dense_ffn(h, hb, w_gu, w_down, ln_g, ln_b):
    bsz, s, d = h.shape
    f = w_down.shape[0]
    f_pad = -(-f // 512) * 512
    wg = jnp.pad(w_gu[:, :f], ((0, 0), (0, f_pad - f))).astype(BF16)
    wu = jnp.pad(w_gu[:, f:], ((0, 0), (0, f_pad - f))).astype(BF16)
    wd = jnp.pad(w_down, ((0, f_pad - f), (0, 0))).astype(BF16)
    act = swiglu_up(hb.reshape(bsz * s, d), wg, wu, tm=_pick(bsz * s, (1024, 512, 256, 128, 64)), tf=512)
    tk = _pick(f_pad, (1408, 1024, 512))
    out, outb = matmul_post_norm(act, wd, h.reshape(bsz * s, d), ln_g, ln_b,
                                 tm=_pick(bsz * s, (512, 256, 128, 64)), tk=tk)
    return out.reshape(bsz, s, d), outb.reshape(bsz, s, d)


def _moe_ffn(h, hb, hp, w_router, w_gu, w_down, ln_g, ln_b):
    bsz, s, d = h.shape
    n_tok = bsz * s
    h2 = h.reshape(n_tok, d)
    w_r = jnp.pad(w_router, ((0, 0), (0, LANES - N_EXPERTS))).astype(BF16)
    routing = router_top2(hb.reshape(n_tok, d), w_r, tm=_pick(n_tok, (1024, 512, 256, 128, 64)),
                          n_experts=N_EXPERTS)
    seg = routing[:, :TOP_K].astype(jnp.int32).reshape(-1)
    tile = min(MOE_TILE, n_tok)
    sub = min(MOE_SUB, tile)
    dest, src, sub_e, sub_n, n_rows, cap = _group_rows(seg, N_EXPERTS, tile, sub)
    xs = gather_rows(hp, src, n_rows, cap=cap, tm=min(GATHER_TILE, tile))
    f = w_down.shape[1]
    act = moe_swiglu_up(xs, w_gu, sub_e, sub_n, n_rows, tm=tile, tf=_pick(f, (512, 256, 128)), sub=sub)
    y_rows = moe_down(act, w_down, sub_e, sub_n, n_rows, tm=sub, tn=_pick(d, (512, 256, 128)), sub=sub)
    out = moe_combine_post_norm(y_rows, dest, routing, h2, ln_g, ln_b, tm=_pick(n_tok, (256, 128, 64)))
    return out.reshape(bsz, s, d)


def kernel(x, w_in_a, w_gate_up_a, b_gate_a, g_norm_a, w_o_a, w_kv_shared, w_q_b, w_o_b,
           ln_mix_g, ln_mix_b, w_gu_dense, w_down_dense, w_router, w_gu_moe, w_down_moe,
           ln_ffn_g, ln_ffn_b):
    h, hb = _gla_layer(x, x.astype(BF16), w_in_a[0], w_gate_up_a[0], b_gate_a[0], g_norm_a[0], w_o_a[0],
                       ln_mix_g[0], ln_mix_b[0])
    h, hb = _dense_ffn(h, hb, w_gu_dense[0], w_down_dense[0], ln_ffn_g[0], ln_ffn_b[0])
    h, hb, hp = _moba_layer(h, hb, w_q_b[0], w_kv_shared, w_o_b[0], ln_mix_g[1], ln_mix_b[1])
    return _moe_ffn(h, hb, hp, w_router[0], w_gu_moe[0], w_down_moe[0], ln_ffn_g[1], ln_ffn_b[1])
```

```python
import functools
import math

import jax
import jax.numpy as jnp
from jax import lax
from jax.experimental import pallas as pl
from jax.experimental.pallas import tpu as pltpu

F32 = jnp.float32
BF16 = jnp.bfloat16

DEPTH = 2
GLA_HEADS = 4
GLA_GATE_RANK = 16
GLA_TAU = 16.0
GLA_CHUNK = 64
MOBA_HEADS = 16
MOBA_KV_HEADS = 4
MOBA_BLOCK = 256
MOBA_TOPK = 3
N_EXPERTS = 8
TOP_K = 2
LN_EPS = 1e-5
RMS_EPS = 1e-6
DEEPNORM_ALPHA = (2 * DEPTH) ** 0.25

LANES = 128
VMEM_LIMIT_BYTES = 56 * 1024 * 1024

GLA_GROUP = 4
GLA_STEP_TOKENS = 512
MOE_TILE = 1024
MOE_SUB = 256
MOE_DOWN_TILE = 512
GATHER_TILE = 512


def _cparams(sem):
    return pltpu.CompilerParams(dimension_semantics=sem, vmem_limit_bytes=VMEM_LIMIT_BYTES)


def _pick(n, prefs):
    for p in prefs:
        if n % p == 0:
            return p
    return n


def _mm_body(x_ref, w_ref, o_ref, *, n_scaled, scale):
    acc = jnp.dot(x_ref[...], w_ref[...], preferred_element_type=F32)
    if n_scaled:
        acc = acc * jnp.where(pl.program_id(0) < n_scaled, scale, 1.0).astype(F32)
    o_ref[...] = acc.astype(o_ref.dtype)


def matmul(x, w, *, tm, tn, out_dtype, n_scaled=0, scale=1.0):
    m, k = x.shape
    n = w.shape[1]
    return pl.pallas_call(
        functools.partial(_mm_body, n_scaled=n_scaled, scale=scale),
        grid=(n // tn, m // tm),
        in_specs=[pl.BlockSpec((tm, k), lambda j, i: (i, 0)),
                  pl.BlockSpec((k, tn), lambda j, i: (0, j))],
        out_specs=pl.BlockSpec((tm, tn), lambda j, i: (i, j)),
        out_shape=jax.ShapeDtypeStruct((m, n), out_dtype),
        compiler_params=_cparams(("parallel", "parallel")),
        name="matmul",
    )(x, w)


def _layer_norm_rows(t, g, b):
    mu = jnp.mean(t, axis=-1, keepdims=True)
    d = t - mu
    var = jnp.mean(d * d, axis=-1, keepdims=True)
    return d * lax.rsqrt(var + LN_EPS) * g + b


def _mm_ln_body(x_ref, w_ref, res_ref, g_ref, b_ref, o_ref, ob_ref, *rest, nk, packed):
    op_ref = rest[0] if packed else None
    acc_ref = rest[-1]
    part = jnp.dot(x_ref[...], w_ref[...], preferred_element_type=F32)

    def finish(y):
        out = _layer_norm_rows(DEEPNORM_ALPHA * res_ref[...] + y, g_ref[...], b_ref[...])
        o_ref[...] = out
        ob_ref[...] = out.astype(BF16)
        if packed:
            op_ref[...] = _pack_bf16_pairs(out)

    if nk == 1:
        finish(part)
        return
    k = pl.program_id(1)

    @pl.when(k == 0)
    def _():
        acc_ref[...] = part

    @pl.when(jnp.logical_and(k > 0, k < nk - 1))
    def _():
        acc_ref[...] += part

    @pl.when(k == nk - 1)
    def _():
        finish(acc_ref[...] + part)


def matmul_post_norm(x, w, res, g, b, *, tm, tk, packed=False):
    m, k = x.shape
    d = w.shape[1]
    nk = k // tk
    out_specs = [pl.BlockSpec((tm, d), lambda i, kk: (i, 0)), pl.BlockSpec((tm, d), lambda i, kk: (i, 0))]
    out_shape = [jax.ShapeDtypeStruct((m, d), F32), jax.ShapeDtypeStruct((m, d), BF16)]
    if packed:
        out_specs.append(pl.BlockSpec((tm, d // 2), lambda i, kk: (i, 0)))
        out_shape.append(jax.ShapeDtypeStruct((m, d // 2), jnp.uint32))
    return pl.pallas_call(
        functools.partial(_mm_ln_body, nk=nk, packed=packed),
        grid=(m // tm, nk),
        in_specs=[pl.BlockSpec((tm, tk), lambda i, kk: (i, kk)),
                  pl.BlockSpec((tk, d), lambda i, kk: (kk, 0)),
                  pl.BlockSpec((tm, d), lambda i, kk: (i, 0)),
                  pl.BlockSpec((1, d), lambda i, kk: (0, 0)),
                  pl.BlockSpec((1, d), lambda i, kk: (0, 0))],
        out_specs=out_specs,
        out_shape=out_shape,
        scratch_shapes=[pltpu.VMEM((tm, d), F32)],
        compiler_params=_cparams(("parallel", "arbitrary")),
        name="matmul_post_norm",
    )(x, w, res, g.reshape(1, d), b.reshape(1, d))


def _silu(x):
    return x / (1.0 + jnp.exp(-x))


def _swiglu_body(x_ref, wg_ref, wu_ref, o_ref):
    x = x_ref[...]
    g = jnp.dot(x, wg_ref[...], preferred_element_type=F32)
    u = jnp.dot(x, wu_ref[...], preferred_element_type=F32)
    o_ref[...] = (_silu(g) * u).astype(o_ref.dtype)


def swiglu_up(x, wg, wu, *, tm, tf):
    m, k = x.shape
    f = wg.shape[1]
    return pl.pallas_call(
        _swiglu_body,
        grid=(f // tf, m // tm),
        in_specs=[pl.BlockSpec((tm, k), lambda j, i: (i, 0)),
                  pl.BlockSpec((k, tf), lambda j, i: (0, j)),
                  pl.BlockSpec((k, tf), lambda j, i: (0, j))],
        out_specs=pl.BlockSpec((tm, tf), lambda j, i: (i, j)),
        out_shape=jax.ShapeDtypeStruct((m, f), BF16),
        compiler_params=_cparams(("parallel", "parallel")),
        name="swiglu_up",
    )(x, wg, wu)


def _gla_body(q_ref, k_ref, v_ref, r_ref, z_ref, wgu_ref, bg_ref, gn_ref, o_ref, state_ref,
              *, chunk, group, n_groups, dk):
    @pl.when(pl.program_id(2) == 0)
    def _():
        state_ref[...] = jnp.zeros_like(state_ref)

    gt = group * chunk
    row = lax.broadcasted_iota(jnp.int32, (gt, gt), 0)
    col = lax.broadcasted_iota(jnp.int32, (gt, gt), 1)
    causal = jnp.logical_and(row // chunk == col // chunk, col <= row)
    tril = jnp.where(causal, 1.0, 0.0).astype(BF16)
    nt = (((1,), (1,)), ((), ()))
    tn = (((0,), (0,)), ((), ()))

    def body(g, carry):
        sl = pl.ds(pl.multiple_of(g * gt, gt), gt)
        gate = jnp.dot(z_ref[0, sl, :].astype(BF16), wgu_ref[0], preferred_element_type=F32) + bg_ref[0]
        log_a = (jnp.minimum(gate, 0.0) - jnp.log(1.0 + jnp.exp(-jnp.abs(gate)))) / GLA_TAU
        la_hi = log_a.astype(BF16)
        la_lo = (log_a - la_hi.astype(F32)).astype(BF16)
        b_cum = (jnp.dot(tril, la_hi, preferred_element_type=F32)
                 + jnp.dot(tril, la_lo, preferred_element_type=F32))
        b_last = [b_cum[(c + 1) * chunk - 1:(c + 1) * chunk, :] for c in range(group)]
        b_last_rows = jnp.concatenate([jnp.broadcast_to(b, (chunk, b.shape[1])) for b in b_last], axis=0)
        q = q_ref[0, sl, :] * (dk ** -0.5)
        k = k_ref[0, sl, :]
        q_dec = (q * jnp.exp(b_cum)).astype(BF16)
        k_inv = (k * jnp.exp(-b_cum)).astype(BF16)
        k_end = (k * jnp.exp(b_last_rows - b_cum)).astype(BF16)
        v = v_ref[0, sl, :].astype(BF16)
        att = lax.dot_general(q_dec, k_inv, nt, preferred_element_type=F32)
        att = jnp.where(causal, att, 0.0).astype(BF16)
        o_intra = jnp.dot(att, v, preferred_element_type=F32)
        state_t = state_ref[...]
        outs = []
        for c in range(group):
            rows = slice(c * chunk, (c + 1) * chunk)
            outs.append(o_intra[rows] + lax.dot_general(q_dec[rows], state_t.astype(BF16), nt,
                                                        preferred_element_type=F32))
            state_t = (jnp.exp(b_last[c]) * state_t
                       + lax.dot_general(v[rows], k_end[rows], tn, preferred_element_type=F32))
        state_ref[...] = state_t
        o = jnp.concatenate(outs, axis=0)
        o = o * lax.rsqrt(jnp.mean(o * o, axis=-1, keepdims=True) + RMS_EPS) * gn_ref[...]
        o_ref[0, sl, :] = (o * _silu(r_ref[0, sl, :])).astype(o_ref.dtype)
        return carry

    lax.fori_loop(0, n_groups, body, 0)


def gla_core(proj, z, wgu, bg, gn, *, heads, dk, dv, step):
    bsz, s, _ = proj.shape
    zr = z.shape[-1]
    k_off = heads
    v_off = 2 * heads * dk // dv
    r_off = v_off + heads
    return pl.pallas_call(
        functools.partial(_gla_body, chunk=GLA_CHUNK, group=GLA_GROUP,
                          n_groups=step // (GLA_GROUP * GLA_CHUNK), dk=dk),
        grid=(bsz, heads, s // step),
        in_specs=[pl.BlockSpec((1, step, dk), lambda b, h, t: (b, t, h)),
                  pl.BlockSpec((1, step, dk), lambda b, h, t: (b, t, k_off + h)),
                  pl.BlockSpec((1, step, dv), lambda b, h, t: (b, t, v_off + h)),
                  pl.BlockSpec((1, step, dv), lambda b, h, t: (b, t, r_off + h)),
                  pl.BlockSpec((1, step, zr), lambda b, h, t: (b, t, 0)),
                  pl.BlockSpec((1, zr, dk), lambda b, h, t: (h, 0, 0)),
                  pl.BlockSpec((1, 1, dk), lambda b, h, t: (h, 0, 0)),
                  pl.BlockSpec((1, dv), lambda b, h, t: (0, 0))],
        out_specs=pl.BlockSpec((1, step, dv), lambda b, h, t: (b, t, h)),
        out_shape=jax.ShapeDtypeStruct((bsz, s, heads * dv), BF16),
        scratch_shapes=[pltpu.VMEM((dv, dk), F32)],
        compiler_params=_cparams(("parallel", "parallel", "arbitrary")),
        name="gla_core",
    )(proj, proj, proj, proj, z, wgu, bg, gn)


MASK_PENALTY = -3.0e38


def _moba_body(q_ref, k_ref, v_ref, o_ref, kmean_ref, kaug_ref, vt_ref, rhs_ref, m_ref, l_ref, acc_ref,
               *, blk, n_blk, rep, hd, topk):
    i = pl.program_id(2)
    nq = rep * blk

    @pl.when(i == 0)
    def _():
        lane = lax.broadcasted_iota(jnp.int32, (blk, hd), 1)
        for j in range(n_blk):
            kj = k_ref[0, j * blk:(j + 1) * blk, :]
            kaug_ref[j, :, :hd] = kj
            kaug_ref[j, :, hd:] = jnp.where(lane == j, 1.0, 0.0).astype(BF16)
            kmean_ref[j:j + 1, :] = jnp.sum(kj.astype(F32), axis=0, keepdims=True) / blk
            vt_ref[j] = v_ref[0, j * blk:(j + 1) * blk, :].astype(F32).T.astype(BF16)
        rhs_ref[hd + n_blk:, :] = jnp.zeros((hd - n_blk, nq), BF16)

    qt = jnp.concatenate([q_ref[0, :, r * hd:(r + 1) * hd].astype(F32).T for r in range(rep)], axis=1)
    qt = qt.astype(BF16)
    gate = jnp.dot(kmean_ref[...].astype(BF16), qt, preferred_element_type=F32)
    blk_id = lax.broadcasted_iota(jnp.int32, (n_blk, nq), 0)
    past = blk_id < i
    gate = jnp.where(past, gate, -jnp.inf)
    beaten = jnp.zeros((n_blk, nq), jnp.int32)
    for c in range(n_blk):
        gc = gate[c:c + 1, :]
        beaten = beaten + jnp.where(gc > gate, 1, jnp.where(gc == gate, jnp.where(c < blk_id, 1, 0), 0))
    penalty = jnp.where(past, jnp.where(beaten < topk, 0.0, MASK_PENALTY),
                        jnp.where(blk_id == i, 0.0, MASK_PENALTY))
    rhs_ref[:hd, :] = qt
    rhs_ref[hd:hd + n_blk, :] = penalty.astype(BF16)

    s = jnp.dot(kaug_ref[i], rhs_ref[...], preferred_element_type=F32)
    key = lax.broadcasted_iota(jnp.int32, (blk, nq), 0)
    qpos = lax.rem(lax.broadcasted_iota(jnp.int32, (blk, nq), 1), blk)
    s = jnp.where(key <= qpos, s, -jnp.inf)
    m0 = jnp.max(s, axis=0, keepdims=True)
    p = jnp.exp(s - m0)
    m_ref[...] = m0
    l_ref[...] = jnp.sum(p, axis=0, keepdims=True)
    acc_ref[...] = jnp.dot(vt_ref[i], p.astype(BF16), preferred_element_type=F32)

    def body(j, carry):
        sj = jnp.dot(kaug_ref[j], rhs_ref[...], preferred_element_type=F32)
        m_old = m_ref[...]
        m_new = jnp.maximum(m_old, jnp.max(sj, axis=0, keepdims=True))
        a = jnp.exp(m_old - m_new)
        pj = jnp.exp(sj - m_new)
        l_ref[...] = a * l_ref[...] + jnp.sum(pj, axis=0, keepdims=True)
        acc_ref[...] = a * acc_ref[...] + jnp.dot(vt_ref[j], pj.astype(BF16), preferred_element_type=F32)
        m_ref[...] = m_new
        return carry

    lax.fori_loop(0, i, body, 0)
    out = acc_ref[...] * (1.0 / l_ref[...])
    for r in range(rep):
        o_ref[0, :, r * hd:(r + 1) * hd] = out[:, r * blk:(r + 1) * blk].T.astype(o_ref.dtype)


def moba_attention(qkv, *, heads, kv_heads, hd, blk, topk):
    bsz, s, _ = qkv.shape
    rep = heads // kv_heads
    n_blk = s // blk
    nq = rep * blk
    assert n_blk <= hd and n_blk % 16 == 0
    k_off = heads
    v_off = heads + kv_heads
    return pl.pallas_call(
        functools.partial(_moba_body, blk=blk, n_blk=n_blk, rep=rep, hd=hd, topk=topk),
        grid=(bsz, kv_heads, n_blk),
        in_specs=[pl.BlockSpec((1, blk, rep * hd), lambda b, g, i: (b, i, g)),
                  pl.BlockSpec((1, s, hd), lambda b, g, i: (b, 0, k_off + g)),
                  pl.BlockSpec((1, s, hd), lambda b, g, i: (b, 0, v_off + g))],
        out_specs=pl.BlockSpec((1, blk, rep * hd), lambda b, g, i: (b, i, g)),
        out_shape=jax.ShapeDtypeStruct((bsz, s, heads * hd), BF16),
        scratch_shapes=[pltpu.VMEM((n_blk, hd), F32),
                        pltpu.VMEM((n_blk, blk, 2 * hd), BF16),
                        pltpu.VMEM((n_blk, hd, blk), BF16),
                        pltpu.VMEM((2 * hd, nq), BF16),
                        pltpu.VMEM((1, nq), F32),
                        pltpu.VMEM((1, nq), F32),
                        pltpu.VMEM((hd, nq), F32)],
        compiler_params=_cparams(("parallel", "parallel", "arbitrary")),
        name="moba_attention",
    )(qkv, qkv, qkv)


def _router_body(x_ref, w_ref, o_ref, *, n_experts):
    logits = jnp.dot(x_ref[...], w_ref[...], preferred_element_type=F32)
    lane = lax.broadcasted_iota(jnp.int32, logits.shape, 1)
    lane_f = lane.astype(F32)
    big = float(logits.shape[1])
    lg = jnp.where(lane < n_experts, logits, -jnp.inf)
    m1 = jnp.max(lg, axis=-1, keepdims=True)
    i1 = jnp.min(jnp.where(lg == m1, lane_f, big), axis=-1, keepdims=True)
    lg2 = jnp.where(lane_f == i1, -jnp.inf, lg)
    m2 = jnp.max(lg2, axis=-1, keepdims=True)
    i2 = jnp.min(jnp.where(lg2 == m2, lane_f, big), axis=-1, keepdims=True)
    t = jnp.exp(m2 - m1)
    g1 = 1.0 / (1.0 + t)
    g2 = t / (1.0 + t)
    o_ref[...] = jnp.where(lane == 0, i1,
                           jnp.where(lane == 1, i2, jnp.where(lane == 2, g1, jnp.where(lane == 3, g2, 0.0))))


def router_top2(x, w_pad, *, tm, n_experts):
    m, k = x.shape
    n = w_pad.shape[1]
    return pl.pallas_call(
        functools.partial(_router_body, n_experts=n_experts),
        grid=(m // tm,),
        in_specs=[pl.BlockSpec((tm, k), lambda i: (i, 0)),
                  pl.BlockSpec((k, n), lambda i: (0, 0))],
        out_specs=pl.BlockSpec((tm, n), lambda i: (i, 0)),
        out_shape=jax.ShapeDtypeStruct((m, n), F32),
        compiler_params=_cparams(("parallel",)),
        name="router_top2",
    )(x, w_pad)


def _row_copy(src_hbm, dst_vmem, sem, src_row, dst_row):
    return pltpu.make_async_copy(src_hbm.at[pl.ds(src_row, 1), :], dst_vmem.at[pl.ds(dst_row, 1), :], sem)


def _pack_bf16_pairs(x):
    half = x.shape[1] // 2
    lo = pltpu.bitcast(x[:, :half].astype(BF16).astype(F32), jnp.uint32)
    hi = pltpu.bitcast(x[:, half:].astype(BF16).astype(F32), jnp.uint32)
    return (hi & jnp.uint32(0xFFFF0000)) | (lo >> 16)


def _unpack_bf16_pairs(w):
    lo = pltpu.bitcast(w << 16, F32).astype(BF16)
    hi = pltpu.bitcast(w & jnp.uint32(0xFFFF0000), F32).astype(BF16)
    return jnp.concatenate([lo, hi], axis=1)


def _gather_body(src_ref, nrows_ref, hp_hbm, o_ref, buf, sem, *, tm, unroll):
    i = pl.program_id(0)

    def issue_tile(t, slot):
        def issue(g, c):
            for u in range(unroll):
                r = g * unroll + u
                _row_copy(hp_hbm, buf.at[slot], sem.at[slot], src_ref[t * tm + r], r).start()
            return c

        lax.fori_loop(0, tm // unroll, issue, 0)

    @pl.when(i == 0)
    def _():
        issue_tile(0, 0)

    nxt = i + 1

    @pl.when(jnp.logical_and(nxt < pl.num_programs(0), nxt * tm < nrows_ref[0]))
    def _():
        issue_tile(nxt, nxt % 2)

    used = i * tm < nrows_ref[0]

    @pl.when(jnp.logical_not(used))
    def _():
        o_ref[...] = jnp.zeros_like(o_ref)

    @pl.when(used)
    def _():
        slot = i % 2
        pltpu.make_async_copy(hp_hbm.at[pl.ds(0, tm), :], buf.at[slot], sem.at[slot]).wait()
        o_ref[...] = _unpack_bf16_pairs(buf[slot])


def gather_rows(hp, src_tok, n_used_rows, *, cap, tm):
    half = hp.shape[1]
    return pl.pallas_call(
        functools.partial(_gather_body, tm=tm, unroll=8),
        grid_spec=pltpu.PrefetchScalarGridSpec(
            num_scalar_prefetch=2,
            grid=(cap // tm,),
            in_specs=[pl.BlockSpec(memory_space=pl.ANY)],
            out_specs=pl.BlockSpec((tm, 2 * half), lambda i, src, nused: (i, 0)),
            scratch_shapes=[pltpu.VMEM((2, tm, half), jnp.uint32), pltpu.SemaphoreType.DMA((2,))]),
        out_shape=jax.ShapeDtypeStruct((cap, 2 * half), BF16),
        compiler_params=_cparams(("arbitrary",)),
        name="gather_rows",
    )(src_tok, n_used_rows, hp)


def _grouped_tile(i, n_rows, tm):
    return jnp.minimum(i, (n_rows[0] - 1) // tm)


def _grouped_fresh(i, sub_e, ratio):
    prev = sub_e[(jnp.maximum(i, 1) - 1) * ratio]
    return jnp.logical_or(i == 0, sub_e[i * ratio] != prev)


def _moe_up_body(sub_e, sub_n, n_rows, x_ref, wg_ref, wu_ref, o_ref, wgb_ref, wub_ref, *, tm, sub):
    i = pl.program_id(1)
    ratio = tm // sub
    used = i * tm < n_rows[0]

    @pl.when(jnp.logical_and(used, _grouped_fresh(i, sub_e, ratio)))
    def _():
        wgb_ref[...] = wg_ref[0].astype(BF16)
        wub_ref[...] = wu_ref[0].astype(BF16)

    for s in range(ratio):
        live = jnp.logical_and(used, sub_n[i * ratio + s] > 0)

        @pl.when(live)
        def _():
            x = x_ref[s * sub:(s + 1) * sub, :]
            g = jnp.dot(x, wgb_ref[...], preferred_element_type=F32)
            u = jnp.dot(x, wub_ref[...], preferred_element_type=F32)
            o_ref[s * sub:(s + 1) * sub, :] = (_silu(g) * u).astype(o_ref.dtype)

        @pl.when(jnp.logical_not(live))
        def _():
            o_ref[s * sub:(s + 1) * sub, :] = jnp.zeros((sub, o_ref.shape[1]), o_ref.dtype)


def moe_swiglu_up(xs, w_gu, sub_e, sub_n, n_rows, *, tm, tf, sub):
    cap, d = xs.shape
    f = w_gu.shape[2] // 2
    nf = f // tf
    ratio = tm // sub

    def x_map(j, i, se, sn, nr):
        return (_grouped_tile(i, nr, tm), 0)

    def wg_map(j, i, se, sn, nr):
        return (se[_grouped_tile(i, nr, tm) * ratio], 0, j)

    def wu_map(j, i, se, sn, nr):
        return (se[_grouped_tile(i, nr, tm) * ratio], 0, nf + j)

    def o_map(j, i, se, sn, nr):
        return (i, j)

    return pl.pallas_call(
        functools.partial(_moe_up_body, tm=tm, sub=sub),
        grid_spec=pltpu.PrefetchScalarGridSpec(
            num_scalar_prefetch=3,
            grid=(nf, cap // tm),
            in_specs=[pl.BlockSpec((tm, d), x_map),
                      pl.BlockSpec((1, d, tf), wg_map),
                      pl.BlockSpec((1, d, tf), wu_map)],
            out_specs=pl.BlockSpec((tm, tf), o_map),
            scratch_shapes=[pltpu.VMEM((d, tf), BF16), pltpu.VMEM((d, tf), BF16)]),
        out_shape=jax.ShapeDtypeStruct((cap, f), BF16),
        compiler_params=_cparams(("arbitrary", "arbitrary")),
        name="moe_swiglu_up",
    )(sub_e, sub_n, n_rows, xs, w_gu, w_gu)


def _moe_down_body(sub_e, sub_n, n_rows, x_ref, w_ref, o_ref, wb_ref, *, tm, sub):
    i = pl.program_id(1)
    ratio = tm // sub
    used = i * tm < n_rows[0]

    @pl.when(jnp.logical_and(used, _grouped_fresh(i, sub_e, ratio)))
    def _():
        wb_ref[...] = w_ref[0].astype(BF16)

    for s in range(ratio):
        live = jnp.logical_and(used, sub_n[i * ratio + s] > 0)

        @pl.when(live)
        def _():
            o_ref[s * sub:(s + 1) * sub, :] = jnp.dot(
                x_ref[s * sub:(s + 1) * sub, :], wb_ref[...], preferred_element_type=F32)

        @pl.when(jnp.logical_not(live))
        def _():
            o_ref[s * sub:(s + 1) * sub, :] = jnp.zeros((sub, o_ref.shape[1]), o_ref.dtype)


def moe_down(act, w_down, sub_e, sub_n, n_rows, *, tm, tn, sub):
    cap, f = act.shape
    d = w_down.shape[2]
    ratio = tm // sub

    def x_map(j, i, se, sn, nr):
        return (_grouped_tile(i, nr, tm), 0)

    def w_map(j, i, se, sn, nr):
        return (se[_grouped_tile(i, nr, tm) * ratio], 0, j)

    def o_map(j, i, se, sn, nr):
        return (i, j)

    return pl.pallas_call(
        functools.partial(_moe_down_body, tm=tm, sub=sub),
        grid_spec=pltpu.PrefetchScalarGridSpec(
            num_scalar_prefetch=3,
            grid=(d // tn, cap // tm),
            in_specs=[pl.BlockSpec((tm, f), x_map),
                      pl.BlockSpec((1, f, tn), w_map)],
            out_specs=pl.BlockSpec((tm, tn), o_map),
            scratch_shapes=[pltpu.VMEM((f, tn), BF16)]),
        out_shape=jax.ShapeDtypeStruct((cap, d), F32),
        compiler_params=_cparams(("arbitrary", "arbitrary")),
        name="moe_down",
    )(sub_e, sub_n, n_rows, act, w_down)


def _combine_body(dest_ref, y_hbm, gates_ref, res_ref, g_ref, b_ref, o_ref, buf, sem, *, tm, unroll):
    i = pl.program_id(0)

    def issue_tile(t, slot):
        def issue(g, c):
            for u in range(unroll):
                r = g * unroll + u
                for k in range(TOP_K):
                    _row_copy(y_hbm, buf.at[slot, k], sem.at[slot, k],
                              dest_ref[TOP_K * (t * tm + r) + k], r).start()
            return c

        lax.fori_loop(0, tm // unroll, issue, 0)

    @pl.when(i == 0)
    def _():
        issue_tile(0, 0)

    nxt = i + 1

    @pl.when(nxt < pl.num_programs(0))
    def _():
        issue_tile(nxt, nxt % 2)

    slot = i % 2
    gates = gates_ref[...]
    y = None
    for k in range(TOP_K):
        pltpu.make_async_copy(y_hbm.at[pl.ds(0, tm), :], buf.at[slot, k], sem.at[slot, k]).wait()
        term = buf[slot, k] * gates[:, TOP_K + k:TOP_K + k + 1]
        y = term if y is None else y + term
    o_ref[...] = _layer_norm_rows(DEEPNORM_ALPHA * res_ref[...] + y, g_ref[...], b_ref[...])


def moe_combine_post_norm(y_rows, dest, routing, res, g, b, *, tm):
    n_tok, d = res.shape
    lanes = routing.shape[1]
    return pl.pallas_call(
        functools.partial(_combine_body, tm=tm, unroll=4),
        grid_spec=pltpu.PrefetchScalarGridSpec(
            num_scalar_prefetch=1,
            grid=(n_tok // tm,),
            in_specs=[pl.BlockSpec(memory_space=pl.ANY),
                      pl.BlockSpec((tm, lanes), lambda i, dst: (i, 0)),
                      pl.BlockSpec((tm, d), lambda i, dst: (i, 0)),
                      pl.BlockSpec((1, d), lambda i, dst: (0, 0)),
                      pl.BlockSpec((1, d), lambda i, dst: (0, 0))],
            out_specs=pl.BlockSpec((tm, d), lambda i, dst: (i, 0)),
            scratch_shapes=[pltpu.VMEM((2, TOP_K, tm, d), F32), pltpu.SemaphoreType.DMA((2, TOP_K))]),
        out_shape=jax.ShapeDtypeStruct((n_tok, d), F32),
        compiler_params=_cparams(("arbitrary",)),
        name="moe_combine_post_norm",
    )(dest, y_rows, routing, res, g.reshape(1, d), b.reshape(1, d))


def _group_rows(seg, n_seg, tile, sub):
    n_asg = seg.shape[0]
    cap = n_asg + n_seg * tile
    onehot = (seg[:, None] == jnp.arange(n_seg, dtype=jnp.int32)[None, :]).astype(jnp.int32)
    csum = jnp.cumsum(onehot, axis=0)
    rank = jnp.sum(csum * onehot, axis=1) - 1
    counts = csum[-1]
    padded = (counts + tile - 1) // tile * tile
    pad_end = jnp.cumsum(padded)
    pad_start = pad_end - padded
    dest = (pad_start[seg] + rank).astype(jnp.int32)
    src = jnp.zeros((cap,), jnp.int32).at[dest].set(
        jnp.arange(n_asg, dtype=jnp.int32) // TOP_K, unique_indices=True, mode="promise_in_bounds")
    sub_start = jnp.arange(cap // sub, dtype=jnp.int32) * sub
    sub_e = jnp.minimum(jnp.searchsorted(pad_end, sub_start, side="right"), n_seg - 1).astype(jnp.int32)
    sub_n = jnp.clip(counts[sub_e] - (sub_start - pad_start[sub_e]), 0, sub).astype(jnp.int32)
    n_rows = pad_end[-1:].astype(jnp.int32)
    return dest, src, sub_e, sub_n, n_rows, cap


def _gla_layer(h, hb, w_in, w_gate_up, b_gate, g_norm, w_o, ln_g, ln_b):
    bsz, s, d = h.shape
    dk = (d // 2) // GLA_HEADS
    dv = d // GLA_HEADS
    n_main = 2 * GLA_HEADS * dk + 2 * GLA_HEADS * dv
    x2 = hb.reshape(bsz * s, d)
    proj = matmul(x2, w_in[:, :n_main].astype(BF16), tm=_pick(bsz * s, (1024, 512, 256, 128, 64)),
                  tn=_pick(n_main, (1024, 512, 256, 128)), out_dtype=F32)
    w_z = jnp.pad(w_in[:, n_main:], ((0, 0), (0, LANES - GLA_GATE_RANK))).astype(BF16)
    z = matmul(x2, w_z, tm=_pick(bsz * s, (1024, 512, 256, 128, 64)), tn=LANES, out_dtype=F32)
    wgu = jnp.pad(w_gate_up, ((0, LANES - GLA_GATE_RANK), (0, 0))).astype(BF16)
    wgu = wgu.reshape(LANES, GLA_HEADS, dk).transpose(1, 0, 2)
    og = gla_core(proj.reshape(bsz, s, n_main), z.reshape(bsz, s, LANES), wgu,
                  b_gate.reshape(GLA_HEADS, 1, dk), g_norm.reshape(1, dv),
                  heads=GLA_HEADS, dk=dk, dv=dv, step=min(GLA_STEP_TOKENS, s))
    out, outb = matmul_post_norm(og.reshape(bsz * s, d), w_o.astype(BF16), h.reshape(bsz * s, d), ln_g, ln_b,
                                 tm=_pick(bsz * s, (256, 128, 64)), tk=d)
    return out.reshape(bsz, s, d), outb.reshape(bsz, s, d)


def _moba_layer(h, hb, w_q, w_kv, w_o, ln_g, ln_b):
    bsz, s, d = h.shape
    hd = d // MOBA_HEADS
    w_all = jnp.concatenate([w_q, w_kv], axis=1).astype(BF16)
    n_all = w_all.shape[1]
    tn = _pick(math.gcd(d, n_all), (1024, 512, 256, 128))
    qkv = matmul(hb.reshape(bsz * s, d), w_all, tm=_pick(bsz * s, (1024, 512, 256, 128, 64)), tn=tn,
                 out_dtype=BF16, n_scaled=d // tn, scale=hd ** -0.5)
    att = moba_attention(qkv.reshape(bsz, s, n_all), heads=MOBA_HEADS, kv_heads=MOBA_KV_HEADS, hd=hd,
                         blk=MOBA_BLOCK, topk=MOBA_TOPK)
    out, outb, outp = matmul_post_norm(att.reshape(bsz * s, d), w_o.astype(BF16), h.reshape(bsz * s, d),
                                       ln_g, ln_b, tm=_pick(bsz * s, (256, 128, 64)), tk=d, packed=True)
    return out.reshape(bsz, s, d), outb.reshape(bsz, s, d), outp


def _dense_ffn(h, hb, w_gu, w_down, ln_g, ln_b):
    bsz, s, d = h.shape
    f = w_down.shape[0]
    f_pad = -(-f // 512) * 512
    w_gu_b = w_gu.astype(BF16)
    wg = jnp.pad(w_gu_b[:, :f], ((0, 0), (0, f_pad - f)))
    wu = jnp.pad(w_gu_b[:, f:], ((0, 0), (0, f_pad - f)))
    wd = jnp.pad(w_down.astype(BF16), ((0, f_pad - f), (0, 0)))
    act = swiglu_up(hb.reshape(bsz * s, d), wg, wu, tm=_pick(bsz * s, (1024, 512, 256, 128, 64)), tf=512)
    tk = _pick(f_pad, (1408, 1024, 512))
    out, outb = matmul_post_norm(act, wd, h.reshape(bsz * s, d), ln_g, ln_b,
                                 tm=_pick(bsz * s, (512, 256, 128, 64)), tk=tk)
    return out.reshape(bsz, s, d), outb.reshape(bsz, s, d)


def _moe_ffn(h, hb, hp, w_router, w_gu, w_down, ln_g, ln_b):
    bsz, s, d = h.shape
    n_tok = bsz * s
    h2 = h.reshape(n_tok, d)
    w_r = jnp.pad(w_router, ((0, 0), (0, LANES - N_EXPERTS))).astype(BF16)
    routing = router_top2(hb.reshape(n_tok, d), w_r, tm=_pick(n_tok, (1024, 512, 256, 128, 64)),
                          n_experts=N_EXPERTS)
    seg = routing[:, :TOP_K].astype(jnp.int32).reshape(-1)
    tile = min(MOE_TILE, n_tok)
    sub = min(MOE_SUB, tile)
    dest, src, sub_e, sub_n, n_rows, cap = _group_rows(seg, N_EXPERTS, tile, sub)
    xs = gather_rows(hp, src, n_rows, cap=cap, tm=min(GATHER_TILE, tile))
    f = w_down.shape[1]
    act = moe_swiglu_up(xs, w_gu, sub_e, sub_n, n_rows, tm=tile, tf=_pick(f, (512, 256, 128)), sub=sub)
    y_rows = moe_down(act, w_down, sub_e, sub_n, n_rows, tm=min(MOE_DOWN_TILE, tile),
                      tn=_pick(d, (512, 256, 128)), sub=sub)
    out = moe_combine_post_norm(y_rows, dest, routing, h2, ln_g, ln_b, tm=_pick(n_tok, (256, 128, 64)))
    return out.reshape(bsz, s, d)


def kernel(x, w_in_a, w_gate_up_a, b_gate_a, g_norm_a, w_o_a, w_kv_shared, w_q_b, w_o_b,
           ln_mix_g, ln_mix_b, w_gu_dense, w_down_dense, w_router, w_gu_moe, w_down_moe,
           ln_ffn_g, ln_ffn_b):
    h, hb = _gla_layer(x, x.astype(BF16), w_in_a[0], w_gate_up_a[0], b_gate_a[0], g_norm_a[0], w_o_a[0],
                       ln_mix_g[0], ln_mix_b[0])
    h, hb = _dense_ffn(h, hb, w_gu_dense[0], w_down_dense[0], ln_ffn_g[0], ln_ffn_b[0])
    h, hb, hp = _moba_layer(h, hb, w_q_b[0], w_kv_shared, w_o_b[0], ln_mix_g[1], ln_mix_b[1])
    return _moe_ffn(h, hb, hp, w_router[0], w_gu_moe[0], w_down_moe[0], ln_ffn_g[1], ln_ffn_b[1])
```

```python
import functools
import math

import jax
import jax.numpy as jnp
from jax import lax
from jax.experimental import pallas as pl
from jax.experimental.pallas import tpu as pltpu

F32 = jnp.float32
BF16 = jnp.bfloat16

DEPTH = 2
GLA_HEADS = 4
GLA_GATE_RANK = 16
GLA_TAU = 16.0
GLA_CHUNK = 64
MOBA_HEADS = 16
MOBA_KV_HEADS = 4
MOBA_BLOCK = 256
MOBA_TOPK = 3
N_EXPERTS = 8
TOP_K = 2
LN_EPS = 1e-5
RMS_EPS = 1e-6
DEEPNORM_ALPHA = (2 * DEPTH) ** 0.25

LANES = 128
VMEM_LIMIT_BYTES = 56 * 1024 * 1024

GLA_GROUP = 4
GLA_STEP_TOKENS = 512
MOE_TILE = 1024
MOE_SUB = 256
MOE_DOWN_TILE = 512
GATHER_TILE = 512


def _cparams(sem):
    return pltpu.CompilerParams(dimension_semantics=sem, vmem_limit_bytes=VMEM_LIMIT_BYTES)


def _pick(n, prefs):
    for p in prefs:
        if n % p == 0:
            return p
    return n


def _mm_body(x_ref, w_ref, o_ref, *, n_scaled, scale):
    acc = jnp.dot(x_ref[...], w_ref[...], preferred_element_type=F32)
    if n_scaled:
        acc = acc * jnp.where(pl.program_id(0) < n_scaled, scale, 1.0).astype(F32)
    o_ref[...] = acc.astype(o_ref.dtype)


def matmul(x, w, *, tm, tn, out_dtype, n_scaled=0, scale=1.0):
    m, k = x.shape
    n = w.shape[1]
    return pl.pallas_call(
        functools.partial(_mm_body, n_scaled=n_scaled, scale=scale),
        grid=(n // tn, m // tm),
        in_specs=[pl.BlockSpec((tm, k), lambda j, i: (i, 0)),
                  pl.BlockSpec((k, tn), lambda j, i: (0, j))],
        out_specs=pl.BlockSpec((tm, tn), lambda j, i: (i, j)),
        out_shape=jax.ShapeDtypeStruct((m, n), out_dtype),
        compiler_params=_cparams(("parallel", "parallel")),
        name="matmul",
    )(x, w)


def _layer_norm_rows(t, g, b):
    mu = jnp.mean(t, axis=-1, keepdims=True)
    d = t - mu
    var = jnp.mean(d * d, axis=-1, keepdims=True)
    return d * lax.rsqrt(var + LN_EPS) * g + b


def _mm_ln_body(x_ref, w_ref, res_ref, g_ref, b_ref, o_ref, ob_ref, *rest, tm, sub, packed):
    for s in range(tm // sub):
        rows = slice(s * sub, (s + 1) * sub)
        y = jnp.dot(x_ref[rows, :], w_ref[...], preferred_element_type=F32)
        out = _layer_norm_rows(DEEPNORM_ALPHA * res_ref[rows, :] + y, g_ref[...], b_ref[...])
        o_ref[rows, :] = out
        ob_ref[rows, :] = out.astype(BF16)
        if packed:
            rest[0][rows, :] = _pack_bf16_pairs(out)


def matmul_post_norm(x, w, res, g, b, *, tm, sub, packed=False):
    m, k = x.shape
    d = w.shape[1]
    out_specs = [pl.BlockSpec((tm, d), lambda i: (i, 0)), pl.BlockSpec((tm, d), lambda i: (i, 0))]
    out_shape = [jax.ShapeDtypeStruct((m, d), F32), jax.ShapeDtypeStruct((m, d), BF16)]
    if packed:
        out_specs.append(pl.BlockSpec((tm, d // 2), lambda i: (i, 0)))
        out_shape.append(jax.ShapeDtypeStruct((m, d // 2), jnp.uint32))
    return pl.pallas_call(
        functools.partial(_mm_ln_body, tm=tm, sub=sub, packed=packed),
        grid=(m // tm,),
        in_specs=[pl.BlockSpec((tm, k), lambda i: (i, 0)),
                  pl.BlockSpec((k, d), lambda i: (0, 0), pipeline_mode=pl.Buffered(1)),
                  pl.BlockSpec((tm, d), lambda i: (i, 0)),
                  pl.BlockSpec((1, d), lambda i: (0, 0)),
                  pl.BlockSpec((1, d), lambda i: (0, 0))],
        out_specs=out_specs,
        out_shape=out_shape,
        compiler_params=_cparams(("parallel",)),
        name="matmul_post_norm",
    )(x, w, res, g.reshape(1, d), b.reshape(1, d))


def _silu(x):
    return x / (1.0 + jnp.exp(-x))


def _swiglu_body(x_ref, wg_ref, wu_ref, o_ref):
    x = x_ref[...]
    g = jnp.dot(x, wg_ref[...], preferred_element_type=F32)
    u = jnp.dot(x, wu_ref[...], preferred_element_type=F32)
    o_ref[...] = (_silu(g) * u).astype(o_ref.dtype)


def swiglu_up(x, wg, wu, *, tm, tf):
    m, k = x.shape
    f = wg.shape[1]
    return pl.pallas_call(
        _swiglu_body,
        grid=(f // tf, m // tm),
        in_specs=[pl.BlockSpec((tm, k), lambda j, i: (i, 0)),
                  pl.BlockSpec((k, tf), lambda j, i: (0, j)),
                  pl.BlockSpec((k, tf), lambda j, i: (0, j))],
        out_specs=pl.BlockSpec((tm, tf), lambda j, i: (i, j)),
        out_shape=jax.ShapeDtypeStruct((m, f), BF16),
        compiler_params=_cparams(("parallel", "parallel")),
        name="swiglu_up",
    )(x, wg, wu)


def _gla_body(q_ref, k_ref, v_ref, r_ref, z_ref, wgu_ref, bg_ref, gn_ref, o_ref, state_ref,
              *, chunk, group, n_groups, dk):
    @pl.when(pl.program_id(2) == 0)
    def _():
        state_ref[...] = jnp.zeros_like(state_ref)

    gt = group * chunk
    row = lax.broadcasted_iota(jnp.int32, (gt, gt), 0)
    col = lax.broadcasted_iota(jnp.int32, (gt, gt), 1)
    causal = jnp.logical_and(row // chunk == col // chunk, col <= row)
    tril = jnp.where(causal, 1.0, 0.0).astype(BF16)
    nt = (((1,), (1,)), ((), ()))
    tn = (((0,), (0,)), ((), ()))

    def body(g, carry):
        sl = pl.ds(pl.multiple_of(g * gt, gt), gt)
        gate = jnp.dot(z_ref[0, sl, :].astype(BF16), wgu_ref[0], preferred_element_type=F32) + bg_ref[0]
        log_a = (jnp.minimum(gate, 0.0) - jnp.log(1.0 + jnp.exp(-jnp.abs(gate)))) / GLA_TAU
        la_hi = log_a.astype(BF16)
        la_lo = (log_a - la_hi.astype(F32)).astype(BF16)
        b_cum = (jnp.dot(tril, la_hi, preferred_element_type=F32)
                 + jnp.dot(tril, la_lo, preferred_element_type=F32))
        b_last = [b_cum[(c + 1) * chunk - 1:(c + 1) * chunk, :] for c in range(group)]
        b_last_rows = jnp.concatenate([jnp.broadcast_to(b, (chunk, b.shape[1])) for b in b_last], axis=0)
        q = q_ref[0, sl, :] * (dk ** -0.5)
        k = k_ref[0, sl, :]
        q_dec = (q * jnp.exp(b_cum)).astype(BF16)
        k_inv = (k * jnp.exp(-b_cum)).astype(BF16)
        k_end = (k * jnp.exp(b_last_rows - b_cum)).astype(BF16)
        v = v_ref[0, sl, :].astype(BF16)
        att = lax.dot_general(q_dec, k_inv, nt, preferred_element_type=F32)
        att = jnp.where(causal, att, 0.0).astype(BF16)
        o_intra = jnp.dot(att, v, preferred_element_type=F32)
        state_t = state_ref[...]
        outs = []
        for c in range(group):
            rows = slice(c * chunk, (c + 1) * chunk)
            outs.append(o_intra[rows] + lax.dot_general(q_dec[rows], state_t.astype(BF16), nt,
                                                        preferred_element_type=F32))
            state_t = (jnp.exp(b_last[c]) * state_t
                       + lax.dot_general(v[rows], k_end[rows], tn, preferred_element_type=F32))
        state_ref[...] = state_t
        o = jnp.concatenate(outs, axis=0)
        o = o * lax.rsqrt(jnp.mean(o * o, axis=-1, keepdims=True) + RMS_EPS) * gn_ref[...]
        o_ref[0, sl, :] = (o * _silu(r_ref[0, sl, :])).astype(o_ref.dtype)
        return carry

    lax.fori_loop(0, n_groups, body, 0)


def gla_core(proj, z, wgu, bg, gn, *, heads, dk, dv, step):
    bsz, s, _ = proj.shape
    zr = z.shape[-1]
    k_off = heads
    v_off = 2 * heads * dk // dv
    r_off = v_off + heads
    return pl.pallas_call(
        functools.partial(_gla_body, chunk=GLA_CHUNK, group=GLA_GROUP,
                          n_groups=step // (GLA_GROUP * GLA_CHUNK), dk=dk),
        grid=(bsz, heads, s // step),
        in_specs=[pl.BlockSpec((1, step, dk), lambda b, h, t: (b, t, h)),
                  pl.BlockSpec((1, step, dk), lambda b, h, t: (b, t, k_off + h)),
                  pl.BlockSpec((1, step, dv), lambda b, h, t: (b, t, v_off + h)),
                  pl.BlockSpec((1, step, dv), lambda b, h, t: (b, t, r_off + h)),
                  pl.BlockSpec((1, step, zr), lambda b, h, t: (b, t, 0)),
                  pl.BlockSpec((1, zr, dk), lambda b, h, t: (h, 0, 0)),
                  pl.BlockSpec((1, 1, dk), lambda b, h, t: (h, 0, 0)),
                  pl.BlockSpec((1, dv), lambda b, h, t: (0, 0))],
        out_specs=pl.BlockSpec((1, step, dv), lambda b, h, t: (b, t, h)),
        out_shape=jax.ShapeDtypeStruct((bsz, s, heads * dv), BF16),
        scratch_shapes=[pltpu.VMEM((dv, dk), F32)],
        compiler_params=_cparams(("parallel", "parallel", "arbitrary")),
        name="gla_core",
    )(proj, proj, proj, proj, z, wgu, bg, gn)


MASK_PENALTY = -3.0e38


def _moba_body(q_ref, k_ref, v_ref, o_ref, kmean_ref, kaug_ref, vt_ref, rhs_ref, m_ref, l_ref, acc_ref,
               *, blk, n_blk, rep, hd, topk):
    i = pl.program_id(2)
    nq = rep * blk

    @pl.when(i == 0)
    def _():
        lane = lax.broadcasted_iota(jnp.int32, (blk, hd), 1)
        for j in range(n_blk):
            kj = k_ref[0, j * blk:(j + 1) * blk, :]
            kaug_ref[j, :, :hd] = kj
            kaug_ref[j, :, hd:] = jnp.where(lane == j, 1.0, 0.0).astype(BF16)
            kmean_ref[j:j + 1, :] = jnp.sum(kj.astype(F32), axis=0, keepdims=True) / blk
            vt_ref[j] = v_ref[0, j * blk:(j + 1) * blk, :].astype(F32).T.astype(BF16)
        rhs_ref[hd + n_blk:, :] = jnp.zeros((hd - n_blk, nq), BF16)

    qt = jnp.concatenate([q_ref[0, :, r * hd:(r + 1) * hd].astype(F32).T for r in range(rep)], axis=1)
    qt = qt.astype(BF16)
    gate = jnp.dot(kmean_ref[...].astype(BF16), qt, preferred_element_type=F32)
    blk_id = lax.broadcasted_iota(jnp.int32, (n_blk, nq), 0)
    past = blk_id < i
    gate = jnp.where(past, gate, -jnp.inf)
    beaten = jnp.zeros((n_blk, nq), jnp.int32)
    for c in range(n_blk):
        gc = gate[c:c + 1, :]
        beaten = beaten + jnp.where(gc > gate, 1, jnp.where(gc == gate, jnp.where(c < blk_id, 1, 0), 0))
    penalty = jnp.where(past, jnp.where(beaten < topk, 0.0, MASK_PENALTY),
                        jnp.where(blk_id == i, 0.0, MASK_PENALTY))
    rhs_ref[:hd, :] = qt
    rhs_ref[hd:hd + n_blk, :] = penalty.astype(BF16)

    s = jnp.dot(kaug_ref[i], rhs_ref[...], preferred_element_type=F32)
    key = lax.broadcasted_iota(jnp.int32, (blk, nq), 0)
    qpos = lax.rem(lax.broadcasted_iota(jnp.int32, (blk, nq), 1), blk)
    s = jnp.where(key <= qpos, s, -jnp.inf)
    m0 = jnp.max(s, axis=0, keepdims=True)
    p = jnp.exp(s - m0)
    m_ref[...] = m0
    l_ref[...] = jnp.sum(p, axis=0, keepdims=True)
    acc_ref[...] = jnp.dot(vt_ref[i], p.astype(BF16), preferred_element_type=F32)

    def body(j, carry):
        sj = jnp.dot(kaug_ref[j], rhs_ref[...], preferred_element_type=F32)
        m_old = m_ref[...]
        m_new = jnp.maximum(m_old, jnp.max(sj, axis=0, keepdims=True))
        a = jnp.exp(m_old - m_new)
        pj = jnp.exp(sj - m_new)
        l_ref[...] = a * l_ref[...] + jnp.sum(pj, axis=0, keepdims=True)
        acc_ref[...] = a * acc_ref[...] + jnp.dot(vt_ref[j], pj.astype(BF16), preferred_element_type=F32)
        m_ref[...] = m_new
        return carry

    lax.fori_loop(0, i, body, 0)
    out = acc_ref[...] * (1.0 / l_ref[...])
    for r in range(rep):
        o_ref[0, :, r * hd:(r + 1) * hd] = out[:, r * blk:(r + 1) * blk].T.astype(o_ref.dtype)


def moba_attention(qkv, *, heads, kv_heads, hd, blk, topk):
    bsz, s, _ = qkv.shape
    rep = heads // kv_heads
    n_blk = s // blk
    nq = rep * blk
    assert n_blk <= hd and n_blk % 16 == 0
    k_off = heads
    v_off = heads + kv_heads
    return pl.pallas_call(
        functools.partial(_moba_body, blk=blk, n_blk=n_blk, rep=rep, hd=hd, topk=topk),
        grid=(bsz, kv_heads, n_blk),
        in_specs=[pl.BlockSpec((1, blk, rep * hd), lambda b, g, i: (b, i, g)),
                  pl.BlockSpec((1, s, hd), lambda b, g, i: (b, 0, k_off + g)),
                  pl.BlockSpec((1, s, hd), lambda b, g, i: (b, 0, v_off + g))],
        out_specs=pl.BlockSpec((1, blk, rep * hd), lambda b, g, i: (b, i, g)),
        out_shape=jax.ShapeDtypeStruct((bsz, s, heads * hd), BF16),
        scratch_shapes=[pltpu.VMEM((n_blk, hd), F32),
                        pltpu.VMEM((n_blk, blk, 2 * hd), BF16),
                        pltpu.VMEM((n_blk, hd, blk), BF16),
                        pltpu.VMEM((2 * hd, nq), BF16),
                        pltpu.VMEM((1, nq), F32),
                        pltpu.VMEM((1, nq), F32),
                        pltpu.VMEM((hd, nq), F32)],
        compiler_params=_cparams(("parallel", "parallel", "arbitrary")),
        name="moba_attention",
    )(qkv, qkv, qkv)


def _router_body(x_ref, w_ref, o_ref, *, n_experts):
    logits = jnp.dot(x_ref[...], w_ref[...], preferred_element_type=F32)
    lane = lax.broadcasted_iota(jnp.int32, logits.shape, 1)
    lane_f = lane.astype(F32)
    big = float(logits.shape[1])
    lg = jnp.where(lane < n_experts, logits, -jnp.inf)
    m1 = jnp.max(lg, axis=-1, keepdims=True)
    i1 = jnp.min(jnp.where(lg == m1, lane_f, big), axis=-1, keepdims=True)
    lg2 = jnp.where(lane_f == i1, -jnp.inf, lg)
    m2 = jnp.max(lg2, axis=-1, keepdims=True)
    i2 = jnp.min(jnp.where(lg2 == m2, lane_f, big), axis=-1, keepdims=True)
    t = jnp.exp(m2 - m1)
    g1 = 1.0 / (1.0 + t)
    g2 = t / (1.0 + t)
    o_ref[...] = jnp.where(lane == 0, i1,
                           jnp.where(lane == 1, i2, jnp.where(lane == 2, g1, jnp.where(lane == 3, g2, 0.0))))


def router_top2(x, w_pad, *, tm, n_experts):
    m, k = x.shape
    n = w_pad.shape[1]
    return pl.pallas_call(
        functools.partial(_router_body, n_experts=n_experts),
        grid=(m // tm,),
        in_specs=[pl.BlockSpec((tm, k), lambda i: (i, 0)),
                  pl.BlockSpec((k, n), lambda i: (0, 0))],
        out_specs=pl.BlockSpec((tm, n), lambda i: (i, 0)),
        out_shape=jax.ShapeDtypeStruct((m, n), F32),
        compiler_params=_cparams(("parallel",)),
        name="router_top2",
    )(x, w_pad)


def _row_copy(src_hbm, dst_vmem, sem, src_row, dst_row):
    return pltpu.make_async_copy(src_hbm.at[pl.ds(src_row, 1), :], dst_vmem.at[pl.ds(dst_row, 1), :], sem)


def _pack_bf16_pairs(x):
    half = x.shape[1] // 2
    lo = pltpu.bitcast(x[:, :half].astype(BF16).astype(F32), jnp.uint32)
    hi = pltpu.bitcast(x[:, half:].astype(BF16).astype(F32), jnp.uint32)
    return (hi & jnp.uint32(0xFFFF0000)) | (lo >> 16)


def _unpack_bf16_pairs(w):
    lo = pltpu.bitcast(w << 16, F32).astype(BF16)
    hi = pltpu.bitcast(w & jnp.uint32(0xFFFF0000), F32).astype(BF16)
    return jnp.concatenate([lo, hi], axis=1)


def _gather_body(src_ref, nrows_ref, hp_hbm, o_ref, buf, sem, *, tm, unroll):
    i = pl.program_id(0)

    def issue_tile(t, slot):
        def issue(g, c):
            for u in range(unroll):
                r = g * unroll + u
                _row_copy(hp_hbm, buf.at[slot], sem.at[slot], src_ref[t * tm + r], r).start()
            return c

        lax.fori_loop(0, tm // unroll, issue, 0)

    @pl.when(i == 0)
    def _():
        issue_tile(0, 0)

    nxt = i + 1

    @pl.when(jnp.logical_and(nxt < pl.num_programs(0), nxt * tm < nrows_ref[0]))
    def _():
        issue_tile(nxt, nxt % 2)

    used = i * tm < nrows_ref[0]

    @pl.when(jnp.logical_not(used))
    def _():
        o_ref[...] = jnp.zeros_like(o_ref)

    @pl.when(used)
    def _():
        slot = i % 2
        pltpu.make_async_copy(hp_hbm.at[pl.ds(0, tm), :], buf.at[slot], sem.at[slot]).wait()
        o_ref[...] = _unpack_bf16_pairs(buf[slot])


def gather_rows(hp, src_tok, n_used_rows, *, cap, tm):
    half = hp.shape[1]
    return pl.pallas_call(
        functools.partial(_gather_body, tm=tm, unroll=8),
        grid_spec=pltpu.PrefetchScalarGridSpec(
            num_scalar_prefetch=2,
            grid=(cap // tm,),
            in_specs=[pl.BlockSpec(memory_space=pl.ANY)],
            out_specs=pl.BlockSpec((tm, 2 * half), lambda i, src, nused: (i, 0)),
            scratch_shapes=[pltpu.VMEM((2, tm, half), jnp.uint32), pltpu.SemaphoreType.DMA((2,))]),
        out_shape=jax.ShapeDtypeStruct((cap, 2 * half), BF16),
        compiler_params=_cparams(("arbitrary",)),
        name="gather_rows",
    )(src_tok, n_used_rows, hp)


def _grouped_tile(i, n_rows, tm):
    return jnp.minimum(i, (n_rows[0] - 1) // tm)


def _grouped_fresh(i, sub_e, ratio):
    prev = sub_e[(jnp.maximum(i, 1) - 1) * ratio]
    return jnp.logical_or(i == 0, sub_e[i * ratio] != prev)


def _moe_up_body(sub_e, sub_n, n_rows, x_ref, wg_ref, wu_ref, o_ref, wgb_ref, wub_ref, *, tm, sub):
    i = pl.program_id(1)
    ratio = tm // sub
    used = i * tm < n_rows[0]

    @pl.when(jnp.logical_and(used, _grouped_fresh(i, sub_e, ratio)))
    def _():
        wgb_ref[...] = wg_ref[0].astype(BF16)
        wub_ref[...] = wu_ref[0].astype(BF16)

    def chunk(s):
        x = x_ref[s * sub:(s + 1) * sub, :]
        g = jnp.dot(x, wgb_ref[...], preferred_element_type=F32)
        u = jnp.dot(x, wub_ref[...], preferred_element_type=F32)
        o_ref[s * sub:(s + 1) * sub, :] = (_silu(g) * u).astype(o_ref.dtype)

    full = jnp.logical_and(used, sub_n[i * ratio + ratio - 1] > 0)

    @pl.when(full)
    def _():
        for s in range(ratio):
            chunk(s)

    for s in range(ratio):
        live = jnp.logical_and(used, sub_n[i * ratio + s] > 0)

        @pl.when(jnp.logical_and(live, jnp.logical_not(full)))
        def _():
            chunk(s)

        @pl.when(jnp.logical_not(live))
        def _():
            o_ref[s * sub:(s + 1) * sub, :] = jnp.zeros((sub, o_ref.shape[1]), o_ref.dtype)


def moe_swiglu_up(xs, w_gu, sub_e, sub_n, n_rows, *, tm, tf, sub):
    cap, d = xs.shape
    f = w_gu.shape[2] // 2
    nf = f // tf
    ratio = tm // sub

    def x_map(j, i, se, sn, nr):
        return (_grouped_tile(i, nr, tm), 0)

    def wg_map(j, i, se, sn, nr):
        return (se[_grouped_tile(i, nr, tm) * ratio], 0, j)

    def wu_map(j, i, se, sn, nr):
        return (se[_grouped_tile(i, nr, tm) * ratio], 0, nf + j)

    def o_map(j, i, se, sn, nr):
        return (i, j)

    return pl.pallas_call(
        functools.partial(_moe_up_body, tm=tm, sub=sub),
        grid_spec=pltpu.PrefetchScalarGridSpec(
            num_scalar_prefetch=3,
            grid=(nf, cap // tm),
            in_specs=[pl.BlockSpec((tm, d), x_map),
                      pl.BlockSpec((1, d, tf), wg_map),
                      pl.BlockSpec((1, d, tf), wu_map)],
            out_specs=pl.BlockSpec((tm, tf), o_map),
            scratch_shapes=[pltpu.VMEM((d, tf), BF16), pltpu.VMEM((d, tf), BF16)]),
        out_shape=jax.ShapeDtypeStruct((cap, f), BF16),
        compiler_params=_cparams(("arbitrary", "arbitrary")),
        name="moe_swiglu_up",
    )(sub_e, sub_n, n_rows, xs, w_gu, w_gu)


def _moe_down_body(sub_e, sub_n, n_rows, x_ref, w_ref, o_ref, wb_ref, *, tm, sub):
    i = pl.program_id(1)
    ratio = tm // sub
    used = i * tm < n_rows[0]

    @pl.when(jnp.logical_and(used, _grouped_fresh(i, sub_e, ratio)))
    def _():
        wb_ref[...] = w_ref[0].astype(BF16)

    def chunk(s):
        o_ref[s * sub:(s + 1) * sub, :] = jnp.dot(
            x_ref[s * sub:(s + 1) * sub, :], wb_ref[...], preferred_element_type=F32)

    full = jnp.logical_and(used, sub_n[i * ratio + ratio - 1] > 0)

    @pl.when(full)
    def _():
        for s in range(ratio):
            chunk(s)

    for s in range(ratio):
        live = jnp.logical_and(used, sub_n[i * ratio + s] > 0)

        @pl.when(jnp.logical_and(live, jnp.logical_not(full)))
        def _():
            chunk(s)

        @pl.when(jnp.logical_not(live))
        def _():
            o_ref[s * sub:(s + 1) * sub, :] = jnp.zeros((sub, o_ref.shape[1]), o_ref.dtype)


def moe_down(act, w_down, sub_e, sub_n, n_rows, *, tm, tn, sub):
    cap, f = act.shape
    d = w_down.shape[2]
    ratio = tm // sub

    def x_map(j, i, se, sn, nr):
        return (_grouped_tile(i, nr, tm), 0)

    def w_map(j, i, se, sn, nr):
        return (se[_grouped_tile(i, nr, tm) * ratio], 0, j)

    def o_map(j, i, se, sn, nr):
        return (i, j)

    return pl.pallas_call(
        functools.partial(_moe_down_body, tm=tm, sub=sub),
        grid_spec=pltpu.PrefetchScalarGridSpec(
            num_scalar_prefetch=3,
            grid=(d // tn, cap // tm),
            in_specs=[pl.BlockSpec((tm, f), x_map),
                      pl.BlockSpec((1, f, tn), w_map)],
            out_specs=pl.BlockSpec((tm, tn), o_map),
            scratch_shapes=[pltpu.VMEM((f, tn), BF16)]),
        out_shape=jax.ShapeDtypeStruct((cap, d), F32),
        compiler_params=_cparams(("arbitrary", "arbitrary")),
        name="moe_down",
    )(sub_e, sub_n, n_rows, act, w_down)


def _combine_body(dest_ref, y_hbm, gates_ref, res_ref, g_ref, b_ref, o_ref, buf, sem, *, tm, unroll):
    i = pl.program_id(0)

    def issue_tile(t, slot):
        def issue(g, c):
            for u in range(unroll):
                r = g * unroll + u
                for k in range(TOP_K):
                    _row_copy(y_hbm, buf.at[slot, k], sem.at[slot, k],
                              dest_ref[TOP_K * (t * tm + r) + k], r).start()
            return c

        lax.fori_loop(0, tm // unroll, issue, 0)

    @pl.when(i == 0)
    def _():
        issue_tile(0, 0)

    nxt = i + 1

    @pl.when(nxt < pl.num_programs(0))
    def _():
        issue_tile(nxt, nxt % 2)

    slot = i % 2
    gates = gates_ref[...]
    y = None
    for k in range(TOP_K):
        pltpu.make_async_copy(y_hbm.at[pl.ds(0, tm), :], buf.at[slot, k], sem.at[slot, k]).wait()
        term = buf[slot, k] * gates[:, TOP_K + k:TOP_K + k + 1]
        y = term if y is None else y + term
    o_ref[...] = _layer_norm_rows(DEEPNORM_ALPHA * res_ref[...] + y, g_ref[...], b_ref[...])


def moe_combine_post_norm(y_rows, dest, routing, res, g, b, *, tm):
    n_tok, d = res.shape
    lanes = routing.shape[1]
    return pl.pallas_call(
        functools.partial(_combine_body, tm=tm, unroll=4),
        grid_spec=pltpu.PrefetchScalarGridSpec(
            num_scalar_prefetch=1,
            grid=(n_tok // tm,),
            in_specs=[pl.BlockSpec(memory_space=pl.ANY),
                      pl.BlockSpec((tm, lanes), lambda i, dst: (i, 0)),
                      pl.BlockSpec((tm, d), lambda i, dst: (i, 0)),
                      pl.BlockSpec((1, d), lambda i, dst: (0, 0)),
                      pl.BlockSpec((1, d), lambda i, dst: (0, 0))],
            out_specs=pl.BlockSpec((tm, d), lambda i, dst: (i, 0)),
            scratch_shapes=[pltpu.VMEM((2, TOP_K, tm, d), F32), pltpu.SemaphoreType.DMA((2, TOP_K))]),
        out_shape=jax.ShapeDtypeStruct((n_tok, d), F32),
        compiler_params=_cparams(("arbitrary",)),
        name="moe_combine_post_norm",
    )(dest, y_rows, routing, res, g.reshape(1, d), b.reshape(1, d))


def _group_rows(seg, n_seg, tile, sub):
    n_asg = seg.shape[0]
    cap = n_asg + n_seg * tile
    onehot = (seg[:, None] == jnp.arange(n_seg, dtype=jnp.int32)[None, :]).astype(jnp.int32)
    csum = jnp.cumsum(onehot, axis=0)
    rank = jnp.sum(csum * onehot, axis=1) - 1
    counts = csum[-1]
    padded = (counts + tile - 1) // tile * tile
    pad_end = jnp.cumsum(padded)
    pad_start = pad_end - padded
    dest = (pad_start[seg] + rank).astype(jnp.int32)
    src = jnp.zeros((cap,), jnp.int32).at[dest].set(
        jnp.arange(n_asg, dtype=jnp.int32) // TOP_K, unique_indices=True, mode="promise_in_bounds")
    sub_start = jnp.arange(cap // sub, dtype=jnp.int32) * sub
    sub_e = jnp.minimum(jnp.searchsorted(pad_end, sub_start, side="right"), n_seg - 1).astype(jnp.int32)
    sub_n = jnp.clip(counts[sub_e] - (sub_start - pad_start[sub_e]), 0, sub).astype(jnp.int32)
    n_rows = pad_end[-1:].astype(jnp.int32)
    return dest, src, sub_e, sub_n, n_rows, cap


def _gla_layer(h, hb, w_in, w_gate_up, b_gate, g_norm, w_o, ln_g, ln_b):
    bsz, s, d = h.shape
    dk = (d // 2) // GLA_HEADS
    dv = d // GLA_HEADS
    n_main = 2 * GLA_HEADS * dk + 2 * GLA_HEADS * dv
    x2 = hb.reshape(bsz * s, d)
    proj = matmul(x2, w_in[:, :n_main].astype(BF16), tm=_pick(bsz * s, (1024, 512, 256, 128, 64)),
                  tn=_pick(n_main, (1024, 512, 256, 128)), out_dtype=F32)
    w_z = jnp.pad(w_in[:, n_main:], ((0, 0), (0, LANES - GLA_GATE_RANK))).astype(BF16)
    z = matmul(x2, w_z, tm=_pick(bsz * s, (1024, 512, 256, 128, 64)), tn=LANES, out_dtype=F32)
    wgu = jnp.pad(w_gate_up, ((0, LANES - GLA_GATE_RANK), (0, 0))).astype(BF16)
    wgu = wgu.reshape(LANES, GLA_HEADS, dk).transpose(1, 0, 2)
    og = gla_core(proj.reshape(bsz, s, n_main), z.reshape(bsz, s, LANES), wgu,
                  b_gate.reshape(GLA_HEADS, 1, dk), g_norm.reshape(1, dv),
                  heads=GLA_HEADS, dk=dk, dv=dv, step=min(GLA_STEP_TOKENS, s))
    out, outb = matmul_post_norm(og.reshape(bsz * s, d), w_o.astype(BF16), h.reshape(bsz * s, d), ln_g, ln_b,
                                 tm=512, sub=256)
    return out.reshape(bsz, s, d), outb.reshape(bsz, s, d)


def _moba_layer(h, hb, w_q, w_kv, w_o, ln_g, ln_b):
    bsz, s, d = h.shape
    hd = d // MOBA_HEADS
    w_all = jnp.concatenate([w_q, w_kv], axis=1).astype(BF16)
    n_all = w_all.shape[1]
    tn = _pick(math.gcd(d, n_all), (1024, 512, 256, 128))
    qkv = matmul(hb.reshape(bsz * s, d), w_all, tm=_pick(bsz * s, (1024, 512, 256, 128, 64)), tn=tn,
                 out_dtype=BF16, n_scaled=d // tn, scale=hd ** -0.5)
    att = moba_attention(qkv.reshape(bsz, s, n_all), heads=MOBA_HEADS, kv_heads=MOBA_KV_HEADS, hd=hd,
                         blk=MOBA_BLOCK, topk=MOBA_TOPK)
    out, outb, outp = matmul_post_norm(att.reshape(bsz * s, d), w_o.astype(BF16), h.reshape(bsz * s, d),
                                       ln_g, ln_b, tm=512, sub=256, packed=True)
    return out.reshape(bsz, s, d), outb.reshape(bsz, s, d), outp


def _dense_ffn(h, hb, w_gu, w_down, ln_g, ln_b):
    bsz, s, d = h.shape
    f = w_down.shape[0]
    f_pad = -(-f // 512) * 512
    w_gu_b = w_gu.astype(BF16)
    wg = jnp.pad(w_gu_b[:, :f], ((0, 0), (0, f_pad - f)))
    wu = jnp.pad(w_gu_b[:, f:], ((0, 0), (0, f_pad - f)))
    wd = jnp.pad(w_down.astype(BF16), ((0, f_pad - f), (0, 0)))
    act = swiglu_up(hb.reshape(bsz * s, d), wg, wu, tm=_pick(bsz * s, (1024, 512, 256, 128, 64)), tf=512)
    out, outb = matmul_post_norm(act, wd, h.reshape(bsz * s, d), ln_g, ln_b, tm=256, sub=128)
    return out.reshape(bsz, s, d), outb.reshape(bsz, s, d)


def _moe_ffn(h, hb, hp, w_router, w_gu, w_down, ln_g, ln_b):
    bsz, s, d = h.shape
    n_tok = bsz * s
    h2 = h.reshape(n_tok, d)
    w_r = jnp.pad(w_router, ((0, 0), (0, LANES - N_EXPERTS))).astype(BF16)
    routing = router_top2(hb.reshape(n_tok, d), w_r, tm=_pick(n_tok, (1024, 512, 256, 128, 64)),
                          n_experts=N_EXPERTS)
    seg = routing[:, :TOP_K].astype(jnp.int32).reshape(-1)
    tile = min(MOE_TILE, n_tok)
    sub = min(MOE_SUB, tile)
    dest, src, sub_e, sub_n, n_rows, cap = _group_rows(seg, N_EXPERTS, tile, sub)
    xs = gather_rows(hp, src, n_rows, cap=cap, tm=min(GATHER_TILE, tile))
    f = w_down.shape[1]
    act = moe_swiglu_up(xs, w_gu, sub_e, sub_n, n_rows, tm=tile, tf=_pick(f, (512, 256, 128)), sub=sub)
    y_rows = moe_down(act, w_down, sub_e, sub_n, n_rows, tm=min(MOE_DOWN_TILE, tile),
                      tn=_pick(d, (512, 256, 128)), sub=sub)
    out = moe_combine_post_norm(y_rows, dest, routing, h2, ln_g, ln_b, tm=_pick(n_tok, (256, 128, 64)))
    return out.reshape(bsz, s, d)


def kernel(x, w_in_a, w_gate_up_a, b_gate_a, g_norm_a, w_o_a, w_kv_shared, w_q_b, w_o_b,
           ln_mix_g, ln_mix_b, w_gu_dense, w_down_dense, w_router, w_gu_moe, w_down_moe,
           ln_ffn_g, ln_ffn_b):
    h, hb = _gla_layer(x, x.astype(BF16), w_in_a[0], w_gate_up_a[0], b_gate_a[0], g_norm_a[0], w_o_a[0],
                       ln_mix_g[0], ln_mix_b[0])
    h, hb = _dense_ffn(h, hb, w_gu_dense[0], w_down_dense[0], ln_ffn_g[0], ln_ffn_b[0])
    h, hb, hp = _moba_layer(h, hb, w_q_b[0], w_kv_shared, w_o_b[0], ln_mix_g[1], ln_mix_b[1])
    return _moe_ffn(h, hb, hp, w_router[0], w_gu_moe[0], w_down_moe[0], ln_ffn_g[1], ln_ffn_b[1])
```

```python
import functools
import math

import jax
import jax.numpy as jnp
from jax import lax
from jax.experimental import pallas as pl
from jax.experimental.pallas import tpu as pltpu

F32 = jnp.float32
BF16 = jnp.bfloat16

DEPTH = 2
GLA_HEADS = 4
GLA_GATE_RANK = 16
GLA_TAU = 16.0
GLA_CHUNK = 64
MOBA_HEADS = 16
MOBA_KV_HEADS = 4
MOBA_BLOCK = 256
MOBA_TOPK = 3
N_EXPERTS = 8
TOP_K = 2
LN_EPS = 1e-5
RMS_EPS = 1e-6
DEEPNORM_ALPHA = (2 * DEPTH) ** 0.25

LANES = 128
VMEM_LIMIT_BYTES = 56 * 1024 * 1024

GLA_GROUP = 4
GLA_STEP_TOKENS = 512
MOE_TILE = 1024
MOE_SUB = 256
MOE_DOWN_TILE = 512
GATHER_TILE = 512


def _cparams(sem):
    return pltpu.CompilerParams(dimension_semantics=sem, vmem_limit_bytes=VMEM_LIMIT_BYTES)


def _pick(n, prefs):
    for p in prefs:
        if n % p == 0:
            return p
    return n


def _mm_body(x_ref, w_ref, o_ref, *, n_scaled, scale):
    acc = jnp.dot(x_ref[...], w_ref[...], preferred_element_type=F32)
    if n_scaled:
        acc = acc * jnp.where(pl.program_id(0) < n_scaled, scale, 1.0).astype(F32)
    o_ref[...] = acc.astype(o_ref.dtype)


def matmul(x, w, *, tm, tn, out_dtype, n_scaled=0, scale=1.0):
    m, k = x.shape
    n = w.shape[1]
    return pl.pallas_call(
        functools.partial(_mm_body, n_scaled=n_scaled, scale=scale),
        grid=(n // tn, m // tm),
        in_specs=[pl.BlockSpec((tm, k), lambda j, i: (i, 0)),
                  pl.BlockSpec((k, tn), lambda j, i: (0, j))],
        out_specs=pl.BlockSpec((tm, tn), lambda j, i: (i, j)),
        out_shape=jax.ShapeDtypeStruct((m, n), out_dtype),
        compiler_params=_cparams(("parallel", "parallel")),
        name="matmul",
    )(x, w)


def _layer_norm_rows(t, g, b):
    mu = jnp.mean(t, axis=-1, keepdims=True)
    d = t - mu
    var = jnp.mean(d * d, axis=-1, keepdims=True)
    return d * lax.rsqrt(var + LN_EPS) * g + b


def _mm_ln_body(x_ref, w_ref, res_ref, g_ref, b_ref, o_ref, ob_ref, *rest, tm, sub, packed):
    for s in range(tm // sub):
        rows = slice(s * sub, (s + 1) * sub)
        y = jnp.dot(x_ref[rows, :], w_ref[...], preferred_element_type=F32)
        out = _layer_norm_rows(DEEPNORM_ALPHA * res_ref[rows, :] + y, g_ref[...], b_ref[...])
        o_ref[rows, :] = out
        ob_ref[rows, :] = out.astype(BF16)
        if packed:
            rest[0][rows, :] = _pack_bf16_pairs(out)


def matmul_post_norm(x, w, res, g, b, *, tm, sub, packed=False):
    m, k = x.shape
    d = w.shape[1]
    out_specs = [pl.BlockSpec((tm, d), lambda i: (i, 0)), pl.BlockSpec((tm, d), lambda i: (i, 0))]
    out_shape = [jax.ShapeDtypeStruct((m, d), F32), jax.ShapeDtypeStruct((m, d), BF16)]
    if packed:
        out_specs.append(pl.BlockSpec((tm, d // 2), lambda i: (i, 0)))
        out_shape.append(jax.ShapeDtypeStruct((m, d // 2), jnp.uint32))
    return pl.pallas_call(
        functools.partial(_mm_ln_body, tm=tm, sub=sub, packed=packed),
        grid=(m // tm,),
        in_specs=[pl.BlockSpec((tm, k), lambda i: (i, 0)),
                  pl.BlockSpec((k, d), lambda i: (0, 0), pipeline_mode=pl.Buffered(1)),
                  pl.BlockSpec((tm, d), lambda i: (i, 0)),
                  pl.BlockSpec((1, d), lambda i: (0, 0)),
                  pl.BlockSpec((1, d), lambda i: (0, 0))],
        out_specs=out_specs,
        out_shape=out_shape,
        compiler_params=_cparams(("parallel",)),
        name="matmul_post_norm",
    )(x, w, res, g.reshape(1, d), b.reshape(1, d))


def _silu(x):
    return x / (1.0 + jnp.exp(-x))


def _swiglu_body(x_ref, wg_ref, wu_ref, o_ref):
    x = x_ref[...]
    g = jnp.dot(x, wg_ref[...], preferred_element_type=F32)
    u = jnp.dot(x, wu_ref[...], preferred_element_type=F32)
    o_ref[...] = (_silu(g) * u).astype(o_ref.dtype)


def swiglu_up(x, wg, wu, *, tm, tf):
    m, k = x.shape
    f = wg.shape[1]
    return pl.pallas_call(
        _swiglu_body,
        grid=(f // tf, m // tm),
        in_specs=[pl.BlockSpec((tm, k), lambda j, i: (i, 0)),
                  pl.BlockSpec((k, tf), lambda j, i: (0, j)),
                  pl.BlockSpec((k, tf), lambda j, i: (0, j))],
        out_specs=pl.BlockSpec((tm, tf), lambda j, i: (i, j)),
        out_shape=jax.ShapeDtypeStruct((m, f), BF16),
        compiler_params=_cparams(("parallel", "parallel")),
        name="swiglu_up",
    )(x, wg, wu)


def _gla_body(q_ref, k_ref, v_ref, r_ref, z_ref, wgu_ref, bg_ref, gn_ref, o_ref, state_ref,
              *, chunk, group, n_groups, dk):
    @pl.when(pl.program_id(2) == 0)
    def _():
        state_ref[...] = jnp.zeros_like(state_ref)

    gt = group * chunk
    row = lax.broadcasted_iota(jnp.int32, (gt, gt), 0)
    col = lax.broadcasted_iota(jnp.int32, (gt, gt), 1)
    causal = jnp.logical_and(row // chunk == col // chunk, col <= row)
    tril = jnp.where(causal, 1.0, 0.0).astype(BF16)
    nt = (((1,), (1,)), ((), ()))
    tn = (((0,), (0,)), ((), ()))

    def body(g, carry):
        sl = pl.ds(pl.multiple_of(g * gt, gt), gt)
        gate = jnp.dot(z_ref[0, sl, :].astype(BF16), wgu_ref[0], preferred_element_type=F32) + bg_ref[0]
        log_a = (jnp.minimum(gate, 0.0) - jnp.log(1.0 + jnp.exp(-jnp.abs(gate)))) / GLA_TAU
        la_hi = log_a.astype(BF16)
        la_lo = (log_a - la_hi.astype(F32)).astype(BF16)
        b_cum = (jnp.dot(tril, la_hi, preferred_element_type=F32)
                 + jnp.dot(tril, la_lo, preferred_element_type=F32))
        b_last = [b_cum[(c + 1) * chunk - 1:(c + 1) * chunk, :] for c in range(group)]
        b_last_rows = jnp.concatenate([jnp.broadcast_to(b, (chunk, b.shape[1])) for b in b_last], axis=0)
        q = q_ref[0, sl, :] * (dk ** -0.5)
        k = k_ref[0, sl, :]
        q_dec = (q * jnp.exp(b_cum)).astype(BF16)
        k_inv = (k * jnp.exp(-b_cum)).astype(BF16)
        k_end = (k * jnp.exp(b_last_rows - b_cum)).astype(BF16)
        v = v_ref[0, sl, :].astype(BF16)
        att = lax.dot_general(q_dec, k_inv, nt, preferred_element_type=F32)
        att = jnp.where(causal, att, 0.0).astype(BF16)
        o_intra = jnp.dot(att, v, preferred_element_type=F32)
        state_t = state_ref[...]
        outs = []
        for c in range(group):
            rows = slice(c * chunk, (c + 1) * chunk)
            outs.append(o_intra[rows] + lax.dot_general(q_dec[rows], state_t.astype(BF16), nt,
                                                        preferred_element_type=F32))
            state_t = (jnp.exp(b_last[c]) * state_t
                       + lax.dot_general(v[rows], k_end[rows], tn, preferred_element_type=F32))
        state_ref[...] = state_t
        o = jnp.concatenate(outs, axis=0)
        o = o * lax.rsqrt(jnp.mean(o * o, axis=-1, keepdims=True) + RMS_EPS) * gn_ref[...]
        o_ref[0, sl, :] = (o * _silu(r_ref[0, sl, :])).astype(o_ref.dtype)
        return carry

    lax.fori_loop(0, n_groups, body, 0)


def gla_core(proj, z, wgu, bg, gn, *, heads, dk, dv, step):
    bsz, s, _ = proj.shape
    zr = z.shape[-1]
    k_off = heads
    v_off = 2 * heads * dk // dv
    r_off = v_off + heads
    return pl.pallas_call(
        functools.partial(_gla_body, chunk=GLA_CHUNK, group=GLA_GROUP,
                          n_groups=step // (GLA_GROUP * GLA_CHUNK), dk=dk),
        grid=(bsz, heads, s // step),
        in_specs=[pl.BlockSpec((1, step, dk), lambda b, h, t: (b, t, h)),
                  pl.BlockSpec((1, step, dk), lambda b, h, t: (b, t, k_off + h)),
                  pl.BlockSpec((1, step, dv), lambda b, h, t: (b, t, v_off + h)),
                  pl.BlockSpec((1, step, dv), lambda b, h, t: (b, t, r_off + h)),
                  pl.BlockSpec((1, step, zr), lambda b, h, t: (b, t, 0)),
                  pl.BlockSpec((1, zr, dk), lambda b, h, t: (h, 0, 0)),
                  pl.BlockSpec((1, 1, dk), lambda b, h, t: (h, 0, 0)),
                  pl.BlockSpec((1, dv), lambda b, h, t: (0, 0))],
        out_specs=pl.BlockSpec((1, step, dv), lambda b, h, t: (b, t, h)),
        out_shape=jax.ShapeDtypeStruct((bsz, s, heads * dv), BF16),
        scratch_shapes=[pltpu.VMEM((dv, dk), F32)],
        compiler_params=_cparams(("parallel", "parallel", "arbitrary")),
        name="gla_core",
    )(proj, proj, proj, proj, z, wgu, bg, gn)


MASK_PENALTY = -3.0e38
LOG2_E = math.log2(math.e)
SUM_ROWS = 16


def _moba_body(q_ref, k_ref, v_ref, o_ref, kmean_ref, kaug_ref, vt_ref, rhs_ref, m_ref, acc_ref,
               *, blk, n_blk, rep, hd, topk):
    i = pl.program_id(2)
    nq = rep * blk

    @pl.when(i == 0)
    def _():
        lane = lax.broadcasted_iota(jnp.int32, (blk, hd), 1)
        for j in range(n_blk):
            kj = k_ref[0, j * blk:(j + 1) * blk, :]
            kaug_ref[j, :, :hd] = kj
            kaug_ref[j, :, hd:] = jnp.where(lane == j, 1.0, 0.0).astype(BF16)
            kmean_ref[j:j + 1, :] = jnp.sum(kj.astype(F32), axis=0, keepdims=True) / blk
            vt_ref[j, :hd, :] = v_ref[0, j * blk:(j + 1) * blk, :].astype(F32).T.astype(BF16)
            vt_ref[j, hd:, :] = jnp.ones((SUM_ROWS, blk), BF16)
        rhs_ref[hd + n_blk:, :] = jnp.zeros((hd - n_blk, nq), BF16)

    qt = jnp.concatenate([q_ref[0, :, r * hd:(r + 1) * hd].astype(F32).T for r in range(rep)], axis=1)
    qt = qt.astype(BF16)
    gate = jnp.dot(kmean_ref[...].astype(BF16), qt, preferred_element_type=F32)
    blk_id = lax.broadcasted_iota(jnp.int32, (n_blk, nq), 0)
    past = blk_id < i
    gate = jnp.where(past, gate, -jnp.inf)
    beaten = jnp.zeros((n_blk, nq), jnp.int32)
    for c in range(n_blk):
        gc = gate[c:c + 1, :]
        beaten = beaten + jnp.where(gc > gate, 1, jnp.where(gc == gate, jnp.where(c < blk_id, 1, 0), 0))
    penalty = jnp.where(past, jnp.where(beaten < topk, 0.0, MASK_PENALTY),
                        jnp.where(blk_id == i, 0.0, MASK_PENALTY))
    rhs_ref[:hd, :] = qt
    rhs_ref[hd:hd + n_blk, :] = penalty.astype(BF16)

    def accumulate(s, vts, first):
        m_blk = jnp.max(s, axis=0, keepdims=True)
        m_old = None if first else m_ref[...]
        m_new = m_blk if first else jnp.maximum(m_old, m_blk)
        p = jnp.exp2(s - m_new).astype(BF16)
        upd = None
        for n, vt in enumerate(vts):
            term = jnp.dot(vt, p[n * blk:(n + 1) * blk], preferred_element_type=F32)
            upd = term if upd is None else upd + term
        acc_ref[...] = upd if first else jnp.exp2(m_old - m_new) * acc_ref[...] + upd
        m_ref[...] = m_new

    s = jnp.dot(kaug_ref[i], rhs_ref[...], preferred_element_type=F32)
    key = lax.broadcasted_iota(jnp.int32, (blk, nq), 0)
    qpos = lax.rem(lax.broadcasted_iota(jnp.int32, (blk, nq), 1), blk)
    accumulate(jnp.where(key <= qpos, s, -jnp.inf), [vt_ref[i]], True)

    def attend(j0, n):
        keys = kaug_ref[pl.ds(j0, n)].reshape(n * blk, 2 * hd)
        accumulate(jnp.dot(keys, rhs_ref[...], preferred_element_type=F32),
                   [vt_ref[j0 + c] for c in range(n)], False)

    def quad(jj, carry):
        attend(4 * jj, 4)
        return carry

    lax.fori_loop(0, i // 4, quad, 0)

    @pl.when(i % 4 >= 2)
    def _():
        attend(4 * (i // 4), 2)

    @pl.when(i % 2 == 1)
    def _():
        attend(i - 1, 1)

    acc = acc_ref[...]
    out = acc[:hd] * (1.0 / acc[hd:hd + 1])
    for r in range(rep):
        o_ref[0, :, r * hd:(r + 1) * hd] = out[:, r * blk:(r + 1) * blk].T.astype(o_ref.dtype)


def moba_attention(qkv, *, heads, kv_heads, hd, blk, topk):
    bsz, s, _ = qkv.shape
    rep = heads // kv_heads
    n_blk = s // blk
    nq = rep * blk
    assert n_blk <= hd and n_blk % 16 == 0
    k_off = heads
    v_off = heads + kv_heads
    return pl.pallas_call(
        functools.partial(_moba_body, blk=blk, n_blk=n_blk, rep=rep, hd=hd, topk=topk),
        grid=(bsz, kv_heads, n_blk),
        in_specs=[pl.BlockSpec((1, blk, rep * hd), lambda b, g, i: (b, i, g)),
                  pl.BlockSpec((1, s, hd), lambda b, g, i: (b, 0, k_off + g)),
                  pl.BlockSpec((1, s, hd), lambda b, g, i: (b, 0, v_off + g))],
        out_specs=pl.BlockSpec((1, blk, rep * hd), lambda b, g, i: (b, i, g)),
        out_shape=jax.ShapeDtypeStruct((bsz, s, heads * hd), BF16),
        scratch_shapes=[pltpu.VMEM((n_blk, hd), F32),
                        pltpu.VMEM((n_blk, blk, 2 * hd), BF16),
                        pltpu.VMEM((n_blk, hd + SUM_ROWS, blk), BF16),
                        pltpu.VMEM((2 * hd, nq), BF16),
                        pltpu.VMEM((1, nq), F32),
                        pltpu.VMEM((hd + SUM_ROWS, nq), F32)],
        compiler_params=_cparams(("parallel", "parallel", "arbitrary")),
        name="moba_attention",
    )(qkv, qkv, qkv)


def _router_body(x_ref, w_ref, o_ref, *, n_experts):
    logits = jnp.dot(x_ref[...], w_ref[...], preferred_element_type=F32)
    lane = lax.broadcasted_iota(jnp.int32, logits.shape, 1)
    lane_f = lane.astype(F32)
    big = float(logits.shape[1])
    lg = jnp.where(lane < n_experts, logits, -jnp.inf)
    m1 = jnp.max(lg, axis=-1, keepdims=True)
    i1 = jnp.min(jnp.where(lg == m1, lane_f, big), axis=-1, keepdims=True)
    lg2 = jnp.where(lane_f == i1, -jnp.inf, lg)
    m2 = jnp.max(lg2, axis=-1, keepdims=True)
    i2 = jnp.min(jnp.where(lg2 == m2, lane_f, big), axis=-1, keepdims=True)
    t = jnp.exp(m2 - m1)
    g1 = 1.0 / (1.0 + t)
    g2 = t / (1.0 + t)
    o_ref[...] = jnp.where(lane == 0, i1,
                           jnp.where(lane == 1, i2, jnp.where(lane == 2, g1, jnp.where(lane == 3, g2, 0.0))))


def router_top2(x, w_pad, *, tm, n_experts):
    m, k = x.shape
    n = w_pad.shape[1]
    return pl.pallas_call(
        functools.partial(_router_body, n_experts=n_experts),
        grid=(m // tm,),
        in_specs=[pl.BlockSpec((tm, k), lambda i: (i, 0)),
                  pl.BlockSpec((k, n), lambda i: (0, 0))],
        out_specs=pl.BlockSpec((tm, n), lambda i: (i, 0)),
        out_shape=jax.ShapeDtypeStruct((m, n), F32),
        compiler_params=_cparams(("parallel",)),
        name="router_top2",
    )(x, w_pad)


def _row_copy(src_hbm, dst_vmem, sem, src_row, dst_row):
    return pltpu.make_async_copy(src_hbm.at[pl.ds(src_row, 1), :], dst_vmem.at[pl.ds(dst_row, 1), :], sem)


def _pack_bf16_pairs(x):
    half = x.shape[1] // 2
    lo = pltpu.bitcast(x[:, :half].astype(BF16).astype(F32), jnp.uint32)
    hi = pltpu.bitcast(x[:, half:].astype(BF16).astype(F32), jnp.uint32)
    return (hi & jnp.uint32(0xFFFF0000)) | (lo >> 16)


def _unpack_bf16_pairs(w):
    lo = pltpu.bitcast(w << 16, F32).astype(BF16)
    hi = pltpu.bitcast(w & jnp.uint32(0xFFFF0000), F32).astype(BF16)
    return jnp.concatenate([lo, hi], axis=1)


def _gather_body(src_ref, nrows_ref, hp_hbm, o_ref, buf, sem, *, tm, unroll):
    i = pl.program_id(0)

    def issue_tile(t, slot):
        def issue(g, c):
            for u in range(unroll):
                r = g * unroll + u
                _row_copy(hp_hbm, buf.at[slot], sem.at[slot], src_ref[t * tm + r], r).start()
            return c

        lax.fori_loop(0, tm // unroll, issue, 0)

    @pl.when(i == 0)
    def _():
        issue_tile(0, 0)

    nxt = i + 1

    @pl.when(jnp.logical_and(nxt < pl.num_programs(0), nxt * tm < nrows_ref[0]))
    def _():
        issue_tile(nxt, nxt % 2)

    used = i * tm < nrows_ref[0]

    @pl.when(jnp.logical_not(used))
    def _():
        o_ref[...] = jnp.zeros_like(o_ref)

    @pl.when(used)
    def _():
        slot = i % 2
        pltpu.make_async_copy(hp_hbm.at[pl.ds(0, tm), :], buf.at[slot], sem.at[slot]).wait()
        o_ref[...] = _unpack_bf16_pairs(buf[slot])


def gather_rows(hp, src_tok, n_used_rows, *, cap, tm):
    half = hp.shape[1]
    return pl.pallas_call(
        functools.partial(_gather_body, tm=tm, unroll=8),
        grid_spec=pltpu.PrefetchScalarGridSpec(
            num_scalar_prefetch=2,
            grid=(cap // tm,),
            in_specs=[pl.BlockSpec(memory_space=pl.ANY)],
            out_specs=pl.BlockSpec((tm, 2 * half), lambda i, src, nused: (i, 0)),
            scratch_shapes=[pltpu.VMEM((2, tm, half), jnp.uint32), pltpu.SemaphoreType.DMA((2,))]),
        out_shape=jax.ShapeDtypeStruct((cap, 2 * half), BF16),
        compiler_params=_cparams(("arbitrary",)),
        name="gather_rows",
    )(src_tok, n_used_rows, hp)


def _grouped_tile(i, n_rows, tm):
    return jnp.minimum(i, (n_rows[0] - 1) // tm)


def _grouped_fresh(i, sub_e, ratio):
    prev = sub_e[(jnp.maximum(i, 1) - 1) * ratio]
    return jnp.logical_or(i == 0, sub_e[i * ratio] != prev)


def _moe_up_body(sub_e, sub_n, n_rows, x_ref, wg_ref, wu_ref, o_ref, wgb_ref, wub_ref, *, tm, sub):
    i = pl.program_id(1)
    ratio = tm // sub
    used = i * tm < n_rows[0]

    @pl.when(jnp.logical_and(used, _grouped_fresh(i, sub_e, ratio)))
    def _():
        wgb_ref[...] = wg_ref[0].astype(BF16)
        wub_ref[...] = wu_ref[0].astype(BF16)

    def chunk(s):
        x = x_ref[s * sub:(s + 1) * sub, :]
        g = jnp.dot(x, wgb_ref[...], preferred_element_type=F32)
        u = jnp.dot(x, wub_ref[...], preferred_element_type=F32)
        o_ref[s * sub:(s + 1) * sub, :] = (_silu(g) * u).astype(o_ref.dtype)

    full = jnp.logical_and(used, sub_n[i * ratio + ratio - 1] > 0)

    @pl.when(full)
    def _():
        for s in range(ratio):
            chunk(s)

    for s in range(ratio):
        live = jnp.logical_and(used, sub_n[i * ratio + s] > 0)

        @pl.when(jnp.logical_and(live, jnp.logical_not(full)))
        def _():
            chunk(s)

        @pl.when(jnp.logical_not(live))
        def _():
            o_ref[s * sub:(s + 1) * sub, :] = jnp.zeros((sub, o_ref.shape[1]), o_ref.dtype)


def moe_swiglu_up(xs, w_gu, sub_e, sub_n, n_rows, *, tm, tf, sub):
    cap, d = xs.shape
    f = w_gu.shape[2] // 2
    nf = f // tf
    ratio = tm // sub

    def x_map(j, i, se, sn, nr):
        return (_grouped_tile(i, nr, tm), 0)

    def wg_map(j, i, se, sn, nr):
        return (se[_grouped_tile(i, nr, tm) * ratio], 0, j)

    def wu_map(j, i, se, sn, nr):
        return (se[_grouped_tile(i, nr, tm) * ratio], 0, nf + j)

    def o_map(j, i, se, sn, nr):
        return (i, j)

    return pl.pallas_call(
        functools.partial(_moe_up_body, tm=tm, sub=sub),
        grid_spec=pltpu.PrefetchScalarGridSpec(
            num_scalar_prefetch=3,
            grid=(nf, cap // tm),
            in_specs=[pl.BlockSpec((tm, d), x_map),
                      pl.BlockSpec((1, d, tf), wg_map),
                      pl.BlockSpec((1, d, tf), wu_map)],
            out_specs=pl.BlockSpec((tm, tf), o_map),
            scratch_shapes=[pltpu.VMEM((d, tf), BF16), pltpu.VMEM((d, tf), BF16)]),
        out_shape=jax.ShapeDtypeStruct((cap, f), BF16),
        compiler_params=_cparams(("arbitrary", "arbitrary")),
        name="moe_swiglu_up",
    )(sub_e, sub_n, n_rows, xs, w_gu, w_gu)


def _moe_down_body(sub_e, sub_n, n_rows, x_ref, w_ref, o_ref, wb_ref, *, tm, sub):
    i = pl.program_id(1)
    ratio = tm // sub
    used = i * tm < n_rows[0]

    @pl.when(jnp.logical_and(used, _grouped_fresh(i, sub_e, ratio)))
    def _():
        wb_ref[...] = w_ref[0].astype(BF16)

    def chunk(s):
        o_ref[s * sub:(s + 1) * sub, :] = jnp.dot(
            x_ref[s * sub:(s + 1) * sub, :], wb_ref[...], preferred_element_type=F32)

    full = jnp.logical_and(used, sub_n[i * ratio + ratio - 1] > 0)

    @pl.when(full)
    def _():
        for s in range(ratio):
            chunk(s)

    for s in range(ratio):
        live = jnp.logical_and(used, sub_n[i * ratio + s] > 0)

        @pl.when(jnp.logical_and(live, jnp.logical_not(full)))
        def _():
            chunk(s)

        @pl.when(jnp.logical_not(live))
        def _():
            o_ref[s * sub:(s + 1) * sub, :] = jnp.zeros((sub, o_ref.shape[1]), o_ref.dtype)


def moe_down(act, w_down, sub_e, sub_n, n_rows, *, tm, tn, sub):
    cap, f = act.shape
    d = w_down.shape[2]
    ratio = tm // sub

    def x_map(j, i, se, sn, nr):
        return (_grouped_tile(i, nr, tm), 0)

    def w_map(j, i, se, sn, nr):
        return (se[_grouped_tile(i, nr, tm) * ratio], 0, j)

    def o_map(j, i, se, sn, nr):
        return (i, j)

    return pl.pallas_call(
        functools.partial(_moe_down_body, tm=tm, sub=sub),
        grid_spec=pltpu.PrefetchScalarGridSpec(
            num_scalar_prefetch=3,
            grid=(d // tn, cap // tm),
            in_specs=[pl.BlockSpec((tm, f), x_map),
                      pl.BlockSpec((1, f, tn), w_map)],
            out_specs=pl.BlockSpec((tm, tn), o_map),
            scratch_shapes=[pltpu.VMEM((f, tn), BF16)]),
        out_shape=jax.ShapeDtypeStruct((cap, d), F32),
        compiler_params=_cparams(("arbitrary", "arbitrary")),
        name="moe_down",
    )(sub_e, sub_n, n_rows, act, w_down)


def _combine_body(dest_ref, y_hbm, gates_ref, res_ref, g_ref, b_ref, o_ref, buf, sem, *, tm, unroll):
    i = pl.program_id(0)

    def issue_tile(t, slot):
        def issue(g, c):
            for u in range(unroll):
                r = g * unroll + u
                for k in range(TOP_K):
                    _row_copy(y_hbm, buf.at[slot, k], sem.at[slot, k],
                              dest_ref[TOP_K * (t * tm + r) + k], r).start()
            return c

        lax.fori_loop(0, tm // unroll, issue, 0)

    @pl.when(i == 0)
    def _():
        issue_tile(0, 0)

    nxt = i + 1

    @pl.when(nxt < pl.num_programs(0))
    def _():
        issue_tile(nxt, nxt % 2)

    slot = i % 2
    gates = gates_ref[...]
    y = None
    for k in range(TOP_K):
        pltpu.make_async_copy(y_hbm.at[pl.ds(0, tm), :], buf.at[slot, k], sem.at[slot, k]).wait()
        term = buf[slot, k] * gates[:, TOP_K + k:TOP_K + k + 1]
        y = term if y is None else y + term
    o_ref[...] = _layer_norm_rows(DEEPNORM_ALPHA * res_ref[...] + y, g_ref[...], b_ref[...])


def moe_combine_post_norm(y_rows, dest, routing, res, g, b, *, tm):
    n_tok, d = res.shape
    lanes = routing.shape[1]
    return pl.pallas_call(
        functools.partial(_combine_body, tm=tm, unroll=4),
        grid_spec=pltpu.PrefetchScalarGridSpec(
            num_scalar_prefetch=1,
            grid=(n_tok // tm,),
            in_specs=[pl.BlockSpec(memory_space=pl.ANY),
                      pl.BlockSpec((tm, lanes), lambda i, dst: (i, 0)),
                      pl.BlockSpec((tm, d), lambda i, dst: (i, 0)),
                      pl.BlockSpec((1, d), lambda i, dst: (0, 0)),
                      pl.BlockSpec((1, d), lambda i, dst: (0, 0))],
            out_specs=pl.BlockSpec((tm, d), lambda i, dst: (i, 0)),
            scratch_shapes=[pltpu.VMEM((2, TOP_K, tm, d), F32), pltpu.SemaphoreType.DMA((2, TOP_K))]),
        out_shape=jax.ShapeDtypeStruct((n_tok, d), F32),
        compiler_params=_cparams(("arbitrary",)),
        name="moe_combine_post_norm",
    )(dest, y_rows, routing, res, g.reshape(1, d), b.reshape(1, d))


def _group_rows(seg, n_seg, tile, sub):
    n_asg = seg.shape[0]
    cap = n_asg + n_seg * tile
    onehot = (seg[:, None] == jnp.arange(n_seg, dtype=jnp.int32)[None, :]).astype(jnp.int32)
    csum = jnp.cumsum(onehot, axis=0)
    rank = jnp.sum(csum * onehot, axis=1) - 1
    counts = csum[-1]
    padded = (counts + tile - 1) // tile * tile
    pad_end = jnp.cumsum(padded)
    pad_start = pad_end - padded
    dest = (pad_start[seg] + rank).astype(jnp.int32)
    src = jnp.zeros((cap,), jnp.int32).at[dest].set(
        jnp.arange(n_asg, dtype=jnp.int32) // TOP_K, unique_indices=True, mode="promise_in_bounds")
    sub_start = jnp.arange(cap // sub, dtype=jnp.int32) * sub
    sub_e = jnp.minimum(jnp.searchsorted(pad_end, sub_start, side="right"), n_seg - 1).astype(jnp.int32)
    sub_n = jnp.clip(counts[sub_e] - (sub_start - pad_start[sub_e]), 0, sub).astype(jnp.int32)
    n_rows = pad_end[-1:].astype(jnp.int32)
    return dest, src, sub_e, sub_n, n_rows, cap


def _gla_layer(h, hb, w_in, w_gate_up, b_gate, g_norm, w_o, ln_g, ln_b):
    bsz, s, d = h.shape
    dk = (d // 2) // GLA_HEADS
    dv = d // GLA_HEADS
    n_main = 2 * GLA_HEADS * dk + 2 * GLA_HEADS * dv
    x2 = hb.reshape(bsz * s, d)
    proj = matmul(x2, w_in[:, :n_main].astype(BF16), tm=_pick(bsz * s, (1024, 512, 256, 128, 64)),
                  tn=_pick(n_main, (1024, 512, 256, 128)), out_dtype=F32)
    w_z = jnp.pad(w_in[:, n_main:], ((0, 0), (0, LANES - GLA_GATE_RANK))).astype(BF16)
    z = matmul(x2, w_z, tm=_pick(bsz * s, (1024, 512, 256, 128, 64)), tn=LANES, out_dtype=F32)
    wgu = jnp.pad(w_gate_up, ((0, LANES - GLA_GATE_RANK), (0, 0))).astype(BF16)
    wgu = wgu.reshape(LANES, GLA_HEADS, dk).transpose(1, 0, 2)
    og = gla_core(proj.reshape(bsz, s, n_main), z.reshape(bsz, s, LANES), wgu,
                  b_gate.reshape(GLA_HEADS, 1, dk), g_norm.reshape(1, dv),
                  heads=GLA_HEADS, dk=dk, dv=dv, step=min(GLA_STEP_TOKENS, s))
    out, outb = matmul_post_norm(og.reshape(bsz * s, d), w_o.astype(BF16), h.reshape(bsz * s, d), ln_g, ln_b,
                                 tm=512, sub=256)
    return out.reshape(bsz, s, d), outb.reshape(bsz, s, d)


def _moba_layer(h, hb, w_q, w_kv, w_o, ln_g, ln_b):
    bsz, s, d = h.shape
    hd = d // MOBA_HEADS
    w_all = jnp.concatenate([w_q, w_kv], axis=1).astype(BF16)
    n_all = w_all.shape[1]
    tn = _pick(math.gcd(d, n_all), (1024, 512, 256, 128))
    qkv = matmul(hb.reshape(bsz * s, d), w_all, tm=_pick(bsz * s, (1024, 512, 256, 128, 64)), tn=tn,
                 out_dtype=BF16, n_scaled=d // tn, scale=hd ** -0.5 * LOG2_E)
    att = moba_attention(qkv.reshape(bsz, s, n_all), heads=MOBA_HEADS, kv_heads=MOBA_KV_HEADS, hd=hd,
                         blk=MOBA_BLOCK, topk=MOBA_TOPK)
    out, outb, outp = matmul_post_norm(att.reshape(bsz * s, d), w_o.astype(BF16), h.reshape(bsz * s, d),
                                       ln_g, ln_b, tm=512, sub=256, packed=True)
    return out.reshape(bsz, s, d), outb.reshape(bsz, s, d), outp


def _dense_ffn(h, hb, w_gu, w_down, ln_g, ln_b):
    bsz, s, d = h.shape
    f = w_down.shape[0]
    f_pad = -(-f // 512) * 512
    w_gu_b = w_gu.astype(BF16)
    wg = jnp.pad(w_gu_b[:, :f], ((0, 0), (0, f_pad - f)))
    wu = jnp.pad(w_gu_b[:, f:], ((0, 0), (0, f_pad - f)))
    wd = jnp.pad(w_down.astype(BF16), ((0, f_pad - f), (0, 0)))
    act = swiglu_up(hb.reshape(bsz * s, d), wg, wu, tm=_pick(bsz * s, (1024, 512, 256, 128, 64)), tf=512)
    out, outb = matmul_post_norm(act, wd, h.reshape(bsz * s, d), ln_g, ln_b, tm=256, sub=128)
    return out.reshape(bsz, s, d), outb.reshape(bsz, s, d)


def _moe_ffn(h, hb, hp, w_router, w_gu, w_down, ln_g, ln_b):
    bsz, s, d = h.shape
    n_tok = bsz * s
    h2 = h.reshape(n_tok, d)
    w_r = jnp.pad(w_router, ((0, 0), (0, LANES - N_EXPERTS))).astype(BF16)
    routing = router_top2(hb.reshape(n_tok, d), w_r, tm=_pick(n_tok, (1024, 512, 256, 128, 64)),
                          n_experts=N_EXPERTS)
    seg = routing[:, :TOP_K].astype(jnp.int32).reshape(-1)
    tile = min(MOE_TILE, n_tok)
    sub = min(MOE_SUB, tile)
    dest, src, sub_e, sub_n, n_rows, cap = _group_rows(seg, N_EXPERTS, tile, sub)
    xs = gather_rows(hp, src, n_rows, cap=cap, tm=min(GATHER_TILE, tile))
    f = w_down.shape[1]
    act = moe_swiglu_up(xs, w_gu, sub_e, sub_n, n_rows, tm=tile, tf=_pick(f, (512, 256, 128)), sub=sub)
    y_rows = moe_down(act, w_down, sub_e, sub_n, n_rows, tm=min(MOE_DOWN_TILE, tile),
                      tn=_pick(d, (512, 256, 128)), sub=sub)
    out = moe_combine_post_norm(y_rows, dest, routing, h2, ln_g, ln_b, tm=_pick(n_tok, (256, 128, 64)))
    return out.reshape(bsz, s, d)


def kernel(x, w_in_a, w_gate_up_a, b_gate_a, g_norm_a, w_o_a, w_kv_shared, w_q_b, w_o_b,
           ln_mix_g, ln_mix_b, w_gu_dense, w_down_dense, w_router, w_gu_moe, w_down_moe,
           ln_ffn_g, ln_ffn_b):
    h, hb = _gla_layer(x, x.astype(BF16), w_in_a[0], w_gate_up_a[0], b_gate_a[0], g_norm_a[0], w_o_a[0],
                       ln_mix_g[0], ln_mix_b[0])
    h, hb = _dense_ffn(h, hb, w_gu_dense[0], w_down_dense[0], ln_ffn_g[0], ln_ffn_b[0])
    h, hb, hp = _moba_layer(h, hb, w_q_b[0], w_kv_shared, w_o_b[0], ln_mix_g[1], ln_mix_b[1])
    return _moe_ffn(h, hb, hp, w_router[0], w_gu_moe[0], w_down_moe[0], ln_ffn_g[1], ln_ffn_b[1])
```

```python
import functools
import math

import jax
import jax.numpy as jnp
from jax import lax
from jax.experimental import pallas as pl
from jax.experimental.pallas import tpu as pltpu

F32 = jnp.float32
BF16 = jnp.bfloat16

DEPTH = 2
GLA_HEADS = 4
GLA_GATE_RANK = 16
GLA_TAU = 16.0
GLA_CHUNK = 64
MOBA_HEADS = 16
MOBA_KV_HEADS = 4
MOBA_BLOCK = 256
MOBA_TOPK = 3
N_EXPERTS = 8
TOP_K = 2
LN_EPS = 1e-5
RMS_EPS = 1e-6
DEEPNORM_ALPHA = (2 * DEPTH) ** 0.25

LANES = 128
VMEM_LIMIT_BYTES = 56 * 1024 * 1024

GLA_GROUP = 4
GLA_STEP_TOKENS = 512
MOE_TILE = 1024
MOE_SUB = 256
MOE_DOWN_TILE = 512
GATHER_TILE = 512


def _cparams(sem):
    return pltpu.CompilerParams(dimension_semantics=sem, vmem_limit_bytes=VMEM_LIMIT_BYTES)


def _pick(n, prefs):
    for p in prefs:
        if n % p == 0:
            return p
    return n


def _mm_body(x_ref, w_ref, o_ref, *, n_scaled, scale):
    acc = jnp.dot(x_ref[...], w_ref[...], preferred_element_type=F32)
    if n_scaled:
        acc = acc * jnp.where(pl.program_id(0) < n_scaled, scale, 1.0).astype(F32)
    o_ref[...] = acc.astype(o_ref.dtype)


def matmul(x, w, *, tm, tn, out_dtype, n_scaled=0, scale=1.0):
    m, k = x.shape
    n = w.shape[1]
    return pl.pallas_call(
        functools.partial(_mm_body, n_scaled=n_scaled, scale=scale),
        grid=(n // tn, m // tm),
        in_specs=[pl.BlockSpec((tm, k), lambda j, i: (i, 0)),
                  pl.BlockSpec((k, tn), lambda j, i: (0, j))],
        out_specs=pl.BlockSpec((tm, tn), lambda j, i: (i, j)),
        out_shape=jax.ShapeDtypeStruct((m, n), out_dtype),
        compiler_params=_cparams(("parallel", "parallel")),
        name="matmul",
    )(x, w)


def _layer_norm_rows(t, g, b):
    mu = jnp.mean(t, axis=-1, keepdims=True)
    d = t - mu
    var = jnp.mean(d * d, axis=-1, keepdims=True)
    return d * lax.rsqrt(var + LN_EPS) * g + b


def _mm_ln_body(x_ref, w_ref, res_ref, g_ref, b_ref, o_ref, ob_ref, *rest, tm, sub, packed):
    for s in range(tm // sub):
        rows = slice(s * sub, (s + 1) * sub)
        y = jnp.dot(x_ref[rows, :], w_ref[...], preferred_element_type=F32)
        out = _layer_norm_rows(DEEPNORM_ALPHA * res_ref[rows, :] + y, g_ref[...], b_ref[...])
        o_ref[rows, :] = out
        ob_ref[rows, :] = out.astype(BF16)
        if packed:
            rest[0][rows, :] = _pack_bf16_pairs(out)


def matmul_post_norm(x, w, res, g, b, *, tm, sub, packed=False):
    m, k = x.shape
    d = w.shape[1]
    out_specs = [pl.BlockSpec((tm, d), lambda i: (i, 0)), pl.BlockSpec((tm, d), lambda i: (i, 0))]
    out_shape = [jax.ShapeDtypeStruct((m, d), F32), jax.ShapeDtypeStruct((m, d), BF16)]
    if packed:
        out_specs.append(pl.BlockSpec((tm, d // 2), lambda i: (i, 0)))
        out_shape.append(jax.ShapeDtypeStruct((m, d // 2), jnp.uint32))
    return pl.pallas_call(
        functools.partial(_mm_ln_body, tm=tm, sub=sub, packed=packed),
        grid=(m // tm,),
        in_specs=[pl.BlockSpec((tm, k), lambda i: (i, 0)),
                  pl.BlockSpec((k, d), lambda i: (0, 0), pipeline_mode=pl.Buffered(1)),
                  pl.BlockSpec((tm, d), lambda i: (i, 0)),
                  pl.BlockSpec((1, d), lambda i: (0, 0)),
                  pl.BlockSpec((1, d), lambda i: (0, 0))],
        out_specs=out_specs,
        out_shape=out_shape,
        compiler_params=_cparams(("parallel",)),
        name="matmul_post_norm",
    )(x, w, res, g.reshape(1, d), b.reshape(1, d))


def _silu(x):
    return x / (1.0 + jnp.exp(-x))


def _swiglu_body(x_ref, wg_ref, wu_ref, o_ref):
    x = x_ref[...]
    g = jnp.dot(x, wg_ref[...], preferred_element_type=F32)
    u = jnp.dot(x, wu_ref[...], preferred_element_type=F32)
    o_ref[...] = (_silu(g) * u).astype(o_ref.dtype)


def swiglu_up(x, wg, wu, *, tm, tf):
    m, k = x.shape
    f = wg.shape[1]
    return pl.pallas_call(
        _swiglu_body,
        grid=(f // tf, m // tm),
        in_specs=[pl.BlockSpec((tm, k), lambda j, i: (i, 0)),
                  pl.BlockSpec((k, tf), lambda j, i: (0, j)),
                  pl.BlockSpec((k, tf), lambda j, i: (0, j))],
        out_specs=pl.BlockSpec((tm, tf), lambda j, i: (i, j)),
        out_shape=jax.ShapeDtypeStruct((m, f), BF16),
        compiler_params=_cparams(("parallel", "parallel")),
        name="swiglu_up",
    )(x, wg, wu)


def _gla_body(q_ref, k_ref, v_ref, r_ref, z_ref, wgu_ref, bg_ref, gn_ref, o_ref, state_ref,
              *, chunk, group, n_groups, dk):
    @pl.when(pl.program_id(2) == 0)
    def _():
        state_ref[...] = jnp.zeros_like(state_ref)

    gt = group * chunk
    row = lax.broadcasted_iota(jnp.int32, (gt, gt), 0)
    col = lax.broadcasted_iota(jnp.int32, (gt, gt), 1)
    causal = jnp.logical_and(row // chunk == col // chunk, col <= row)
    tril = jnp.where(causal, 1.0, 0.0).astype(BF16)
    nt = (((1,), (1,)), ((), ()))
    tn = (((0,), (0,)), ((), ()))

    def body(g, carry):
        sl = pl.ds(pl.multiple_of(g * gt, gt), gt)
        gate = jnp.dot(z_ref[0, sl, :].astype(BF16), wgu_ref[0], preferred_element_type=F32) + bg_ref[0]
        log_a = (jnp.minimum(gate, 0.0) - jnp.log(1.0 + jnp.exp(-jnp.abs(gate)))) / GLA_TAU
        la_hi = log_a.astype(BF16)
        la_lo = (log_a - la_hi.astype(F32)).astype(BF16)
        b_cum = (jnp.dot(tril, la_hi, preferred_element_type=F32)
                 + jnp.dot(tril, la_lo, preferred_element_type=F32))
        b_last = [b_cum[(c + 1) * chunk - 1:(c + 1) * chunk, :] for c in range(group)]
        b_last_rows = jnp.concatenate([jnp.broadcast_to(b, (chunk, b.shape[1])) for b in b_last], axis=0)
        q = q_ref[0, sl, :] * (dk ** -0.5)
        k = k_ref[0, sl, :]
        q_dec = (q * jnp.exp(b_cum)).astype(BF16)
        k_inv = (k * jnp.exp(-b_cum)).astype(BF16)
        k_end = (k * jnp.exp(b_last_rows - b_cum)).astype(BF16)
        v = v_ref[0, sl, :]
        att = lax.dot_general(q_dec, k_inv, nt, preferred_element_type=F32)
        att = jnp.where(causal, att, 0.0).astype(BF16)
        o_intra = jnp.dot(att, v, preferred_element_type=F32)
        state_t = state_ref[...]
        outs = []
        for c in range(group):
            rows = slice(c * chunk, (c + 1) * chunk)
            outs.append(o_intra[rows] + lax.dot_general(q_dec[rows], state_t.astype(BF16), nt,
                                                        preferred_element_type=F32))
            state_t = (jnp.exp(b_last[c]) * state_t
                       + lax.dot_general(v[rows], k_end[rows], tn, preferred_element_type=F32))
        state_ref[...] = state_t
        o = jnp.concatenate(outs, axis=0)
        o = o * lax.rsqrt(jnp.mean(o * o, axis=-1, keepdims=True) + RMS_EPS) * gn_ref[...]
        o_ref[0, sl, :] = (o * _silu(r_ref[0, sl, :].astype(F32))).astype(o_ref.dtype)
        return carry

    lax.fori_loop(0, n_groups, body, 0)


def gla_core(qk, vr, z, wgu, bg, gn, *, heads, dk, dv, step):
    bsz, s, _ = qk.shape
    zr = z.shape[-1]
    k_off = heads
    r_off = heads
    return pl.pallas_call(
        functools.partial(_gla_body, chunk=GLA_CHUNK, group=GLA_GROUP,
                          n_groups=step // (GLA_GROUP * GLA_CHUNK), dk=dk),
        grid=(bsz, heads, s // step),
        in_specs=[pl.BlockSpec((1, step, dk), lambda b, h, t: (b, t, h)),
                  pl.BlockSpec((1, step, dk), lambda b, h, t: (b, t, k_off + h)),
                  pl.BlockSpec((1, step, dv), lambda b, h, t: (b, t, h)),
                  pl.BlockSpec((1, step, dv), lambda b, h, t: (b, t, r_off + h)),
                  pl.BlockSpec((1, step, zr), lambda b, h, t: (b, t, 0)),
                  pl.BlockSpec((1, zr, dk), lambda b, h, t: (h, 0, 0)),
                  pl.BlockSpec((1, 1, dk), lambda b, h, t: (h, 0, 0)),
                  pl.BlockSpec((1, dv), lambda b, h, t: (0, 0))],
        out_specs=pl.BlockSpec((1, step, dv), lambda b, h, t: (b, t, h)),
        out_shape=jax.ShapeDtypeStruct((bsz, s, heads * dv), BF16),
        scratch_shapes=[pltpu.VMEM((dv, dk), F32)],
        compiler_params=_cparams(("parallel", "parallel", "arbitrary")),
        name="gla_core",
    )(qk, qk, vr, vr, z, wgu, bg, gn)


MASK_PENALTY = -3.0e38
LOG2_E = math.log2(math.e)
SUM_ROWS = 16


def _moba_body(q_ref, k_ref, v_ref, o_ref, kmean_ref, kaug_ref, vt_ref, rhs_ref, m_ref, acc_ref,
               *, blk, n_blk, rep, hd, topk):
    i = pl.program_id(2)
    nq = rep * blk

    @pl.when(i == 0)
    def _():
        lane = lax.broadcasted_iota(jnp.int32, (blk, hd), 1)
        for j in range(n_blk):
            kj = k_ref[0, j * blk:(j + 1) * blk, :]
            kaug_ref[j, :, :hd] = kj
            kaug_ref[j, :, hd:] = jnp.where(lane == j, 1.0, 0.0).astype(BF16)
            kmean_ref[j:j + 1, :] = jnp.sum(kj.astype(F32), axis=0, keepdims=True) / blk
            vt_ref[j, :hd, :] = v_ref[0, j * blk:(j + 1) * blk, :].astype(F32).T.astype(BF16)
            vt_ref[j, hd:, :] = jnp.ones((SUM_ROWS, blk), BF16)
        rhs_ref[hd + n_blk:, :] = jnp.zeros((hd - n_blk, nq), BF16)

    qt = jnp.concatenate([q_ref[0, :, r * hd:(r + 1) * hd].astype(F32).T for r in range(rep)], axis=1)
    qt = qt.astype(BF16)
    gate = jnp.dot(kmean_ref[...].astype(BF16), qt, preferred_element_type=F32)
    blk_id = lax.broadcasted_iota(jnp.int32, (n_blk, nq), 0)
    past = blk_id < i
    gate = jnp.where(past, gate, -jnp.inf)
    beaten = jnp.zeros((n_blk, nq), jnp.int32)
    for c in range(n_blk):
        gc = gate[c:c + 1, :]
        beaten = beaten + jnp.where(gc > gate, 1, jnp.where(gc == gate, jnp.where(c < blk_id, 1, 0), 0))
    penalty = jnp.where(past, jnp.where(beaten < topk, 0.0, MASK_PENALTY),
                        jnp.where(blk_id == i, 0.0, MASK_PENALTY))
    rhs_ref[:hd, :] = qt
    rhs_ref[hd:hd + n_blk, :] = penalty.astype(BF16)

    m_ref[...] = jnp.full((1, nq), -jnp.inf, F32)
    acc_ref[...] = jnp.zeros_like(acc_ref)
    group = 4

    def attend(j0, n, own):
        keys = kaug_ref[pl.ds(j0, n)].reshape(n * blk, 2 * hd)
        s = jnp.dot(keys, rhs_ref[...], preferred_element_type=F32)
        if own:
            key = lax.broadcasted_iota(jnp.int32, (blk, nq), 0)
            qpos = lax.rem(lax.broadcasted_iota(jnp.int32, (blk, nq), 1), blk)
            tail = jnp.where(key <= qpos, s[(n - 1) * blk:], -jnp.inf)
            s = tail if n == 1 else jnp.concatenate([s[:(n - 1) * blk], tail], axis=0)
        m_old = m_ref[...]
        m_new = jnp.maximum(m_old, jnp.max(s, axis=0, keepdims=True))
        p = jnp.exp2(s - m_new).astype(BF16)
        upd = jnp.exp2(m_old - m_new) * acc_ref[...]
        for c in range(n):
            upd = upd + jnp.dot(vt_ref[j0 + c], p[c * blk:(c + 1) * blk], preferred_element_type=F32)
        acc_ref[...] = upd
        m_ref[...] = m_new

    n_last = i + 1 - (i // group) * group

    def full_group(jj, carry):
        attend(group * jj, group, False)
        return carry

    lax.fori_loop(0, i // group, full_group, 0)
    for n in range(1, group + 1):
        @pl.when(n_last == n)
        def _():
            attend(i + 1 - n, n, True)

    acc = acc_ref[...]
    out = acc[:hd] * (1.0 / acc[hd:hd + 1])
    for r in range(rep):
        o_ref[0, :, r * hd:(r + 1) * hd] = out[:, r * blk:(r + 1) * blk].T.astype(o_ref.dtype)


def moba_attention(qkv, *, heads, kv_heads, hd, blk, topk):
    bsz, s, _ = qkv.shape
    rep = heads // kv_heads
    n_blk = s // blk
    nq = rep * blk
    assert n_blk <= hd and n_blk % 16 == 0
    k_off = heads
    v_off = heads + kv_heads
    return pl.pallas_call(
        functools.partial(_moba_body, blk=blk, n_blk=n_blk, rep=rep, hd=hd, topk=topk),
        grid=(bsz, kv_heads, n_blk),
        in_specs=[pl.BlockSpec((1, blk, rep * hd), lambda b, g, i: (b, i, g)),
                  pl.BlockSpec((1, s, hd), lambda b, g, i: (b, 0, k_off + g)),
                  pl.BlockSpec((1, s, hd), lambda b, g, i: (b, 0, v_off + g))],
        out_specs=pl.BlockSpec((1, blk, rep * hd), lambda b, g, i: (b, i, g)),
        out_shape=jax.ShapeDtypeStruct((bsz, s, heads * hd), BF16),
        scratch_shapes=[pltpu.VMEM((n_blk, hd), F32),
                        pltpu.VMEM((n_blk, blk, 2 * hd), BF16),
                        pltpu.VMEM((n_blk, hd + SUM_ROWS, blk), BF16),
                        pltpu.VMEM((2 * hd, nq), BF16),
                        pltpu.VMEM((1, nq), F32),
                        pltpu.VMEM((hd + SUM_ROWS, nq), F32)],
        compiler_params=_cparams(("parallel", "parallel", "arbitrary")),
        name="moba_attention",
    )(qkv, qkv, qkv)


def _router_body(x_ref, w_ref, o_ref, *, n_experts):
    logits = jnp.dot(x_ref[...], w_ref[...], preferred_element_type=F32)
    lane = lax.broadcasted_iota(jnp.int32, logits.shape, 1)
    lane_f = lane.astype(F32)
    big = float(logits.shape[1])
    lg = jnp.where(lane < n_experts, logits, -jnp.inf)
    m1 = jnp.max(lg, axis=-1, keepdims=True)
    i1 = jnp.min(jnp.where(lg == m1, lane_f, big), axis=-1, keepdims=True)
    lg2 = jnp.where(lane_f == i1, -jnp.inf, lg)
    m2 = jnp.max(lg2, axis=-1, keepdims=True)
    i2 = jnp.min(jnp.where(lg2 == m2, lane_f, big), axis=-1, keepdims=True)
    t = jnp.exp(m2 - m1)
    g1 = 1.0 / (1.0 + t)
    g2 = t / (1.0 + t)
    o_ref[...] = jnp.where(lane == 0, i1,
                           jnp.where(lane == 1, i2, jnp.where(lane == 2, g1, jnp.where(lane == 3, g2, 0.0))))


def router_top2(x, w_pad, *, tm, n_experts):
    m, k = x.shape
    n = w_pad.shape[1]
    return pl.pallas_call(
        functools.partial(_router_body, n_experts=n_experts),
        grid=(m // tm,),
        in_specs=[pl.BlockSpec((tm, k), lambda i: (i, 0)),
                  pl.BlockSpec((k, n), lambda i: (0, 0))],
        out_specs=pl.BlockSpec((tm, n), lambda i: (i, 0)),
        out_shape=jax.ShapeDtypeStruct((m, n), F32),
        compiler_params=_cparams(("parallel",)),
        name="router_top2",
    )(x, w_pad)


def _row_copy(src_hbm, dst_vmem, sem, src_row, dst_row):
    return pltpu.make_async_copy(src_hbm.at[pl.ds(src_row, 1), :], dst_vmem.at[pl.ds(dst_row, 1), :], sem)


def _pack_bf16_pairs(x):
    half = x.shape[1] // 2
    lo = pltpu.bitcast(x[:, :half].astype(BF16).astype(F32), jnp.uint32)
    hi = pltpu.bitcast(x[:, half:].astype(BF16).astype(F32), jnp.uint32)
    return (hi & jnp.uint32(0xFFFF0000)) | (lo >> 16)


def _unpack_bf16_pairs(w):
    lo = pltpu.bitcast(w << 16, F32).astype(BF16)
    hi = pltpu.bitcast(w & jnp.uint32(0xFFFF0000), F32).astype(BF16)
    return jnp.concatenate([lo, hi], axis=1)


def _gather_body(src_ref, nrows_ref, hp_hbm, o_ref, buf, sem, *, tm, unroll):
    i = pl.program_id(0)

    def issue_tile(t, slot):
        def issue(g, c):
            for u in range(unroll):
                r = g * unroll + u
                _row_copy(hp_hbm, buf.at[slot], sem.at[slot], src_ref[t * tm + r], r).start(priority=u % 2)
            return c

        lax.fori_loop(0, tm // unroll, issue, 0)

    @pl.when(i == 0)
    def _():
        issue_tile(0, 0)

    nxt = i + 1

    @pl.when(jnp.logical_and(nxt < pl.num_programs(0), nxt * tm < nrows_ref[0]))
    def _():
        issue_tile(nxt, nxt % 2)

    used = i * tm < nrows_ref[0]

    @pl.when(jnp.logical_not(used))
    def _():
        o_ref[...] = jnp.zeros_like(o_ref)

    @pl.when(used)
    def _():
        slot = i % 2
        pltpu.make_async_copy(hp_hbm.at[pl.ds(0, tm), :], buf.at[slot], sem.at[slot]).wait()
        o_ref[...] = _unpack_bf16_pairs(buf[slot])


def gather_rows(hp, src_tok, n_used_rows, *, cap, tm):
    half = hp.shape[1]
    return pl.pallas_call(
        functools.partial(_gather_body, tm=tm, unroll=8),
        grid_spec=pltpu.PrefetchScalarGridSpec(
            num_scalar_prefetch=2,
            grid=(cap // tm,),
            in_specs=[pl.BlockSpec(memory_space=pl.ANY)],
            out_specs=pl.BlockSpec((tm, 2 * half), lambda i, src, nused: (i, 0)),
            scratch_shapes=[pltpu.VMEM((2, tm, half), jnp.uint32), pltpu.SemaphoreType.DMA((2,))]),
        out_shape=jax.ShapeDtypeStruct((cap, 2 * half), BF16),
        compiler_params=_cparams(("arbitrary",)),
        name="gather_rows",
    )(src_tok, n_used_rows, hp)


def _grouped_tile(i, n_rows, tm):
    return jnp.minimum(i, (n_rows[0] - 1) // tm)


def _grouped_fresh(i, sub_e, ratio):
    prev = sub_e[(jnp.maximum(i, 1) - 1) * ratio]
    return jnp.logical_or(i == 0, sub_e[i * ratio] != prev)


def _moe_up_body(sub_e, sub_n, n_rows, x_ref, wg_ref, wu_ref, o_ref, wgb_ref, wub_ref, *, tm, sub):
    i = pl.program_id(1)
    ratio = tm // sub
    used = i * tm < n_rows[0]

    @pl.when(jnp.logical_and(used, _grouped_fresh(i, sub_e, ratio)))
    def _():
        wgb_ref[...] = wg_ref[0].astype(BF16)
        wub_ref[...] = wu_ref[0].astype(BF16)

    def chunk(s):
        x = x_ref[s * sub:(s + 1) * sub, :]
        g = jnp.dot(x, wgb_ref[...], preferred_element_type=F32)
        u = jnp.dot(x, wub_ref[...], preferred_element_type=F32)
        o_ref[s * sub:(s + 1) * sub, :] = (_silu(g) * u).astype(o_ref.dtype)

    full = jnp.logical_and(used, sub_n[i * ratio + ratio - 1] > 0)

    @pl.when(full)
    def _():
        for s in range(ratio):
            chunk(s)

    for s in range(ratio):
        live = jnp.logical_and(used, sub_n[i * ratio + s] > 0)

        @pl.when(jnp.logical_and(live, jnp.logical_not(full)))
        def _():
            chunk(s)

        @pl.when(jnp.logical_not(live))
        def _():
            o_ref[s * sub:(s + 1) * sub, :] = jnp.zeros((sub, o_ref.shape[1]), o_ref.dtype)


def moe_swiglu_up(xs, w_gu, sub_e, sub_n, n_rows, *, tm, tf, sub):
    cap, d = xs.shape
    f = w_gu.shape[2] // 2
    nf = f // tf
    ratio = tm // sub

    def x_map(j, i, se, sn, nr):
        return (_grouped_tile(i, nr, tm), 0)

    def wg_map(j, i, se, sn, nr):
        return (se[_grouped_tile(i, nr, tm) * ratio], 0, j)

    def wu_map(j, i, se, sn, nr):
        return (se[_grouped_tile(i, nr, tm) * ratio], 0, nf + j)

    def o_map(j, i, se, sn, nr):
        return (i, j)

    return pl.pallas_call(
        functools.partial(_moe_up_body, tm=tm, sub=sub),
        grid_spec=pltpu.PrefetchScalarGridSpec(
            num_scalar_prefetch=3,
            grid=(nf, cap // tm),
            in_specs=[pl.BlockSpec((tm, d), x_map),
                      pl.BlockSpec((1, d, tf), wg_map),
                      pl.BlockSpec((1, d, tf), wu_map)],
            out_specs=pl.BlockSpec((tm, tf), o_map),
            scratch_shapes=[pltpu.VMEM((d, tf), BF16), pltpu.VMEM((d, tf), BF16)]),
        out_shape=jax.ShapeDtypeStruct((cap, f), BF16),
        compiler_params=_cparams(("arbitrary", "arbitrary")),
        name="moe_swiglu_up",
    )(sub_e, sub_n, n_rows, xs, w_gu, w_gu)


def _moe_down_body(sub_e, sub_n, n_rows, x_ref, w_ref, o_ref, wb_ref, *, tm, sub):
    i = pl.program_id(1)
    ratio = tm // sub
    used = i * tm < n_rows[0]

    @pl.when(jnp.logical_and(used, _grouped_fresh(i, sub_e, ratio)))
    def _():
        wb_ref[...] = w_ref[0].astype(BF16)

    def chunk(s):
        o_ref[s * sub:(s + 1) * sub, :] = jnp.dot(
            x_ref[s * sub:(s + 1) * sub, :], wb_ref[...], preferred_element_type=F32)

    full = jnp.logical_and(used, sub_n[i * ratio + ratio - 1] > 0)

    @pl.when(full)
    def _():
        for s in range(ratio):
            chunk(s)

    for s in range(ratio):
        live = jnp.logical_and(used, sub_n[i * ratio + s] > 0)

        @pl.when(jnp.logical_and(live, jnp.logical_not(full)))
        def _():
            chunk(s)

        @pl.when(jnp.logical_not(live))
        def _():
            o_ref[s * sub:(s + 1) * sub, :] = jnp.zeros((sub, o_ref.shape[1]), o_ref.dtype)


def moe_down(act, w_down, sub_e, sub_n, n_rows, *, tm, tn, sub):
    cap, f = act.shape
    d = w_down.shape[2]
    ratio = tm // sub

    def x_map(j, i, se, sn, nr):
        return (_grouped_tile(i, nr, tm), 0)

    def w_map(j, i, se, sn, nr):
        return (se[_grouped_tile(i, nr, tm) * ratio], 0, j)

    def o_map(j, i, se, sn, nr):
        return (i, j)

    return pl.pallas_call(
        functools.partial(_moe_down_body, tm=tm, sub=sub),
        grid_spec=pltpu.PrefetchScalarGridSpec(
            num_scalar_prefetch=3,
            grid=(d // tn, cap // tm),
            in_specs=[pl.BlockSpec((tm, f), x_map),
                      pl.BlockSpec((1, f, tn), w_map)],
            out_specs=pl.BlockSpec((tm, tn), o_map),
            scratch_shapes=[pltpu.VMEM((f, tn), BF16)]),
        out_shape=jax.ShapeDtypeStruct((cap, d), F32),
        compiler_params=_cparams(("arbitrary", "arbitrary")),
        name="moe_down",
    )(sub_e, sub_n, n_rows, act, w_down)


def _combine_body(dest_ref, y_hbm, gates_ref, res_ref, g_ref, b_ref, o_ref, buf, sem, *, tm, unroll):
    i = pl.program_id(0)

    def issue_tile(t, slot):
        def issue(g, c):
            for u in range(unroll):
                r = g * unroll + u
                for k in range(TOP_K):
                    _row_copy(y_hbm, buf.at[slot, k], sem.at[slot, k],
                              dest_ref[TOP_K * (t * tm + r) + k], r).start(priority=k % 2)
            return c

        lax.fori_loop(0, tm // unroll, issue, 0)

    @pl.when(i == 0)
    def _():
        issue_tile(0, 0)

    nxt = i + 1

    @pl.when(nxt < pl.num_programs(0))
    def _():
        issue_tile(nxt, nxt % 2)

    slot = i % 2
    gates = gates_ref[...]
    y = None
    for k in range(TOP_K):
        pltpu.make_async_copy(y_hbm.at[pl.ds(0, tm), :], buf.at[slot, k], sem.at[slot, k]).wait()
        term = buf[slot, k] * gates[:, TOP_K + k:TOP_K + k + 1]
        y = term if y is None else y + term
    o_ref[...] = _layer_norm_rows(DEEPNORM_ALPHA * res_ref[...] + y, g_ref[...], b_ref[...])


def moe_combine_post_norm(y_rows, dest, routing, res, g, b, *, tm):
    n_tok, d = res.shape
    lanes = routing.shape[1]
    return pl.pallas_call(
        functools.partial(_combine_body, tm=tm, unroll=4),
        grid_spec=pltpu.PrefetchScalarGridSpec(
            num_scalar_prefetch=1,
            grid=(n_tok // tm,),
            in_specs=[pl.BlockSpec(memory_space=pl.ANY),
                      pl.BlockSpec((tm, lanes), lambda i, dst: (i, 0)),
                      pl.BlockSpec((tm, d), lambda i, dst: (i, 0)),
                      pl.BlockSpec((1, d), lambda i, dst: (0, 0)),
                      pl.BlockSpec((1, d), lambda i, dst: (0, 0))],
            out_specs=pl.BlockSpec((tm, d), lambda i, dst: (i, 0)),
            scratch_shapes=[pltpu.VMEM((2, TOP_K, tm, d), F32), pltpu.SemaphoreType.DMA((2, TOP_K))]),
        out_shape=jax.ShapeDtypeStruct((n_tok, d), F32),
        compiler_params=_cparams(("arbitrary",)),
        name="moe_combine_post_norm",
    )(dest, y_rows, routing, res, g.reshape(1, d), b.reshape(1, d))


def _group_rows(seg, n_seg, tile, sub):
    n_asg = seg.shape[0]
    cap = n_asg + n_seg * tile
    onehot = (seg[:, None] == jnp.arange(n_seg, dtype=jnp.int32)[None, :]).astype(jnp.int32)
    csum = jnp.cumsum(onehot, axis=0)
    rank = jnp.sum(csum * onehot, axis=1) - 1
    counts = csum[-1]
    padded = (counts + tile - 1) // tile * tile
    pad_end = jnp.cumsum(padded)
    pad_start = pad_end - padded
    dest = (pad_start[seg] + rank).astype(jnp.int32)
    src = jnp.zeros((cap,), jnp.int32).at[dest].set(
        jnp.arange(n_asg, dtype=jnp.int32) // TOP_K, unique_indices=True, mode="promise_in_bounds")
    sub_start = jnp.arange(cap // sub, dtype=jnp.int32) * sub
    sub_e = jnp.minimum(jnp.searchsorted(pad_end, sub_start, side="right"), n_seg - 1).astype(jnp.int32)
    sub_n = jnp.clip(counts[sub_e] - (sub_start - pad_start[sub_e]), 0, sub).astype(jnp.int32)
    n_rows = pad_end[-1:].astype(jnp.int32)
    return dest, src, sub_e, sub_n, n_rows, cap


def _gla_layer(h, hb, w_in, w_gate_up, b_gate, g_norm, w_o, ln_g, ln_b):
    bsz, s, d = h.shape
    dk = (d // 2) // GLA_HEADS
    dv = d // GLA_HEADS
    n_main = 2 * GLA_HEADS * dk + 2 * GLA_HEADS * dv
    x2 = hb.reshape(bsz * s, d)
    n_qk = 2 * GLA_HEADS * dk
    tm = _pick(bsz * s, (1024, 512, 256, 128, 64))
    qk = matmul(x2, w_in[:, :n_qk].astype(BF16), tm=tm, tn=_pick(n_qk, (1024, 512, 256, 128)), out_dtype=F32)
    vr = matmul(x2, w_in[:, n_qk:n_main].astype(BF16), tm=tm, tn=_pick(n_main - n_qk, (1024, 512, 256, 128)),
                out_dtype=BF16)
    w_z = jnp.pad(w_in[:, n_main:], ((0, 0), (0, LANES - GLA_GATE_RANK))).astype(BF16)
    z = matmul(x2, w_z, tm=_pick(bsz * s, (1024, 512, 256, 128, 64)), tn=LANES, out_dtype=F32)
    wgu = jnp.pad(w_gate_up, ((0, LANES - GLA_GATE_RANK), (0, 0))).astype(BF16)
    wgu = wgu.reshape(LANES, GLA_HEADS, dk).transpose(1, 0, 2)
    og = gla_core(qk.reshape(bsz, s, n_qk), vr.reshape(bsz, s, n_main - n_qk), z.reshape(bsz, s, LANES), wgu,
                  b_gate.reshape(GLA_HEADS, 1, dk), g_norm.reshape(1, dv),
                  heads=GLA_HEADS, dk=dk, dv=dv, step=min(GLA_STEP_TOKENS, s))
    out, outb = matmul_post_norm(og.reshape(bsz * s, d), w_o.astype(BF16), h.reshape(bsz * s, d), ln_g, ln_b,
                                 tm=512, sub=256)
    return out.reshape(bsz, s, d), outb.reshape(bsz, s, d)


def _moba_layer(h, hb, w_q, w_kv, w_o, ln_g, ln_b):
    bsz, s, d = h.shape
    hd = d // MOBA_HEADS
    w_all = jnp.concatenate([w_q, w_kv], axis=1).astype(BF16)
    n_all = w_all.shape[1]
    tn = _pick(math.gcd(d, n_all), (1024, 512, 256, 128))
    qkv = matmul(hb.reshape(bsz * s, d), w_all, tm=_pick(bsz * s, (1024, 512, 256, 128, 64)), tn=tn,
                 out_dtype=BF16, n_scaled=d // tn, scale=hd ** -0.5 * LOG2_E)
    att = moba_attention(qkv.reshape(bsz, s, n_all), heads=MOBA_HEADS, kv_heads=MOBA_KV_HEADS, hd=hd,
                         blk=MOBA_BLOCK, topk=MOBA_TOPK)
    out, outb, outp = matmul_post_norm(att.reshape(bsz * s, d), w_o.astype(BF16), h.reshape(bsz * s, d),
                                       ln_g, ln_b, tm=512, sub=256, packed=True)
    return out.reshape(bsz, s, d), outb.reshape(bsz, s, d), outp


def _dense_ffn(h, hb, w_gu, w_down, ln_g, ln_b):
    bsz, s, d = h.shape
    f = w_down.shape[0]
    f_pad = -(-f // 512) * 512
    w_gu_b = w_gu.astype(BF16)
    wg = jnp.pad(w_gu_b[:, :f], ((0, 0), (0, f_pad - f)))
    wu = jnp.pad(w_gu_b[:, f:], ((0, 0), (0, f_pad - f)))
    wd = jnp.pad(w_down.astype(BF16), ((0, f_pad - f), (0, 0)))
    act = swiglu_up(hb.reshape(bsz * s, d), wg, wu, tm=_pick(bsz * s, (1024, 512, 256, 128, 64)), tf=512)
    out, outb = matmul_post_norm(act, wd, h.reshape(bsz * s, d), ln_g, ln_b, tm=256, sub=128)
    return out.reshape(bsz, s, d), outb.reshape(bsz, s, d)


def _moe_ffn(h, hb, hp, w_router, w_gu, w_down, ln_g, ln_b):
    bsz, s, d = h.shape
    n_tok = bsz * s
    h2 = h.reshape(n_tok, d)
    w_r = jnp.pad(w_router, ((0, 0), (0, LANES - N_EXPERTS))).astype(BF16)
    routing = router_top2(hb.reshape(n_tok, d), w_r, tm=_pick(n_tok, (1024, 512, 256, 128, 64)),
                          n_experts=N_EXPERTS)
    seg = routing[:, :TOP_K].astype(jnp.int32).reshape(-1)
    tile = min(MOE_TILE, n_tok)
    sub = min(MOE_SUB, tile)
    dest, src, sub_e, sub_n, n_rows, cap = _group_rows(seg, N_EXPERTS, tile, sub)
    xs = gather_rows(hp, src, n_rows, cap=cap, tm=min(GATHER_TILE, tile))
    f = w_down.shape[1]
    act = moe_swiglu_up(xs, w_gu, sub_e, sub_n, n_rows, tm=tile, tf=_pick(f, (512, 256, 128)), sub=sub)
    y_rows = moe_down(act, w_down, sub_e, sub_n, n_rows, tm=min(MOE_DOWN_TILE, tile),
                      tn=_pick(d, (512, 256, 128)), sub=sub)
    out = moe_combine_post_norm(y_rows, dest, routing, h2, ln_g, ln_b, tm=_pick(n_tok, (256, 128, 64)))
    return out.reshape(bsz, s, d)


def kernel(x, w_in_a, w_gate_up_a, b_gate_a, g_norm_a, w_o_a, w_kv_shared, w_q_b, w_o_b,
           ln_mix_g, ln_mix_b, w_gu_dense, w_down_dense, w_router, w_gu_moe, w_down_moe,
           ln_ffn_g, ln_ffn_b):
    h, hb = _gla_layer(x, x.astype(BF16), w_in_a[0], w_gate_up_a[0], b_gate_a[0], g_norm_a[0], w_o_a[0],
                       ln_mix_g[0], ln_mix_b[0])
    h, hb = _dense_ffn(h, hb, w_gu_dense[0], w_down_dense[0], ln_ffn_g[0], ln_ffn_b[0])
    h, hb, hp = _moba_layer(h, hb, w_q_b[0], w_kv_shared, w_o_b[0], ln_mix_g[1], ln_mix_b[1])
    return _moe_ffn(h, hb, hp, w_router[0], w_gu_moe[0], w_down_moe[0], ln_ffn_g[1], ln_ffn_b[1])
```

```python
import functools
import math

import jax
import jax.numpy as jnp
from jax import lax
from jax.experimental import pallas as pl
from jax.experimental.pallas import tpu as pltpu

F32 = jnp.float32
BF16 = jnp.bfloat16

DEPTH = 2
GLA_HEADS = 4
GLA_GATE_RANK = 16
GLA_TAU = 16.0
GLA_CHUNK = 64
MOBA_HEADS = 16
MOBA_KV_HEADS = 4
MOBA_BLOCK = 256
MOBA_TOPK = 3
N_EXPERTS = 8
TOP_K = 2
LN_EPS = 1e-5
RMS_EPS = 1e-6
DEEPNORM_ALPHA = (2 * DEPTH) ** 0.25

LANES = 128
VMEM_LIMIT_BYTES = 56 * 1024 * 1024

GLA_GROUP = 4
GLA_STEP_TOKENS = 512
MOE_TILE = 1024
MOE_SUB = 256
MOE_DOWN_TILE = 512
GATHER_TILE = 512


def _cparams(sem):
    return pltpu.CompilerParams(dimension_semantics=sem, vmem_limit_bytes=VMEM_LIMIT_BYTES)


def _pick(n, prefs):
    for p in prefs:
        if n % p == 0:
            return p
    return n


def _mm_body(x_ref, w_ref, o_ref, *, n_scaled, scale):
    acc = jnp.dot(x_ref[...], w_ref[...], preferred_element_type=F32)
    if n_scaled:
        acc = acc * jnp.where(pl.program_id(0) < n_scaled, scale, 1.0).astype(F32)
    o_ref[...] = acc.astype(o_ref.dtype)


def matmul(x, w, *, tm, tn, out_dtype, n_scaled=0, scale=1.0):
    m, k = x.shape
    n = w.shape[1]
    return pl.pallas_call(
        functools.partial(_mm_body, n_scaled=n_scaled, scale=scale),
        grid=(n // tn, m // tm),
        in_specs=[pl.BlockSpec((tm, k), lambda j, i: (i, 0)),
                  pl.BlockSpec((k, tn), lambda j, i: (0, j))],
        out_specs=pl.BlockSpec((tm, tn), lambda j, i: (i, j)),
        out_shape=jax.ShapeDtypeStruct((m, n), out_dtype),
        compiler_params=_cparams(("parallel", "parallel")),
        name="matmul",
    )(x, w)


def _layer_norm_rows(t, g, b):
    mu = jnp.mean(t, axis=-1, keepdims=True)
    d = t - mu
    var = jnp.mean(d * d, axis=-1, keepdims=True)
    return d * lax.rsqrt(var + LN_EPS) * g + b


def _mm_ln_body(x_ref, w_ref, res_ref, g_ref, b_ref, o_ref, ob_ref, *rest, tm, sub, packed):
    for s in range(tm // sub):
        rows = slice(s * sub, (s + 1) * sub)
        y = jnp.dot(x_ref[rows, :], w_ref[...], preferred_element_type=F32)
        out = _layer_norm_rows(DEEPNORM_ALPHA * res_ref[rows, :] + y, g_ref[...], b_ref[...])
        o_ref[rows, :] = out
        ob_ref[rows, :] = out.astype(BF16)
        if packed:
            _store_token_major(rest[0], rows, _pack_bf16_pairs(out))


def matmul_post_norm(x, w, res, g, b, *, tm, sub, packed=False):
    m, k = x.shape
    d = w.shape[1]
    out_specs = [pl.BlockSpec((tm, d), lambda i: (i, 0)), pl.BlockSpec((tm, d), lambda i: (i, 0))]
    out_shape = [jax.ShapeDtypeStruct((m, d), F32), jax.ShapeDtypeStruct((m, d), BF16)]
    if packed:
        out_specs.append(pl.BlockSpec((tm, d // 2 // LANES, LANES), lambda i: (i, 0, 0)))
        out_shape.append(jax.ShapeDtypeStruct((m, d // 2 // LANES, LANES), jnp.uint32))
    return pl.pallas_call(
        functools.partial(_mm_ln_body, tm=tm, sub=sub, packed=packed),
        grid=(m // tm,),
        in_specs=[pl.BlockSpec((tm, k), lambda i: (i, 0)),
                  pl.BlockSpec((k, d), lambda i: (0, 0), pipeline_mode=pl.Buffered(1)),
                  pl.BlockSpec((tm, d), lambda i: (i, 0)),
                  pl.BlockSpec((1, d), lambda i: (0, 0)),
                  pl.BlockSpec((1, d), lambda i: (0, 0))],
        out_specs=out_specs,
        out_shape=out_shape,
        compiler_params=_cparams(("parallel",)),
        name="matmul_post_norm",
    )(x, w, res, g.reshape(1, d), b.reshape(1, d))


def _silu(x):
    return x / (1.0 + jnp.exp(-x))


def _swiglu_body(x_ref, wg_ref, wu_ref, o_ref):
    x = x_ref[...]
    g = jnp.dot(x, wg_ref[...], preferred_element_type=F32)
    u = jnp.dot(x, wu_ref[...], preferred_element_type=F32)
    o_ref[...] = (_silu(g) * u).astype(o_ref.dtype)


def swiglu_up(x, wg, wu, *, tm, tf):
    m, k = x.shape
    f = wg.shape[1]
    return pl.pallas_call(
        _swiglu_body,
        grid=(f // tf, m // tm),
        in_specs=[pl.BlockSpec((tm, k), lambda j, i: (i, 0)),
                  pl.BlockSpec((k, tf), lambda j, i: (0, j)),
                  pl.BlockSpec((k, tf), lambda j, i: (0, j))],
        out_specs=pl.BlockSpec((tm, tf), lambda j, i: (i, j)),
        out_shape=jax.ShapeDtypeStruct((m, f), BF16),
        compiler_params=_cparams(("parallel", "parallel")),
        name="swiglu_up",
    )(x, wg, wu)


def _gla_body(q_ref, k_ref, v_ref, r_ref, z_ref, wgu_ref, bg_ref, gn_ref, o_ref, state_ref,
              *, chunk, group, n_groups, dk):
    @pl.when(pl.program_id(2) == 0)
    def _():
        state_ref[...] = jnp.zeros_like(state_ref)

    gt = group * chunk
    row = lax.broadcasted_iota(jnp.int32, (gt, gt), 0)
    col = lax.broadcasted_iota(jnp.int32, (gt, gt), 1)
    causal = jnp.logical_and(row // chunk == col // chunk, col <= row)
    tril = jnp.where(causal, 1.0, 0.0).astype(BF16)
    nt = (((1,), (1,)), ((), ()))
    tn = (((0,), (0,)), ((), ()))

    def body(g, carry):
        sl = pl.ds(pl.multiple_of(g * gt, gt), gt)
        gate = jnp.dot(z_ref[0, sl, :].astype(BF16), wgu_ref[0], preferred_element_type=F32) + bg_ref[0]
        log_a = (jnp.minimum(gate, 0.0) - jnp.log(1.0 + jnp.exp(-jnp.abs(gate)))) / GLA_TAU
        la_hi = log_a.astype(BF16)
        la_lo = (log_a - la_hi.astype(F32)).astype(BF16)
        b_cum = (jnp.dot(tril, la_hi, preferred_element_type=F32)
                 + jnp.dot(tril, la_lo, preferred_element_type=F32))
        b_last = [b_cum[(c + 1) * chunk - 1:(c + 1) * chunk, :] for c in range(group)]
        b_last_rows = jnp.concatenate([jnp.broadcast_to(b, (chunk, b.shape[1])) for b in b_last], axis=0)
        q = q_ref[0, sl, :] * (dk ** -0.5)
        k = k_ref[0, sl, :]
        q_dec = (q * jnp.exp(b_cum)).astype(BF16)
        k_inv = (k * jnp.exp(-b_cum)).astype(BF16)
        k_end = (k * jnp.exp(b_last_rows - b_cum)).astype(BF16)
        v = v_ref[0, sl, :].astype(BF16)
        att = lax.dot_general(q_dec, k_inv, nt, preferred_element_type=F32)
        att = jnp.where(causal, att, 0.0).astype(BF16)
        o_intra = jnp.dot(att, v, preferred_element_type=F32)
        state_t = state_ref[...]
        outs = []
        for c in range(group):
            rows = slice(c * chunk, (c + 1) * chunk)
            outs.append(o_intra[rows] + lax.dot_general(q_dec[rows], state_t.astype(BF16), nt,
                                                        preferred_element_type=F32))
            state_t = (jnp.exp(b_last[c]) * state_t
                       + lax.dot_general(v[rows], k_end[rows], tn, preferred_element_type=F32))
        state_ref[...] = state_t
        o = jnp.concatenate(outs, axis=0)
        o = o * lax.rsqrt(jnp.mean(o * o, axis=-1, keepdims=True) + RMS_EPS) * gn_ref[...]
        o_ref[0, sl, :] = (o * _silu(r_ref[0, sl, :].astype(F32))).astype(o_ref.dtype)
        return carry

    lax.fori_loop(0, n_groups, body, 0)


def gla_core(qk, vr, z, wgu, bg, gn, *, heads, dk, dv, step):
    bsz, s, _ = qk.shape
    zr = z.shape[-1]
    k_off = heads
    r_off = heads
    return pl.pallas_call(
        functools.partial(_gla_body, chunk=GLA_CHUNK, group=GLA_GROUP,
                          n_groups=step // (GLA_GROUP * GLA_CHUNK), dk=dk),
        grid=(bsz, heads, s // step),
        in_specs=[pl.BlockSpec((1, step, dk), lambda b, h, t: (b, t, h)),
                  pl.BlockSpec((1, step, dk), lambda b, h, t: (b, t, k_off + h)),
                  pl.BlockSpec((1, step, dv), lambda b, h, t: (b, t, h)),
                  pl.BlockSpec((1, step, dv), lambda b, h, t: (b, t, r_off + h)),
                  pl.BlockSpec((1, step, zr), lambda b, h, t: (b, t, 0)),
                  pl.BlockSpec((1, zr, dk), lambda b, h, t: (h, 0, 0)),
                  pl.BlockSpec((1, 1, dk), lambda b, h, t: (h, 0, 0)),
                  pl.BlockSpec((1, dv), lambda b, h, t: (0, 0))],
        out_specs=pl.BlockSpec((1, step, dv), lambda b, h, t: (b, t, h)),
        out_shape=jax.ShapeDtypeStruct((bsz, s, heads * dv), BF16),
        scratch_shapes=[pltpu.VMEM((dv, dk), F32)],
        compiler_params=_cparams(("parallel", "parallel", "arbitrary")),
        name="gla_core",
    )(qk, qk, vr, vr, z, wgu, bg, gn)


MASK_PENALTY = -3.0e38
LOG2_E = math.log2(math.e)
SUM_ROWS = 16


def _moba_body(q_ref, k_ref, v_ref, o_ref, kmean_ref, kaug_ref, vt_ref, rhs_ref, m_ref, acc_ref,
               *, blk, n_blk, rep, hd, topk):
    i = pl.program_id(2)
    nq = rep * blk

    @pl.when(i == 0)
    def _():
        lane = lax.broadcasted_iota(jnp.int32, (blk, hd), 1)
        for j in range(n_blk):
            kj = k_ref[0, j * blk:(j + 1) * blk, :]
            kaug_ref[j, :, :hd] = kj
            kaug_ref[j, :, hd:] = jnp.where(lane == j, 1.0, 0.0).astype(BF16)
            kmean_ref[j:j + 1, :] = jnp.sum(kj.astype(F32), axis=0, keepdims=True) / blk
            vt_ref[j, :hd, :] = v_ref[0, j * blk:(j + 1) * blk, :].astype(F32).T.astype(BF16)
            vt_ref[j, hd:, :] = jnp.ones((SUM_ROWS, blk), BF16)
        rhs_ref[hd + n_blk:, :] = jnp.zeros((hd - n_blk, nq), BF16)

    qt = jnp.concatenate([q_ref[0, :, r * hd:(r + 1) * hd].astype(F32).T for r in range(rep)], axis=1)
    qt = qt.astype(BF16)
    gate = jnp.dot(kmean_ref[...].astype(BF16), qt, preferred_element_type=F32)
    blk_id = lax.broadcasted_iota(jnp.int32, (n_blk, nq), 0)
    past = blk_id < i
    gate = jnp.where(past, gate, -jnp.inf)
    beaten = jnp.zeros((n_blk, nq), jnp.int32)
    for c in range(n_blk):
        gc = gate[c:c + 1, :]
        beaten = beaten + jnp.where(gc > gate, 1, jnp.where(gc == gate, jnp.where(c < blk_id, 1, 0), 0))
    penalty = jnp.where(past, jnp.where(beaten < topk, 0.0, MASK_PENALTY),
                        jnp.where(blk_id == i, 0.0, MASK_PENALTY))
    rhs_ref[:hd, :] = qt
    rhs_ref[hd:hd + n_blk, :] = penalty.astype(BF16)

    m_ref[...] = jnp.full((1, nq), -jnp.inf, F32)
    acc_ref[...] = jnp.zeros_like(acc_ref)
    group = 4

    def attend(j0, n, own):
        keys = kaug_ref[pl.ds(j0, n)].reshape(n * blk, 2 * hd)
        s = jnp.dot(keys, rhs_ref[...], preferred_element_type=F32)
        if own:
            key = lax.broadcasted_iota(jnp.int32, (blk, nq), 0)
            qpos = lax.rem(lax.broadcasted_iota(jnp.int32, (blk, nq), 1), blk)
            tail = jnp.where(key <= qpos, s[(n - 1) * blk:], -jnp.inf)
            s = tail if n == 1 else jnp.concatenate([s[:(n - 1) * blk], tail], axis=0)
        m_old = m_ref[...]
        m_new = jnp.maximum(m_old, jnp.max(s, axis=0, keepdims=True))
        p = jnp.exp2(s - m_new).astype(BF16)
        upd = jnp.exp2(m_old - m_new) * acc_ref[...]
        for c in range(n):
            upd = upd + jnp.dot(vt_ref[j0 + c], p[c * blk:(c + 1) * blk], preferred_element_type=F32)
        acc_ref[...] = upd
        m_ref[...] = m_new

    n_last = i + 1 - (i // group) * group

    def full_group(jj, carry):
        attend(group * jj, group, False)
        return carry

    lax.fori_loop(0, i // group, full_group, 0)
    for n in range(1, group + 1):
        @pl.when(n_last == n)
        def _():
            attend(i + 1 - n, n, True)

    acc = acc_ref[...]
    out = acc[:hd] * (1.0 / acc[hd:hd + 1])
    for r in range(rep):
        o_ref[0, :, r * hd:(r + 1) * hd] = out[:, r * blk:(r + 1) * blk].T.astype(o_ref.dtype)


def moba_attention(qkv, *, heads, kv_heads, hd, blk, topk):
    bsz, s, _ = qkv.shape
    rep = heads // kv_heads
    n_blk = s // blk
    nq = rep * blk
    assert n_blk <= hd and n_blk % 16 == 0
    k_off = heads
    v_off = heads + kv_heads
    return pl.pallas_call(
        functools.partial(_moba_body, blk=blk, n_blk=n_blk, rep=rep, hd=hd, topk=topk),
        grid=(bsz, kv_heads, n_blk),
        in_specs=[pl.BlockSpec((1, blk, rep * hd), lambda b, g, i: (b, i, g)),
                  pl.BlockSpec((1, s, hd), lambda b, g, i: (b, 0, k_off + g)),
                  pl.BlockSpec((1, s, hd), lambda b, g, i: (b, 0, v_off + g))],
        out_specs=pl.BlockSpec((1, blk, rep * hd), lambda b, g, i: (b, i, g)),
        out_shape=jax.ShapeDtypeStruct((bsz, s, heads * hd), BF16),
        scratch_shapes=[pltpu.VMEM((n_blk, hd), F32),
                        pltpu.VMEM((n_blk, blk, 2 * hd), BF16),
                        pltpu.VMEM((n_blk, hd + SUM_ROWS, blk), BF16),
                        pltpu.VMEM((2 * hd, nq), BF16),
                        pltpu.VMEM((1, nq), F32),
                        pltpu.VMEM((hd + SUM_ROWS, nq), F32)],
        compiler_params=_cparams(("parallel", "parallel", "arbitrary")),
        name="moba_attention",
    )(qkv, qkv, qkv)


def _router_body(x_ref, w_ref, o_ref, *, n_experts):
    logits = jnp.dot(x_ref[...], w_ref[...], preferred_element_type=F32)
    lane = lax.broadcasted_iota(jnp.int32, logits.shape, 1)
    lane_f = lane.astype(F32)
    big = float(logits.shape[1])
    lg = jnp.where(lane < n_experts, logits, -jnp.inf)
    m1 = jnp.max(lg, axis=-1, keepdims=True)
    i1 = jnp.min(jnp.where(lg == m1, lane_f, big), axis=-1, keepdims=True)
    lg2 = jnp.where(lane_f == i1, -jnp.inf, lg)
    m2 = jnp.max(lg2, axis=-1, keepdims=True)
    i2 = jnp.min(jnp.where(lg2 == m2, lane_f, big), axis=-1, keepdims=True)
    t = jnp.exp(m2 - m1)
    g1 = 1.0 / (1.0 + t)
    g2 = t / (1.0 + t)
    o_ref[...] = jnp.where(lane == 0, i1,
                           jnp.where(lane == 1, i2, jnp.where(lane == 2, g1, jnp.where(lane == 3, g2, 0.0))))


def router_top2(x, w_pad, *, tm, n_experts):
    m, k = x.shape
    n = w_pad.shape[1]
    return pl.pallas_call(
        functools.partial(_router_body, n_experts=n_experts),
        grid=(m // tm,),
        in_specs=[pl.BlockSpec((tm, k), lambda i: (i, 0)),
                  pl.BlockSpec((k, n), lambda i: (0, 0))],
        out_specs=pl.BlockSpec((tm, n), lambda i: (i, 0)),
        out_shape=jax.ShapeDtypeStruct((m, n), F32),
        compiler_params=_cparams(("parallel",)),
        name="router_top2",
    )(x, w_pad)


def _row_copy(src_hbm, dst_vmem, sem, src_row, dst_row):
    return pltpu.make_async_copy(src_hbm.at[pl.ds(src_row, 1), :], dst_vmem.at[pl.ds(dst_row, 1), :], sem)


def _token_copy(src_hbm, dst_vmem, sem, src_row, dst_row):
    return pltpu.make_async_copy(src_hbm.at[src_row], dst_vmem.at[dst_row], sem)


def _store_token_major(ref, rows, x):
    for c in range(x.shape[1] // LANES):
        ref[rows, c, :] = x[:, c * LANES:(c + 1) * LANES]


def _load_token_major(x):
    return jnp.concatenate([x[:, c, :] for c in range(x.shape[1])], axis=1)


def _pack_bf16_pairs(x):
    half = x.shape[1] // 2
    lo = pltpu.bitcast(x[:, :half].astype(BF16).astype(F32), jnp.uint32)
    hi = pltpu.bitcast(x[:, half:].astype(BF16).astype(F32), jnp.uint32)
    return (hi & jnp.uint32(0xFFFF0000)) | (lo >> 16)


def _unpack_bf16_pairs(w):
    lo = pltpu.bitcast(w << 16, F32).astype(BF16)
    hi = pltpu.bitcast(w & jnp.uint32(0xFFFF0000), F32).astype(BF16)
    return jnp.concatenate([lo, hi], axis=1)


def _gather_body(tok_ref, base_ref, nrows_ref, hp_hbm, o_ref, buf, sem, *, tm, sub, unroll):
    i = pl.program_id(0)

    def issue_tile(t, slot):
        base = base_ref[t * (tm // sub)]

        def issue(g, c):
            for u in range(unroll):
                r = g * unroll + u
                _token_copy(hp_hbm, buf.at[slot], sem.at[slot], tok_ref[base + r], r).start()
            return c

        lax.fori_loop(0, tm // unroll, issue, 0)

    @pl.when(i == 0)
    def _():
        issue_tile(0, 0)

    nxt = i + 1

    @pl.when(jnp.logical_and(nxt < pl.num_programs(0), nxt * tm < nrows_ref[0]))
    def _():
        issue_tile(nxt, nxt % 2)

    used = i * tm < nrows_ref[0]

    @pl.when(jnp.logical_not(used))
    def _():
        o_ref[...] = jnp.zeros_like(o_ref)

    @pl.when(used)
    def _():
        slot = i % 2
        pltpu.make_async_copy(hp_hbm.at[pl.ds(0, tm)], buf.at[slot], sem.at[slot]).wait()
        o_ref[...] = _unpack_bf16_pairs(_load_token_major(buf[slot]))


def gather_rows(hp, sorted_tok, sub_base, n_used_rows, *, cap, tm, sub):
    half = hp.shape[1] * hp.shape[2]
    return pl.pallas_call(
        functools.partial(_gather_body, tm=tm, sub=sub, unroll=8),
        grid_spec=pltpu.PrefetchScalarGridSpec(
            num_scalar_prefetch=3,
            grid=(cap // tm,),
            in_specs=[pl.BlockSpec(memory_space=pl.ANY)],
            out_specs=pl.BlockSpec((tm, 2 * half), lambda i, tok, base, nused: (i, 0)),
            scratch_shapes=[pltpu.VMEM((2, tm) + hp.shape[1:], jnp.uint32), pltpu.SemaphoreType.DMA((2,))]),
        out_shape=jax.ShapeDtypeStruct((cap, 2 * half), BF16),
        compiler_params=_cparams(("arbitrary",)),
        name="gather_rows",
    )(sorted_tok, sub_base, n_used_rows, hp)


def _grouped_tile(i, n_rows, tm):
    return jnp.minimum(i, (n_rows[0] - 1) // tm)


def _grouped_fresh(i, sub_e, ratio):
    prev = sub_e[(jnp.maximum(i, 1) - 1) * ratio]
    return jnp.logical_or(i == 0, sub_e[i * ratio] != prev)


def _moe_up_body(sub_e, sub_n, n_rows, x_ref, wg_ref, wu_ref, o_ref, wgb_ref, wub_ref, *, tm, sub):
    i = pl.program_id(1)
    ratio = tm // sub
    used = i * tm < n_rows[0]

    @pl.when(jnp.logical_and(used, _grouped_fresh(i, sub_e, ratio)))
    def _():
        wgb_ref[...] = wg_ref[0].astype(BF16)
        wub_ref[...] = wu_ref[0].astype(BF16)

    def chunk(s):
        x = x_ref[s * sub:(s + 1) * sub, :]
        g = jnp.dot(x, wgb_ref[...], preferred_element_type=F32)
        u = jnp.dot(x, wub_ref[...], preferred_element_type=F32)
        o_ref[s * sub:(s + 1) * sub, :] = (_silu(g) * u).astype(o_ref.dtype)

    full = jnp.logical_and(used, sub_n[i * ratio + ratio - 1] > 0)

    @pl.when(full)
    def _():
        for s in range(ratio):
            chunk(s)

    for s in range(ratio):
        live = jnp.logical_and(used, sub_n[i * ratio + s] > 0)

        @pl.when(jnp.logical_and(live, jnp.logical_not(full)))
        def _():
            chunk(s)

        @pl.when(jnp.logical_not(live))
        def _():
            o_ref[s * sub:(s + 1) * sub, :] = jnp.zeros((sub, o_ref.shape[1]), o_ref.dtype)


def moe_swiglu_up(xs, w_gu, sub_e, sub_n, n_rows, *, tm, tf, sub):
    cap, d = xs.shape
    f = w_gu.shape[2] // 2
    nf = f // tf
    ratio = tm // sub

    def x_map(j, i, se, sn, nr):
        return (_grouped_tile(i, nr, tm), 0)

    def wg_map(j, i, se, sn, nr):
        return (se[_grouped_tile(i, nr, tm) * ratio], 0, j)

    def wu_map(j, i, se, sn, nr):
        return (se[_grouped_tile(i, nr, tm) * ratio], 0, nf + j)

    def o_map(j, i, se, sn, nr):
        return (i, j)

    return pl.pallas_call(
        functools.partial(_moe_up_body, tm=tm, sub=sub),
        grid_spec=pltpu.PrefetchScalarGridSpec(
            num_scalar_prefetch=3,
            grid=(nf, cap // tm),
            in_specs=[pl.BlockSpec((tm, d), x_map),
                      pl.BlockSpec((1, d, tf), wg_map),
                      pl.BlockSpec((1, d, tf), wu_map)],
            out_specs=pl.BlockSpec((tm, tf), o_map),
            scratch_shapes=[pltpu.VMEM((d, tf), BF16), pltpu.VMEM((d, tf), BF16)]),
        out_shape=jax.ShapeDtypeStruct((cap, f), BF16),
        compiler_params=_cparams(("arbitrary", "arbitrary")),
        name="moe_swiglu_up",
    )(sub_e, sub_n, n_rows, xs, w_gu, w_gu)


def _moe_down_body(sub_e, sub_n, n_rows, x_ref, w_ref, o_ref, wb_ref, *, tm, sub):
    i = pl.program_id(1)
    ratio = tm // sub
    used = i * tm < n_rows[0]

    @pl.when(jnp.logical_and(used, _grouped_fresh(i, sub_e, ratio)))
    def _():
        wb_ref[...] = w_ref[0].astype(BF16)

    def chunk(s):
        o_ref[s * sub:(s + 1) * sub, :] = jnp.dot(
            x_ref[s * sub:(s + 1) * sub, :], wb_ref[...], preferred_element_type=F32)

    full = jnp.logical_and(used, sub_n[i * ratio + ratio - 1] > 0)

    @pl.when(full)
    def _():
        for s in range(ratio):
            chunk(s)

    for s in range(ratio):
        live = jnp.logical_and(used, sub_n[i * ratio + s] > 0)

        @pl.when(jnp.logical_and(live, jnp.logical_not(full)))
        def _():
            chunk(s)

        @pl.when(jnp.logical_not(live))
        def _():
            o_ref[s * sub:(s + 1) * sub, :] = jnp.zeros((sub, o_ref.shape[1]), o_ref.dtype)


def moe_down(act, w_down, sub_e, sub_n, n_rows, *, tm, tn, sub):
    cap, f = act.shape
    d = w_down.shape[2]
    ratio = tm // sub

    def x_map(j, i, se, sn, nr):
        return (_grouped_tile(i, nr, tm), 0)

    def w_map(j, i, se, sn, nr):
        return (se[_grouped_tile(i, nr, tm) * ratio], 0, j)

    def o_map(j, i, se, sn, nr):
        return (i, j)

    return pl.pallas_call(
        functools.partial(_moe_down_body, tm=tm, sub=sub),
        grid_spec=pltpu.PrefetchScalarGridSpec(
            num_scalar_prefetch=3,
            grid=(d // tn, cap // tm),
            in_specs=[pl.BlockSpec((tm, f), x_map),
                      pl.BlockSpec((1, f, tn), w_map)],
            out_specs=pl.BlockSpec((tm, tn), o_map),
            scratch_shapes=[pltpu.VMEM((f, tn), BF16)]),
        out_shape=jax.ShapeDtypeStruct((cap, d), F32),
        compiler_params=_cparams(("arbitrary", "arbitrary")),
        name="moe_down",
    )(sub_e, sub_n, n_rows, act, w_down)


def _combine_body(dest_ref, y_hbm, gates_ref, res_ref, g_ref, b_ref, o_ref, buf, sem, *, tm, unroll):
    i = pl.program_id(0)

    def issue_tile(t, slot):
        def issue(g, c):
            for u in range(unroll):
                r = g * unroll + u
                for k in range(TOP_K):
                    _row_copy(y_hbm, buf.at[slot, k], sem.at[slot, k],
                              dest_ref[TOP_K * (t * tm + r) + k], r).start()
            return c

        lax.fori_loop(0, tm // unroll, issue, 0)

    @pl.when(i == 0)
    def _():
        issue_tile(0, 0)

    nxt = i + 1

    @pl.when(nxt < pl.num_programs(0))
    def _():
        issue_tile(nxt, nxt % 2)

    slot = i % 2
    gates = gates_ref[...]
    y = None
    for k in range(TOP_K):
        pltpu.make_async_copy(y_hbm.at[pl.ds(0, tm), :], buf.at[slot, k], sem.at[slot, k]).wait()
        term = buf[slot, k] * gates[:, TOP_K + k:TOP_K + k + 1]
        y = term if y is None else y + term
    o_ref[...] = _layer_norm_rows(DEEPNORM_ALPHA * res_ref[...] + y, g_ref[...], b_ref[...])


def moe_combine_post_norm(y_rows, dest, routing, res, g, b, *, tm):
    n_tok, d = res.shape
    lanes = routing.shape[1]
    return pl.pallas_call(
        functools.partial(_combine_body, tm=tm, unroll=4),
        grid_spec=pltpu.PrefetchScalarGridSpec(
            num_scalar_prefetch=1,
            grid=(n_tok // tm,),
            in_specs=[pl.BlockSpec(memory_space=pl.ANY),
                      pl.BlockSpec((tm, lanes), lambda i, dst: (i, 0)),
                      pl.BlockSpec((tm, d), lambda i, dst: (i, 0)),
                      pl.BlockSpec((1, d), lambda i, dst: (0, 0)),
                      pl.BlockSpec((1, d), lambda i, dst: (0, 0))],
            out_specs=pl.BlockSpec((tm, d), lambda i, dst: (i, 0)),
            scratch_shapes=[pltpu.VMEM((2, TOP_K, tm, d), F32), pltpu.SemaphoreType.DMA((2, TOP_K))]),
        out_shape=jax.ShapeDtypeStruct((n_tok, d), F32),
        compiler_params=_cparams(("arbitrary",)),
        name="moe_combine_post_norm",
    )(dest, y_rows, routing, res, g.reshape(1, d), b.reshape(1, d))


def _group_rows(seg, n_seg, tile, sub):
    n_asg = seg.shape[0]
    cap = n_asg + n_seg * tile
    onehot = (seg[:, None] == jnp.arange(n_seg, dtype=jnp.int32)[None, :]).astype(jnp.int32)
    csum = jnp.cumsum(onehot, axis=0)
    rank = jnp.sum(csum * onehot, axis=1) - 1
    counts = csum[-1]
    padded = (counts + tile - 1) // tile * tile
    pad_end = jnp.cumsum(padded)
    pad_start = pad_end - padded
    dest = (pad_start[seg] + rank).astype(jnp.int32)
    order = jnp.sort(seg * n_asg + jnp.arange(n_asg, dtype=jnp.int32))
    sorted_tok = jnp.pad((order % n_asg) // TOP_K, (0, tile)).astype(jnp.int32)
    sub_start = jnp.arange(cap // sub, dtype=jnp.int32) * sub
    sub_e = jnp.minimum(jnp.searchsorted(pad_end, sub_start, side="right"), n_seg - 1).astype(jnp.int32)
    sub_n = jnp.clip(counts[sub_e] - (sub_start - pad_start[sub_e]), 0, sub).astype(jnp.int32)
    start = jnp.cumsum(counts) - counts
    sub_base = (sub_start - (pad_start - start)[sub_e]).astype(jnp.int32)
    n_rows = pad_end[-1:].astype(jnp.int32)
    return dest, sorted_tok, sub_base, sub_e, sub_n, n_rows, cap


def _gla_layer(h, hb, w_in, w_gate_up, b_gate, g_norm, w_o, ln_g, ln_b):
    bsz, s, d = h.shape
    dk = (d // 2) // GLA_HEADS
    dv = d // GLA_HEADS
    n_main = 2 * GLA_HEADS * dk + 2 * GLA_HEADS * dv
    x2 = hb.reshape(bsz * s, d)
    n_qk = 2 * GLA_HEADS * dk
    tm = _pick(bsz * s, (1024, 512, 256, 128, 64))
    qk = matmul(x2, w_in[:, :n_qk].astype(BF16), tm=tm, tn=_pick(n_qk, (1024, 512, 256, 128)), out_dtype=F32)
    vr = matmul(x2, w_in[:, n_qk:n_main].astype(BF16), tm=tm, tn=_pick(n_main - n_qk, (1024, 512, 256, 128)),
                out_dtype=F32)
    w_z = jnp.pad(w_in[:, n_main:], ((0, 0), (0, LANES - GLA_GATE_RANK))).astype(BF16)
    z = matmul(x2, w_z, tm=_pick(bsz * s, (1024, 512, 256, 128, 64)), tn=LANES, out_dtype=F32)
    wgu = jnp.pad(w_gate_up, ((0, LANES - GLA_GATE_RANK), (0, 0))).astype(BF16)
    wgu = wgu.reshape(LANES, GLA_HEADS, dk).transpose(1, 0, 2)
    og = gla_core(qk.reshape(bsz, s, n_qk), vr.reshape(bsz, s, n_main - n_qk), z.reshape(bsz, s, LANES), wgu,
                  b_gate.reshape(GLA_HEADS, 1, dk), g_norm.reshape(1, dv),
                  heads=GLA_HEADS, dk=dk, dv=dv, step=min(GLA_STEP_TOKENS, s))
    out, outb = matmul_post_norm(og.reshape(bsz * s, d), w_o.astype(BF16), h.reshape(bsz * s, d), ln_g, ln_b,
                                 tm=512, sub=256)
    return out.reshape(bsz, s, d), outb.reshape(bsz, s, d)


def _moba_layer(h, hb, w_q, w_kv, w_o, ln_g, ln_b):
    bsz, s, d = h.shape
    hd = d // MOBA_HEADS
    w_all = jnp.concatenate([w_q, w_kv], axis=1).astype(BF16)
    n_all = w_all.shape[1]
    tn = _pick(math.gcd(d, n_all), (1024, 512, 256, 128))
    qkv = matmul(hb.reshape(bsz * s, d), w_all, tm=_pick(bsz * s, (1024, 512, 256, 128, 64)), tn=tn,
                 out_dtype=BF16, n_scaled=d // tn, scale=hd ** -0.5 * LOG2_E)
    att = moba_attention(qkv.reshape(bsz, s, n_all), heads=MOBA_HEADS, kv_heads=MOBA_KV_HEADS, hd=hd,
                         blk=MOBA_BLOCK, topk=MOBA_TOPK)
    out, outb, outp = matmul_post_norm(att.reshape(bsz * s, d), w_o.astype(BF16), h.reshape(bsz * s, d),
                                       ln_g, ln_b, tm=512, sub=256, packed=True)
    return out.reshape(bsz, s, d), outb.reshape(bsz, s, d), outp


def _dense_ffn(h, hb, w_gu, w_down, ln_g, ln_b):
    bsz, s, d = h.shape
    f = w_down.shape[0]
    f_pad = -(-f // 512) * 512
    w_gu_b = w_gu.astype(BF16)
    wg = jnp.pad(w_gu_b[:, :f], ((0, 0), (0, f_pad - f)))
    wu = jnp.pad(w_gu_b[:, f:], ((0, 0), (0, f_pad - f)))
    wd = jnp.pad(w_down.astype(BF16), ((0, f_pad - f), (0, 0)))
    act = swiglu_up(hb.reshape(bsz * s, d), wg, wu, tm=_pick(bsz * s, (1024, 512, 256, 128, 64)), tf=512)
    out, outb = matmul_post_norm(act, wd, h.reshape(bsz * s, d), ln_g, ln_b, tm=256, sub=128)
    return out.reshape(bsz, s, d), outb.reshape(bsz, s, d)


def _moe_ffn(h, hb, hp, w_router, w_gu, w_down, ln_g, ln_b):
    bsz, s, d = h.shape
    n_tok = bsz * s
    h2 = h.reshape(n_tok, d)
    w_r = jnp.pad(w_router, ((0, 0), (0, LANES - N_EXPERTS))).astype(BF16)
    routing = router_top2(hb.reshape(n_tok, d), w_r, tm=_pick(n_tok, (1024, 512, 256, 128, 64)),
                          n_experts=N_EXPERTS)
    seg = routing[:, :TOP_K].astype(jnp.int32).reshape(-1)
    tile = min(MOE_TILE, n_tok)
    sub = min(MOE_SUB, tile)
    dest, sorted_tok, sub_base, sub_e, sub_n, n_rows, cap = _group_rows(seg, N_EXPERTS, tile, sub)
    xs = gather_rows(hp, sorted_tok, sub_base, n_rows, cap=cap, tm=min(GATHER_TILE, tile), sub=sub)
    f = w_down.shape[1]
    act = moe_swiglu_up(xs, w_gu, sub_e, sub_n, n_rows, tm=tile, tf=_pick(f, (512, 256, 128)), sub=sub)
    y_rows = moe_down(act, w_down, sub_e, sub_n, n_rows, tm=min(MOE_DOWN_TILE, tile),
                      tn=_pick(d, (512, 256, 128)), sub=sub)
    out = moe_combine_post_norm(y_rows, dest, routing, h2, ln_g, ln_b, tm=_pick(n_tok, (256, 128, 64)))
    return out.reshape(bsz, s, d)


def kernel(x, w_in_a, w_gate_up_a, b_gate_a, g_norm_a, w_o_a, w_kv_shared, w_q_b, w_o_b,
           ln_mix_g, ln_mix_b, w_gu_dense, w_down_dense, w_router, w_gu_moe, w_down_moe,
           ln_ffn_g, ln_ffn_b):
    h, hb = _gla_layer(x, x.astype(BF16), w_in_a[0], w_gate_up_a[0], b_gate_a[0], g_norm_a[0], w_o_a[0],
                       ln_mix_g[0], ln_mix_b[0])
    h, hb = _dense_ffn(h, hb, w_gu_dense[0], w_down_dense[0], ln_ffn_g[0], ln_ffn_b[0])
    h, hb, hp = _moba_layer(h, hb, w_q_b[0], w_kv_shared, w_o_b[0], ln_mix_g[1], ln_mix_b[1])
    return _moe_ffn(h, hb, hp, w_router[0], w_gu_moe[0], w_down_moe[0], ln_ffn_g[1], ln_ffn_b[1])
```

```python
import functools
import math

import jax
import jax.numpy as jnp
from jax import lax
from jax.experimental import pallas as pl
from jax.experimental.pallas import tpu as pltpu

F32 = jnp.float32
BF16 = jnp.bfloat16

DEPTH = 2
GLA_HEADS = 4
GLA_GATE_RANK = 16
GLA_TAU = 16.0
GLA_CHUNK = 64
MOBA_HEADS = 16
MOBA_KV_HEADS = 4
MOBA_BLOCK = 256
MOBA_TOPK = 3
N_EXPERTS = 8
TOP_K = 2
LN_EPS = 1e-5
RMS_EPS = 1e-6
DEEPNORM_ALPHA = (2 * DEPTH) ** 0.25

LANES = 128
VMEM_LIMIT_BYTES = 56 * 1024 * 1024

GLA_GROUP = 4
GLA_STEP_TOKENS = 512
MOE_TILE = 1024
MOE_SUB = 256
MOE_DOWN_TILE = 512
GATHER_TILE = 512


def _cparams(sem):
    return pltpu.CompilerParams(dimension_semantics=sem, vmem_limit_bytes=VMEM_LIMIT_BYTES)


def _pick(n, prefs):
    for p in prefs:
        if n % p == 0:
            return p
    return n


def _mm_body(x_ref, w_ref, o_ref, *, n_scaled, scale):
    acc = jnp.dot(x_ref[...], w_ref[...], preferred_element_type=F32)
    if n_scaled:
        acc = acc * jnp.where(pl.program_id(0) < n_scaled, scale, 1.0).astype(F32)
    o_ref[...] = acc.astype(o_ref.dtype)


def matmul(x, w, *, tm, tn, out_dtype, n_scaled=0, scale=1.0):
    m, k = x.shape
    n = w.shape[1]
    return pl.pallas_call(
        functools.partial(_mm_body, n_scaled=n_scaled, scale=scale),
        grid=(n // tn, m // tm),
        in_specs=[pl.BlockSpec((tm, k), lambda j, i: (i, 0)),
                  pl.BlockSpec((k, tn), lambda j, i: (0, j))],
        out_specs=pl.BlockSpec((tm, tn), lambda j, i: (i, j)),
        out_shape=jax.ShapeDtypeStruct((m, n), out_dtype),
        compiler_params=_cparams(("parallel", "parallel")),
        name="matmul",
    )(x, w)


def _layer_norm_rows(t, g, b):
    mu = jnp.mean(t, axis=-1, keepdims=True)
    d = t - mu
    var = jnp.mean(d * d, axis=-1, keepdims=True)
    return d * lax.rsqrt(var + LN_EPS) * g + b


def _mm_ln_body(x_ref, w_ref, res_ref, g_ref, b_ref, o_ref, ob_ref, *rest, tm, sub, packed):
    for s in range(tm // sub):
        rows = slice(s * sub, (s + 1) * sub)
        y = jnp.dot(x_ref[rows, :], w_ref[...], preferred_element_type=F32)
        out = _layer_norm_rows(DEEPNORM_ALPHA * res_ref[rows, :] + y, g_ref[...], b_ref[...])
        o_ref[rows, :] = out
        ob_ref[rows, :] = out.astype(BF16)
        if packed:
            _store_token_major(rest[0], rows, _pack_bf16_pairs(out))


def matmul_post_norm(x, w, res, g, b, *, tm, sub, packed=False):
    m, k = x.shape
    d = w.shape[1]
    out_specs = [pl.BlockSpec((tm, d), lambda i: (i, 0)), pl.BlockSpec((tm, d), lambda i: (i, 0))]
    out_shape = [jax.ShapeDtypeStruct((m, d), F32), jax.ShapeDtypeStruct((m, d), BF16)]
    if packed:
        out_specs.append(pl.BlockSpec((tm, d // 2 // LANES, LANES), lambda i: (i, 0, 0)))
        out_shape.append(jax.ShapeDtypeStruct((m, d // 2 // LANES, LANES), jnp.uint32))
    return pl.pallas_call(
        functools.partial(_mm_ln_body, tm=tm, sub=sub, packed=packed),
        grid=(m // tm,),
        in_specs=[pl.BlockSpec((tm, k), lambda i: (i, 0)),
                  pl.BlockSpec((k, d), lambda i: (0, 0), pipeline_mode=pl.Buffered(1)),
                  pl.BlockSpec((tm, d), lambda i: (i, 0)),
                  pl.BlockSpec((1, d), lambda i: (0, 0)),
                  pl.BlockSpec((1, d), lambda i: (0, 0))],
        out_specs=out_specs,
        out_shape=out_shape,
        compiler_params=_cparams(("parallel",)),
        name="matmul_post_norm",
    )(x, w, res, g.reshape(1, d), b.reshape(1, d))


def _silu(x):
    return x / (1.0 + jnp.exp(-x))


def _swiglu_body(x_ref, wg_ref, wu_ref, o_ref):
    x = x_ref[...]
    g = jnp.dot(x, wg_ref[...], preferred_element_type=F32)
    u = jnp.dot(x, wu_ref[...], preferred_element_type=F32)
    o_ref[...] = (_silu(g) * u).astype(o_ref.dtype)


def swiglu_up(x, wg, wu, *, tm, tf):
    m, k = x.shape
    f = wg.shape[1]
    return pl.pallas_call(
        _swiglu_body,
        grid=(f // tf, m // tm),
        in_specs=[pl.BlockSpec((tm, k), lambda j, i: (i, 0)),
                  pl.BlockSpec((k, tf), lambda j, i: (0, j)),
                  pl.BlockSpec((k, tf), lambda j, i: (0, j))],
        out_specs=pl.BlockSpec((tm, tf), lambda j, i: (i, j)),
        out_shape=jax.ShapeDtypeStruct((m, f), BF16),
        compiler_params=_cparams(("parallel", "parallel")),
        name="swiglu_up",
    )(x, wg, wu)


def _gla_body(q_ref, k_ref, v_ref, r_ref, z_ref, wgu_ref, bg_ref, gn_ref, o_ref, state_ref,
              *, chunk, group, n_groups, dk):
    @pl.when(pl.program_id(2) == 0)
    def _():
        state_ref[...] = jnp.zeros_like(state_ref)

    gt = group * chunk
    row = lax.broadcasted_iota(jnp.int32, (gt, gt), 0)
    col = lax.broadcasted_iota(jnp.int32, (gt, gt), 1)
    causal = jnp.logical_and(row // chunk == col // chunk, col <= row)
    tril = jnp.where(causal, 1.0, 0.0).astype(BF16)
    nt = (((1,), (1,)), ((), ()))
    tn = (((0,), (0,)), ((), ()))

    def body(g, carry):
        sl = pl.ds(pl.multiple_of(g * gt, gt), gt)
        gate = jnp.dot(z_ref[0, sl, :].astype(BF16), wgu_ref[0], preferred_element_type=F32) + bg_ref[0]
        log_a = (jnp.minimum(gate, 0.0) - jnp.log(1.0 + jnp.exp(-jnp.abs(gate)))) / GLA_TAU
        la_hi = log_a.astype(BF16)
        la_lo = (log_a - la_hi.astype(F32)).astype(BF16)
        b_cum = (jnp.dot(tril, la_hi, preferred_element_type=F32)
                 + jnp.dot(tril, la_lo, preferred_element_type=F32))
        b_last = [b_cum[(c + 1) * chunk - 1:(c + 1) * chunk, :] for c in range(group)]
        b_last_rows = jnp.concatenate([jnp.broadcast_to(b, (chunk, b.shape[1])) for b in b_last], axis=0)
        q = q_ref[0, sl, :] * (dk ** -0.5)
        k = k_ref[0, sl, :]
        q_dec = (q * jnp.exp(b_cum)).astype(BF16)
        k_inv = (k * jnp.exp(-b_cum)).astype(BF16)
        k_end = (k * jnp.exp(b_last_rows - b_cum)).astype(BF16)
        v = v_ref[0, sl, :].astype(BF16)
        att = lax.dot_general(q_dec, k_inv, nt, preferred_element_type=F32)
        att = jnp.where(causal, att, 0.0).astype(BF16)
        o_intra = jnp.dot(att, v, preferred_element_type=F32)
        state_t = state_ref[...]
        outs = []
        for c in range(group):
            rows = slice(c * chunk, (c + 1) * chunk)
            outs.append(o_intra[rows] + lax.dot_general(q_dec[rows], state_t.astype(BF16), nt,
                                                        preferred_element_type=F32))
            state_t = (jnp.exp(b_last[c]) * state_t
                       + lax.dot_general(v[rows], k_end[rows], tn, preferred_element_type=F32))
        state_ref[...] = state_t
        o = jnp.concatenate(outs, axis=0)
        o = o * lax.rsqrt(jnp.mean(o * o, axis=-1, keepdims=True) + RMS_EPS) * gn_ref[...]
        o_ref[0, sl, :] = (o * _silu(r_ref[0, sl, :].astype(F32))).astype(o_ref.dtype)
        return carry

    lax.fori_loop(0, n_groups, body, 0)


def gla_core(qk, vr, z, wgu, bg, gn, *, heads, dk, dv, step):
    bsz, s, _ = qk.shape
    zr = z.shape[-1]
    k_off = heads
    r_off = heads
    return pl.pallas_call(
        functools.partial(_gla_body, chunk=GLA_CHUNK, group=GLA_GROUP,
                          n_groups=step // (GLA_GROUP * GLA_CHUNK), dk=dk),
        grid=(bsz, heads, s // step),
        in_specs=[pl.BlockSpec((1, step, dk), lambda b, h, t: (b, t, h)),
                  pl.BlockSpec((1, step, dk), lambda b, h, t: (b, t, k_off + h)),
                  pl.BlockSpec((1, step, dv), lambda b, h, t: (b, t, h)),
                  pl.BlockSpec((1, step, dv), lambda b, h, t: (b, t, r_off + h)),
                  pl.BlockSpec((1, step, zr), lambda b, h, t: (b, t, 0)),
                  pl.BlockSpec((1, zr, dk), lambda b, h, t: (h, 0, 0)),
                  pl.BlockSpec((1, 1, dk), lambda b, h, t: (h, 0, 0)),
                  pl.BlockSpec((1, dv), lambda b, h, t: (0, 0))],
        out_specs=pl.BlockSpec((1, step, dv), lambda b, h, t: (b, t, h)),
        out_shape=jax.ShapeDtypeStruct((bsz, s, heads * dv), BF16),
        scratch_shapes=[pltpu.VMEM((dv, dk), F32)],
        compiler_params=_cparams(("parallel", "parallel", "arbitrary")),
        name="gla_core",
    )(qk, qk, vr, vr, z, wgu, bg, gn)


MASK_PENALTY = -3.0e38
LOG2_E = math.log2(math.e)
SUM_ROWS = 16


def _moba_body(q_ref, k_ref, v_ref, o_ref, kmean_ref, kaug_ref, vt_ref, rhs_ref, m_ref, acc_ref,
               *, blk, n_blk, rep, hd, topk):
    i = pl.program_id(2)
    nq = rep * blk

    @pl.when(i == 0)
    def _():
        lane = lax.broadcasted_iota(jnp.int32, (blk, hd), 1)
        for j in range(n_blk):
            kj = k_ref[0, j * blk:(j + 1) * blk, :]
            kaug_ref[j, :, :hd] = kj
            kaug_ref[j, :, hd:] = jnp.where(lane == j, 1.0, 0.0).astype(BF16)
            kmean_ref[j:j + 1, :] = jnp.sum(kj.astype(F32), axis=0, keepdims=True) / blk
            vt_ref[j, :hd, :] = v_ref[0, j * blk:(j + 1) * blk, :].astype(F32).T.astype(BF16)
            vt_ref[j, hd:, :] = jnp.ones((SUM_ROWS, blk), BF16)
        rhs_ref[hd + n_blk:, :] = jnp.zeros((hd - n_blk, nq), BF16)

    qt = jnp.concatenate([q_ref[0, :, r * hd:(r + 1) * hd].astype(F32).T for r in range(rep)], axis=1)
    qt = qt.astype(BF16)
    gate = jnp.dot(kmean_ref[...].astype(BF16), qt, preferred_element_type=F32)
    blk_id = lax.broadcasted_iota(jnp.int32, (n_blk, nq), 0)
    past = blk_id < i
    gate = jnp.where(past, gate, -jnp.inf)
    beaten = jnp.zeros((n_blk, nq), jnp.int32)
    for c in range(n_blk):
        gc = gate[c:c + 1, :]
        beaten = beaten + jnp.where(gc > gate, 1, jnp.where(gc == gate, jnp.where(c < blk_id, 1, 0), 0))
    penalty = jnp.where(past, jnp.where(beaten < topk, 0.0, MASK_PENALTY),
                        jnp.where(blk_id == i, 0.0, MASK_PENALTY))
    rhs_ref[:hd, :] = qt
    rhs_ref[hd:hd + n_blk, :] = penalty.astype(BF16)

    m_ref[...] = jnp.full((1, nq), -jnp.inf, F32)
    acc_ref[...] = jnp.zeros_like(acc_ref)
    group = 4

    def attend(j0, n, own):
        keys = kaug_ref[pl.ds(j0, n)].reshape(n * blk, 2 * hd)
        s = jnp.dot(keys, rhs_ref[...], preferred_element_type=F32)
        if own:
            key = lax.broadcasted_iota(jnp.int32, (blk, nq), 0)
            qpos = lax.rem(lax.broadcasted_iota(jnp.int32, (blk, nq), 1), blk)
            tail = jnp.where(key <= qpos, s[(n - 1) * blk:], -jnp.inf)
            s = tail if n == 1 else jnp.concatenate([s[:(n - 1) * blk], tail], axis=0)
        m_old = m_ref[...]
        m_new = jnp.maximum(m_old, jnp.max(s, axis=0, keepdims=True))
        p = jnp.exp2(s - m_new).astype(BF16)
        upd = jnp.exp2(m_old - m_new) * acc_ref[...]
        for c in range(n):
            upd = upd + jnp.dot(vt_ref[j0 + c], p[c * blk:(c + 1) * blk], preferred_element_type=F32)
        acc_ref[...] = upd
        m_ref[...] = m_new

    n_last = i + 1 - (i // group) * group

    def full_group(jj, carry):
        attend(group * jj, group, False)
        return carry

    lax.fori_loop(0, i // group, full_group, 0)
    for n in range(1, group + 1):
        @pl.when(n_last == n)
        def _():
            attend(i + 1 - n, n, True)

    acc = acc_ref[...]
    out = acc[:hd] * (1.0 / acc[hd:hd + 1])
    for r in range(rep):
        o_ref[0, :, r * hd:(r + 1) * hd] = out[:, r * blk:(r + 1) * blk].T.astype(o_ref.dtype)


def moba_attention(qkv, *, heads, kv_heads, hd, blk, topk):
    bsz, s, _ = qkv.shape
    rep = heads // kv_heads
    n_blk = s // blk
    nq = rep * blk
    assert n_blk <= hd and n_blk % 16 == 0
    k_off = heads
    v_off = heads + kv_heads
    return pl.pallas_call(
        functools.partial(_moba_body, blk=blk, n_blk=n_blk, rep=rep, hd=hd, topk=topk),
        grid=(bsz, kv_heads, n_blk),
        in_specs=[pl.BlockSpec((1, blk, rep * hd), lambda b, g, i: (b, i, g)),
                  pl.BlockSpec((1, s, hd), lambda b, g, i: (b, 0, k_off + g)),
                  pl.BlockSpec((1, s, hd), lambda b, g, i: (b, 0, v_off + g))],
        out_specs=pl.BlockSpec((1, blk, rep * hd), lambda b, g, i: (b, i, g)),
        out_shape=jax.ShapeDtypeStruct((bsz, s, heads * hd), BF16),
        scratch_shapes=[pltpu.VMEM((n_blk, hd), F32),
                        pltpu.VMEM((n_blk, blk, 2 * hd), BF16),
                        pltpu.VMEM((n_blk, hd + SUM_ROWS, blk), BF16),
                        pltpu.VMEM((2 * hd, nq), BF16),
                        pltpu.VMEM((1, nq), F32),
                        pltpu.VMEM((hd + SUM_ROWS, nq), F32)],
        compiler_params=_cparams(("parallel", "parallel", "arbitrary")),
        name="moba_attention",
    )(qkv, qkv, qkv)


def _router_body(x_ref, w_ref, o_ref, *, n_experts):
    logits = jnp.dot(x_ref[...], w_ref[...], preferred_element_type=F32)
    lane = lax.broadcasted_iota(jnp.int32, logits.shape, 1)
    lane_f = lane.astype(F32)
    big = float(logits.shape[1])
    lg = jnp.where(lane < n_experts, logits, -jnp.inf)
    m1 = jnp.max(lg, axis=-1, keepdims=True)
    i1 = jnp.min(jnp.where(lg == m1, lane_f, big), axis=-1, keepdims=True)
    lg2 = jnp.where(lane_f == i1, -jnp.inf, lg)
    m2 = jnp.max(lg2, axis=-1, keepdims=True)
    i2 = jnp.min(jnp.where(lg2 == m2, lane_f, big), axis=-1, keepdims=True)
    t = jnp.exp(m2 - m1)
    g1 = 1.0 / (1.0 + t)
    g2 = t / (1.0 + t)
    o_ref[...] = jnp.where(lane == 0, i1,
                           jnp.where(lane == 1, i2, jnp.where(lane == 2, g1, jnp.where(lane == 3, g2, 0.0))))


def router_top2(x, w_pad, *, tm, n_experts):
    m, k = x.shape
    n = w_pad.shape[1]
    return pl.pallas_call(
        functools.partial(_router_body, n_experts=n_experts),
        grid=(m // tm,),
        in_specs=[pl.BlockSpec((tm, k), lambda i: (i, 0)),
                  pl.BlockSpec((k, n), lambda i: (0, 0))],
        out_specs=pl.BlockSpec((tm, n), lambda i: (i, 0)),
        out_shape=jax.ShapeDtypeStruct((m, n), F32),
        compiler_params=_cparams(("parallel",)),
        name="router_top2",
    )(x, w_pad)


def _row_copy(src_hbm, dst_vmem, sem, src_row, dst_row):
    return pltpu.make_async_copy(src_hbm.at[pl.ds(src_row, 1), :], dst_vmem.at[pl.ds(dst_row, 1), :], sem)


def _token_copy(src_hbm, dst_vmem, sem, src_row, dst_row):
    return pltpu.make_async_copy(src_hbm.at[src_row], dst_vmem.at[dst_row], sem)


def _store_token_major(ref, rows, x):
    for c in range(x.shape[1] // LANES):
        ref[rows, c, :] = x[:, c * LANES:(c + 1) * LANES]


def _load_token_major(x):
    return jnp.concatenate([x[:, c, :] for c in range(x.shape[1])], axis=1)


def _pack_bf16_pairs(x):
    half = x.shape[1] // 2
    lo = pltpu.bitcast(x[:, :half].astype(BF16).astype(F32), jnp.uint32)
    hi = pltpu.bitcast(x[:, half:].astype(BF16).astype(F32), jnp.uint32)
    return (hi & jnp.uint32(0xFFFF0000)) | (lo >> 16)


def _unpack_bf16_pairs(w):
    lo = pltpu.bitcast(w << 16, F32).astype(BF16)
    hi = pltpu.bitcast(w & jnp.uint32(0xFFFF0000), F32).astype(BF16)
    return jnp.concatenate([lo, hi], axis=1)


def _gather_body(tok_ref, base_ref, nrows_ref, hp_hbm, o_ref, buf0, buf1, sem, *, tm, sub, rows):
    i = pl.program_id(0)
    n_it = tm // rows

    def issue(dst, dst_sem, base, g):
        for u in range(rows):
            r = g * rows + u
            _token_copy(hp_hbm, dst, dst_sem, tok_ref[base + r], r).start()

    def unpack(src, g):
        sl = pl.ds(pl.multiple_of(g * rows, rows), rows)
        o_ref[sl, :] = _unpack_bf16_pairs(_load_token_major(src[sl]))

    @pl.when(i == 0)
    def _():
        base = base_ref[0]

        def first(g, c):
            issue(buf0, sem.at[0], base, g)
            return c

        lax.fori_loop(0, n_it, first, 0)

    nxt = i + 1
    used = i * tm < nrows_ref[0]
    has_next = jnp.logical_and(nxt < pl.num_programs(0), nxt * tm < nrows_ref[0])

    @pl.when(jnp.logical_not(used))
    def _():
        o_ref[...] = jnp.zeros_like(o_ref)

    for parity, (cur, other) in enumerate(((buf0, buf1), (buf1, buf0))):
        here = i % 2 == parity

        @pl.when(jnp.logical_and(used, here))
        def _():
            pltpu.make_async_copy(hp_hbm.at[pl.ds(0, tm)], cur, sem.at[parity]).wait()

        @pl.when(jnp.logical_and(has_next, here))
        def _():
            base = base_ref[nxt * (tm // sub)]

            def both(g, c):
                issue(other, sem.at[1 - parity], base, g)
                unpack(cur, g)
                return c

            lax.fori_loop(0, n_it, both, 0)

        @pl.when(jnp.logical_and(jnp.logical_and(used, jnp.logical_not(has_next)), here))
        def _():
            def last(g, c):
                unpack(cur, g)
                return c

            lax.fori_loop(0, n_it, last, 0)


def gather_rows(hp, sorted_tok, sub_base, n_used_rows, *, cap, tm, sub):
    half = hp.shape[1] * hp.shape[2]
    return pl.pallas_call(
        functools.partial(_gather_body, tm=tm, sub=sub, rows=16),
        grid_spec=pltpu.PrefetchScalarGridSpec(
            num_scalar_prefetch=3,
            grid=(cap // tm,),
            in_specs=[pl.BlockSpec(memory_space=pl.ANY)],
            out_specs=pl.BlockSpec((tm, 2 * half), lambda i, tok, base, nused: (i, 0)),
            scratch_shapes=[pltpu.VMEM((tm,) + hp.shape[1:], jnp.uint32),
                            pltpu.VMEM((tm,) + hp.shape[1:], jnp.uint32),
                            pltpu.SemaphoreType.DMA((2,))]),
        out_shape=jax.ShapeDtypeStruct((cap, 2 * half), BF16),
        compiler_params=_cparams(("arbitrary",)),
        name="gather_rows",
    )(sorted_tok, sub_base, n_used_rows, hp)


def _grouped_tile(i, n_rows, tm):
    return jnp.minimum(i, (n_rows[0] - 1) // tm)


def _grouped_fresh(i, sub_e, ratio):
    prev = sub_e[(jnp.maximum(i, 1) - 1) * ratio]
    return jnp.logical_or(i == 0, sub_e[i * ratio] != prev)


def _moe_up_body(sub_e, sub_n, n_rows, x_ref, wg_ref, wu_ref, o_ref, wgb_ref, wub_ref, *, tm, sub):
    i = pl.program_id(1)
    ratio = tm // sub
    used = i * tm < n_rows[0]

    @pl.when(jnp.logical_and(used, _grouped_fresh(i, sub_e, ratio)))
    def _():
        wgb_ref[...] = wg_ref[0].astype(BF16)
        wub_ref[...] = wu_ref[0].astype(BF16)

    def chunk(s):
        x = x_ref[s * sub:(s + 1) * sub, :]
        g = jnp.dot(x, wgb_ref[...], preferred_element_type=F32)
        u = jnp.dot(x, wub_ref[...], preferred_element_type=F32)
        o_ref[s * sub:(s + 1) * sub, :] = (_silu(g) * u).astype(o_ref.dtype)

    full = jnp.logical_and(used, sub_n[i * ratio + ratio - 1] > 0)

    @pl.when(full)
    def _():
        for s in range(ratio):
            chunk(s)

    for s in range(ratio):
        live = jnp.logical_and(used, sub_n[i * ratio + s] > 0)

        @pl.when(jnp.logical_and(live, jnp.logical_not(full)))
        def _():
            chunk(s)

        @pl.when(jnp.logical_not(live))
        def _():
            o_ref[s * sub:(s + 1) * sub, :] = jnp.zeros((sub, o_ref.shape[1]), o_ref.dtype)


def moe_swiglu_up(xs, w_gu, sub_e, sub_n, n_rows, *, tm, tf, sub):
    cap, d = xs.shape
    f = w_gu.shape[2] // 2
    nf = f // tf
    ratio = tm // sub

    def x_map(j, i, se, sn, nr):
        return (_grouped_tile(i, nr, tm), 0)

    def wg_map(j, i, se, sn, nr):
        return (se[_grouped_tile(i, nr, tm) * ratio], 0, j)

    def wu_map(j, i, se, sn, nr):
        return (se[_grouped_tile(i, nr, tm) * ratio], 0, nf + j)

    def o_map(j, i, se, sn, nr):
        return (i, j)

    return pl.pallas_call(
        functools.partial(_moe_up_body, tm=tm, sub=sub),
        grid_spec=pltpu.PrefetchScalarGridSpec(
            num_scalar_prefetch=3,
            grid=(nf, cap // tm),
            in_specs=[pl.BlockSpec((tm, d), x_map),
                      pl.BlockSpec((1, d, tf), wg_map),
                      pl.BlockSpec((1, d, tf), wu_map)],
            out_specs=pl.BlockSpec((tm, tf), o_map),
            scratch_shapes=[pltpu.VMEM((d, tf), BF16), pltpu.VMEM((d, tf), BF16)]),
        out_shape=jax.ShapeDtypeStruct((cap, f), BF16),
        compiler_params=_cparams(("arbitrary", "arbitrary")),
        name="moe_swiglu_up",
    )(sub_e, sub_n, n_rows, xs, w_gu, w_gu)


def _moe_down_body(sub_e, sub_n, n_rows, x_ref, w_ref, o_ref, wb_ref, *, tm, sub):
    i = pl.program_id(1)
    ratio = tm // sub
    used = i * tm < n_rows[0]

    @pl.when(jnp.logical_and(used, _grouped_fresh(i, sub_e, ratio)))
    def _():
        wb_ref[...] = w_ref[0].astype(BF16)

    def chunk(s):
        o_ref[s * sub:(s + 1) * sub, :] = jnp.dot(
            x_ref[s * sub:(s + 1) * sub, :], wb_ref[...], preferred_element_type=F32)

    full = jnp.logical_and(used, sub_n[i * ratio + ratio - 1] > 0)

    @pl.when(full)
    def _():
        for s in range(ratio):
            chunk(s)

    for s in range(ratio):
        live = jnp.logical_and(used, sub_n[i * ratio + s] > 0)

        @pl.when(jnp.logical_and(live, jnp.logical_not(full)))
        def _():
            chunk(s)

        @pl.when(jnp.logical_not(live))
        def _():
            o_ref[s * sub:(s + 1) * sub, :] = jnp.zeros((sub, o_ref.shape[1]), o_ref.dtype)


def moe_down(act, w_down, sub_e, sub_n, n_rows, *, tm, tn, sub):
    cap, f = act.shape
    d = w_down.shape[2]
    ratio = tm // sub

    def x_map(j, i, se, sn, nr):
        return (_grouped_tile(i, nr, tm), 0)

    def w_map(j, i, se, sn, nr):
        return (se[_grouped_tile(i, nr, tm) * ratio], 0, j)

    def o_map(j, i, se, sn, nr):
        return (i, j)

    return pl.pallas_call(
        functools.partial(_moe_down_body, tm=tm, sub=sub),
        grid_spec=pltpu.PrefetchScalarGridSpec(
            num_scalar_prefetch=3,
            grid=(d // tn, cap // tm),
            in_specs=[pl.BlockSpec((tm, f), x_map),
                      pl.BlockSpec((1, f, tn), w_map)],
            out_specs=pl.BlockSpec((tm, tn), o_map),
            scratch_shapes=[pltpu.VMEM((f, tn), BF16)]),
        out_shape=jax.ShapeDtypeStruct((cap, d), F32),
        compiler_params=_cparams(("arbitrary", "arbitrary")),
        name="moe_down",
    )(sub_e, sub_n, n_rows, act, w_down)


def _combine_body(dest_ref, y_hbm, gates_ref, res_ref, g_ref, b_ref, o_ref, *scratch, tm, rows):
    bufs = (scratch[:TOP_K], scratch[TOP_K:2 * TOP_K])
    sem = scratch[2 * TOP_K]
    i = pl.program_id(0)
    n_it = tm // rows

    def issue(t, dst, parity, g):
        for u in range(rows):
            r = g * rows + u
            for k in range(TOP_K):
                _row_copy(y_hbm, dst[k], sem.at[parity, k], dest_ref[TOP_K * (t * tm + r) + k], r).start()

    def finish(src, g):
        sl = pl.ds(pl.multiple_of(g * rows, rows), rows)
        gates = gates_ref[sl, :]
        y = None
        for k in range(TOP_K):
            term = src[k][sl, :] * gates[:, TOP_K + k:TOP_K + k + 1]
            y = term if y is None else y + term
        o_ref[sl, :] = _layer_norm_rows(DEEPNORM_ALPHA * res_ref[sl, :] + y, g_ref[...], b_ref[...])

    @pl.when(i == 0)
    def _():
        def first(g, c):
            issue(0, bufs[0], 0, g)
            return c

        lax.fori_loop(0, n_it, first, 0)

    nxt = i + 1
    has_next = nxt < pl.num_programs(0)
    for parity in range(2):
        cur, other = bufs[parity], bufs[1 - parity]
        here = i % 2 == parity

        @pl.when(here)
        def _():
            for k in range(TOP_K):
                pltpu.make_async_copy(y_hbm.at[pl.ds(0, tm), :], cur[k], sem.at[parity, k]).wait()

        @pl.when(jnp.logical_and(has_next, here))
        def _():
            def both(g, c):
                issue(nxt, other, 1 - parity, g)
                finish(cur, g)
                return c

            lax.fori_loop(0, n_it, both, 0)

        @pl.when(jnp.logical_and(jnp.logical_not(has_next), here))
        def _():
            def last(g, c):
                finish(cur, g)
                return c

            lax.fori_loop(0, n_it, last, 0)


def moe_combine_post_norm(y_rows, dest, routing, res, g, b, *, tm):
    n_tok, d = res.shape
    lanes = routing.shape[1]
    return pl.pallas_call(
        functools.partial(_combine_body, tm=tm, rows=min(64, tm)),
        grid_spec=pltpu.PrefetchScalarGridSpec(
            num_scalar_prefetch=1,
            grid=(n_tok // tm,),
            in_specs=[pl.BlockSpec(memory_space=pl.ANY),
                      pl.BlockSpec((tm, lanes), lambda i, dst: (i, 0)),
                      pl.BlockSpec((tm, d), lambda i, dst: (i, 0)),
                      pl.BlockSpec((1, d), lambda i, dst: (0, 0)),
                      pl.BlockSpec((1, d), lambda i, dst: (0, 0))],
            out_specs=pl.BlockSpec((tm, d), lambda i, dst: (i, 0)),
            scratch_shapes=[pltpu.VMEM((tm, d), F32) for _ in range(2 * TOP_K)]
            + [pltpu.SemaphoreType.DMA((2, TOP_K))]),
        out_shape=jax.ShapeDtypeStruct((n_tok, d), F32),
        compiler_params=_cparams(("arbitrary",)),
        name="moe_combine_post_norm",
    )(dest, y_rows, routing, res, g.reshape(1, d), b.reshape(1, d))


def _group_rows(seg, n_seg, tile, sub):
    n_asg = seg.shape[0]
    cap = n_asg + n_seg * tile
    onehot = (seg[:, None] == jnp.arange(n_seg, dtype=jnp.int32)[None, :]).astype(jnp.int32)
    csum = jnp.cumsum(onehot, axis=0)
    rank = jnp.sum(csum * onehot, axis=1) - 1
    counts = csum[-1]
    padded = (counts + tile - 1) // tile * tile
    pad_end = jnp.cumsum(padded)
    pad_start = pad_end - padded
    dest = (pad_start[seg] + rank).astype(jnp.int32)
    order = jnp.sort(seg * n_asg + jnp.arange(n_asg, dtype=jnp.int32))
    sorted_tok = jnp.pad((order % n_asg) // TOP_K, (0, tile)).astype(jnp.int32)
    sub_start = jnp.arange(cap // sub, dtype=jnp.int32) * sub
    sub_e = jnp.minimum(jnp.searchsorted(pad_end, sub_start, side="right"), n_seg - 1).astype(jnp.int32)
    sub_n = jnp.clip(counts[sub_e] - (sub_start - pad_start[sub_e]), 0, sub).astype(jnp.int32)
    start = jnp.cumsum(counts) - counts
    sub_base = (sub_start - (pad_start - start)[sub_e]).astype(jnp.int32)
    n_rows = pad_end[-1:].astype(jnp.int32)
    return dest, sorted_tok, sub_base, sub_e, sub_n, n_rows, cap


def _gla_layer(h, hb, w_in, w_gate_up, b_gate, g_norm, w_o, ln_g, ln_b):
    bsz, s, d = h.shape
    dk = (d // 2) // GLA_HEADS
    dv = d // GLA_HEADS
    n_main = 2 * GLA_HEADS * dk + 2 * GLA_HEADS * dv
    x2 = hb.reshape(bsz * s, d)
    n_qk = 2 * GLA_HEADS * dk
    tm = _pick(bsz * s, (1024, 512, 256, 128, 64))
    qk = matmul(x2, w_in[:, :n_qk].astype(BF16), tm=tm, tn=_pick(n_qk, (1024, 512, 256, 128)), out_dtype=F32)
    vr = matmul(x2, w_in[:, n_qk:n_main].astype(BF16), tm=tm, tn=_pick(n_main - n_qk, (1024, 512, 256, 128)),
                out_dtype=F32)
    w_z = jnp.pad(w_in[:, n_main:], ((0, 0), (0, LANES - GLA_GATE_RANK))).astype(BF16)
    z = matmul(x2, w_z, tm=_pick(bsz * s, (1024, 512, 256, 128, 64)), tn=LANES, out_dtype=F32)
    wgu = jnp.pad(w_gate_up, ((0, LANES - GLA_GATE_RANK), (0, 0))).astype(BF16)
    wgu = wgu.reshape(LANES, GLA_HEADS, dk).transpose(1, 0, 2)
    og = gla_core(qk.reshape(bsz, s, n_qk), vr.reshape(bsz, s, n_main - n_qk), z.reshape(bsz, s, LANES), wgu,
                  b_gate.reshape(GLA_HEADS, 1, dk), g_norm.reshape(1, dv),
                  heads=GLA_HEADS, dk=dk, dv=dv, step=min(GLA_STEP_TOKENS, s))
    out, outb = matmul_post_norm(og.reshape(bsz * s, d), w_o.astype(BF16), h.reshape(bsz * s, d), ln_g, ln_b,
                                 tm=512, sub=256)
    return out.reshape(bsz, s, d), outb.reshape(bsz, s, d)


def _moba_layer(h, hb, w_q, w_kv, w_o, ln_g, ln_b):
    bsz, s, d = h.shape
    hd = d // MOBA_HEADS
    w_all = jnp.concatenate([w_q, w_kv], axis=1).astype(BF16)
    n_all = w_all.shape[1]
    tn = _pick(math.gcd(d, n_all), (1024, 512, 256, 128))
    qkv = matmul(hb.reshape(bsz * s, d), w_all, tm=_pick(bsz * s, (1024, 512, 256, 128, 64)), tn=tn,
                 out_dtype=BF16, n_scaled=d // tn, scale=hd ** -0.5 * LOG2_E)
    att = moba_attention(qkv.reshape(bsz, s, n_all), heads=MOBA_HEADS, kv_heads=MOBA_KV_HEADS, hd=hd,
                         blk=MOBA_BLOCK, topk=MOBA_TOPK)
    out, outb, outp = matmul_post_norm(att.reshape(bsz * s, d), w_o.astype(BF16), h.reshape(bsz * s, d),
                                       ln_g, ln_b, tm=512, sub=256, packed=True)
    return out.reshape(bsz, s, d), outb.reshape(bsz, s, d), outp


def _dense_ffn(h, hb, w_gu, w_down, ln_g, ln_b):
    bsz, s, d = h.shape
    f = w_down.shape[0]
    f_pad = -(-f // 512) * 512
    w_gu_b = w_gu.astype(BF16)
    wg = jnp.pad(w_gu_b[:, :f], ((0, 0), (0, f_pad - f)))
    wu = jnp.pad(w_gu_b[:, f:], ((0, 0), (0, f_pad - f)))
    wd = jnp.pad(w_down.astype(BF16), ((0, f_pad - f), (0, 0)))
    act = swiglu_up(hb.reshape(bsz * s, d), wg, wu, tm=_pick(bsz * s, (1024, 512, 256, 128, 64)), tf=512)
    out, outb = matmul_post_norm(act, wd, h.reshape(bsz * s, d), ln_g, ln_b, tm=256, sub=128)
    return out.reshape(bsz, s, d), outb.reshape(bsz, s, d)


def _moe_ffn(h, hb, hp, w_router, w_gu, w_down, ln_g, ln_b):
    bsz, s, d = h.shape
    n_tok = bsz * s
    h2 = h.reshape(n_tok, d)
    w_r = jnp.pad(w_router, ((0, 0), (0, LANES - N_EXPERTS))).astype(BF16)
    routing = router_top2(hb.reshape(n_tok, d), w_r, tm=_pick(n_tok, (1024, 512, 256, 128, 64)),
                          n_experts=N_EXPERTS)
    seg = routing[:, :TOP_K].astype(jnp.int32).reshape(-1)
    tile = min(MOE_TILE, n_tok)
    sub = min(MOE_SUB, tile)
    dest, sorted_tok, sub_base, sub_e, sub_n, n_rows, cap = _group_rows(seg, N_EXPERTS, tile, sub)
    xs = gather_rows(hp, sorted_tok, sub_base, n_rows, cap=cap, tm=min(GATHER_TILE, tile), sub=sub)
    f = w_down.shape[1]
    act = moe_swiglu_up(xs, w_gu, sub_e, sub_n, n_rows, tm=tile, tf=_pick(f, (512, 256, 128)), sub=sub)
    y_rows = moe_down(act, w_down, sub_e, sub_n, n_rows, tm=min(MOE_DOWN_TILE, tile),
                      tn=_pick(d, (512, 256, 128)), sub=sub)
    out = moe_combine_post_norm(y_rows, dest, routing, h2, ln_g, ln_b, tm=_pick(n_tok, (256, 128, 64)))
    return out.reshape(bsz, s, d)


def kernel(x, w_in_a, w_gate_up_a, b_gate_a, g_norm_a, w_o_a, w_kv_shared, w_q_b, w_o_b,
           ln_mix_g, ln_mix_b, w_gu_dense, w_down_dense, w_router, w_gu_moe, w_down_moe,
           ln_ffn_g, ln_ffn_b):
    h, hb = _gla_layer(x, x.astype(BF16), w_in_a[0], w_gate_up_a[0], b_gate_a[0], g_norm_a[0], w_o_a[0],
                       ln_mix_g[0], ln_mix_b[0])
    h, hb = _dense_ffn(h, hb, w_gu_dense[0], w_down_dense[0], ln_ffn_g[0], ln_ffn_b[0])
    h, hb, hp = _moba_layer(h, hb, w_q_b[0], w_kv_shared, w_o_b[0], ln_mix_g[1], ln_mix_b[1])
    return _moe_ffn(h, hb, hp, w_router[0], w_gu_moe[0], w_down_moe[0], ln_ffn_g[1], ln_ffn_b[1])
```

```python
import functools
import math

import jax
import jax.numpy as jnp
from jax import lax
from jax.experimental import pallas as pl
from jax.experimental.pallas import tpu as pltpu

F32 = jnp.float32
BF16 = jnp.bfloat16

DEPTH = 2
GLA_HEADS = 4
GLA_GATE_RANK = 16
GLA_TAU = 16.0
GLA_CHUNK = 64
MOBA_HEADS = 16
MOBA_KV_HEADS = 4
MOBA_BLOCK = 256
MOBA_TOPK = 3
N_EXPERTS = 8
TOP_K = 2
LN_EPS = 1e-5
RMS_EPS = 1e-6
DEEPNORM_ALPHA = (2 * DEPTH) ** 0.25

LANES = 128
VMEM_LIMIT_BYTES = 56 * 1024 * 1024

GLA_GROUP = 4
GLA_STEP_TOKENS = 512
MOE_TILE = 1024
MOE_SUB = 256
MOE_DOWN_TILE = 512
GATHER_TILE = 512


def _cparams(sem):
    return pltpu.CompilerParams(dimension_semantics=sem, vmem_limit_bytes=VMEM_LIMIT_BYTES)


def _pick(n, prefs):
    for p in prefs:
        if n % p == 0:
            return p
    return n


def _mm_body(x_ref, w_ref, o_ref, wb_ref, *, scale):
    @pl.when(pl.program_id(1) == 0)
    def _():
        wb_ref[...] = w_ref[...].astype(BF16)

    acc = jnp.dot(x_ref[...], wb_ref[...], preferred_element_type=F32)
    if scale != 1.0:
        acc = acc * scale
    o_ref[...] = acc.astype(o_ref.dtype)


def matmul(x, w, *, tm, tn, out_dtype, col_start=0, n_cols=None, scale=1.0):
    m, k = x.shape
    n_cols = w.shape[1] - col_start if n_cols is None else n_cols
    assert col_start % tn == 0 and n_cols % tn == 0
    first = col_start // tn
    return pl.pallas_call(
        functools.partial(_mm_body, scale=scale),
        grid=(n_cols // tn, m // tm),
        in_specs=[pl.BlockSpec((tm, k), lambda j, i: (i, 0)),
                  pl.BlockSpec((k, tn), lambda j, i: (0, first + j))],
        out_specs=pl.BlockSpec((tm, tn), lambda j, i: (i, j)),
        out_shape=jax.ShapeDtypeStruct((m, n_cols), out_dtype),
        scratch_shapes=[pltpu.VMEM((k, tn), BF16)],
        compiler_params=_cparams(("arbitrary", "arbitrary")),
        name="matmul",
    )(x, w)


def _layer_norm_rows(t, g, b):
    mu = jnp.mean(t, axis=-1, keepdims=True)
    d = t - mu
    var = jnp.mean(d * d, axis=-1, keepdims=True)
    return d * lax.rsqrt(var + LN_EPS) * g + b


def _mm_ln_body(x_ref, w_ref, res_ref, g_ref, b_ref, o_ref, ob_ref, *rest, tm, sub, packed, cast):
    if cast:
        wb_ref = rest[-1]

        @pl.when(pl.program_id(0) == 0)
        def _():
            wb_ref[...] = w_ref[...].astype(BF16)
    else:
        wb_ref = w_ref
    for s in range(tm // sub):
        rows = slice(s * sub, (s + 1) * sub)
        y = jnp.dot(x_ref[rows, :], wb_ref[...], preferred_element_type=F32)
        out = _layer_norm_rows(DEEPNORM_ALPHA * res_ref[rows, :] + y, g_ref[...], b_ref[...])
        o_ref[rows, :] = out
        ob_ref[rows, :] = out.astype(BF16)
        if packed:
            _store_token_major(rest[0], rows, _pack_bf16_pairs(out))


def matmul_post_norm(x, w, res, g, b, *, tm, sub, packed=False):
    m, k = x.shape
    d = w.shape[1]
    cast = w.dtype != BF16
    out_specs = [pl.BlockSpec((tm, d), lambda i: (i, 0)), pl.BlockSpec((tm, d), lambda i: (i, 0))]
    out_shape = [jax.ShapeDtypeStruct((m, d), F32), jax.ShapeDtypeStruct((m, d), BF16)]
    if packed:
        out_specs.append(pl.BlockSpec((tm, d // 2 // LANES, LANES), lambda i: (i, 0, 0)))
        out_shape.append(jax.ShapeDtypeStruct((m, d // 2 // LANES, LANES), jnp.uint32))
    return pl.pallas_call(
        functools.partial(_mm_ln_body, tm=tm, sub=sub, packed=packed, cast=cast),
        grid=(m // tm,),
        scratch_shapes=[pltpu.VMEM((k, d), BF16)] if cast else [],
        in_specs=[pl.BlockSpec((tm, k), lambda i: (i, 0)),
                  pl.BlockSpec((k, d), lambda i: (0, 0), pipeline_mode=pl.Buffered(1)),
                  pl.BlockSpec((tm, d), lambda i: (i, 0)),
                  pl.BlockSpec((1, d), lambda i: (0, 0)),
                  pl.BlockSpec((1, d), lambda i: (0, 0))],
        out_specs=out_specs,
        out_shape=out_shape,
        compiler_params=_cparams(("arbitrary",)),
        name="matmul_post_norm",
    )(x, w, res, g.reshape(1, d), b.reshape(1, d))


def _silu(x):
    return x / (1.0 + jnp.exp(-x))


def _swiglu_body(x_ref, wg_ref, *rest, f, tm, tf, sub):
    n_u = tf // LANES
    wu_refs, o_ref, wgb_ref, wub_ref = rest[:n_u], rest[n_u], rest[n_u + 1], rest[n_u + 2]
    j = pl.program_id(0)

    @pl.when(pl.program_id(1) == 0)
    def _():
        col = j * tf + lax.broadcasted_iota(jnp.int32, wg_ref.shape, 1)
        wgb_ref[...] = jnp.where(col < f, wg_ref[...], 0.0).astype(BF16)
        for c, wu_ref in enumerate(wu_refs):
            keep = j * tf + c * LANES < f
            wub_ref[:, c * LANES:(c + 1) * LANES] = jnp.where(keep, wu_ref[...], 0.0).astype(BF16)

    for s in range(tm // sub):
        x = x_ref[s * sub:(s + 1) * sub, :]
        g = jnp.dot(x, wgb_ref[...], preferred_element_type=F32)
        u = jnp.dot(x, wub_ref[...], preferred_element_type=F32)
        o_ref[s * sub:(s + 1) * sub, :] = (_silu(g) * u).astype(o_ref.dtype)


def swiglu_up(x, w_gu, *, tm, tf, sub):
    m, k = x.shape
    f = w_gu.shape[1] // 2
    assert f % LANES == 0 and tf % LANES == 0
    nf = pl.cdiv(f, tf)
    n_u = tf // LANES
    last_u = 2 * f // LANES - 1

    def u_map(c):
        return lambda j, i: (0, jnp.minimum(f // LANES + j * n_u + c, last_u))

    return pl.pallas_call(
        functools.partial(_swiglu_body, f=f, tm=tm, tf=tf, sub=sub),
        grid=(nf, m // tm),
        in_specs=[pl.BlockSpec((tm, k), lambda j, i: (i, 0)),
                  pl.BlockSpec((k, tf), lambda j, i: (0, j))]
        + [pl.BlockSpec((k, LANES), u_map(c)) for c in range(n_u)],
        out_specs=pl.BlockSpec((tm, tf), lambda j, i: (i, j)),
        out_shape=jax.ShapeDtypeStruct((m, nf * tf), BF16),
        scratch_shapes=[pltpu.VMEM((k, tf), BF16), pltpu.VMEM((k, tf), BF16)],
        compiler_params=_cparams(("arbitrary", "arbitrary")),
        name="swiglu_up",
    )(x, w_gu, *([w_gu] * n_u))


def _gla_body(q_ref, k_ref, v_ref, r_ref, z_ref, wgu_ref, bg_ref, gn_ref, o_ref, state_ref,
              *, chunk, group, n_groups, dk):
    @pl.when(pl.program_id(2) == 0)
    def _():
        state_ref[...] = jnp.zeros_like(state_ref)

    gt = group * chunk
    row = lax.broadcasted_iota(jnp.int32, (gt, gt), 0)
    col = lax.broadcasted_iota(jnp.int32, (gt, gt), 1)
    causal = jnp.logical_and(row // chunk == col // chunk, col <= row)
    tril = jnp.where(causal, 1.0, 0.0).astype(BF16)
    nt = (((1,), (1,)), ((), ()))
    tn = (((0,), (0,)), ((), ()))

    def body(g, carry):
        sl = pl.ds(pl.multiple_of(g * gt, gt), gt)
        gate = jnp.dot(z_ref[0, sl, :].astype(BF16), wgu_ref[0], preferred_element_type=F32) + bg_ref[0]
        log_a = (jnp.minimum(gate, 0.0) - jnp.log(1.0 + jnp.exp(-jnp.abs(gate)))) / GLA_TAU
        la_hi = log_a.astype(BF16)
        la_lo = (log_a - la_hi.astype(F32)).astype(BF16)
        b_cum = (jnp.dot(tril, la_hi, preferred_element_type=F32)
                 + jnp.dot(tril, la_lo, preferred_element_type=F32))
        b_last = [b_cum[(c + 1) * chunk - 1:(c + 1) * chunk, :] for c in range(group)]
        b_last_rows = jnp.concatenate([jnp.broadcast_to(b, (chunk, b.shape[1])) for b in b_last], axis=0)
        q = q_ref[0, sl, :] * (dk ** -0.5)
        k = k_ref[0, sl, :]
        q_dec = (q * jnp.exp(b_cum)).astype(BF16)
        k_inv = (k * jnp.exp(-b_cum)).astype(BF16)
        k_end = (k * jnp.exp(b_last_rows - b_cum)).astype(BF16)
        v = v_ref[0, sl, :].astype(BF16)
        att = lax.dot_general(q_dec, k_inv, nt, preferred_element_type=F32)
        att = jnp.where(causal, att, 0.0).astype(BF16)
        o_intra = jnp.dot(att, v, preferred_element_type=F32)
        state_t = state_ref[...]
        outs = []
        for c in range(group):
            rows = slice(c * chunk, (c + 1) * chunk)
            outs.append(o_intra[rows] + lax.dot_general(q_dec[rows], state_t.astype(BF16), nt,
                                                        preferred_element_type=F32))
            state_t = (jnp.exp(b_last[c]) * state_t
                       + lax.dot_general(v[rows], k_end[rows], tn, preferred_element_type=F32))
        state_ref[...] = state_t
        o = jnp.concatenate(outs, axis=0)
        o = o * lax.rsqrt(jnp.mean(o * o, axis=-1, keepdims=True) + RMS_EPS) * gn_ref[...]
        o_ref[0, sl, :] = (o * _silu(r_ref[0, sl, :].astype(F32))).astype(o_ref.dtype)
        return carry

    lax.fori_loop(0, n_groups, body, 0)


def gla_core(qk, vr, z, wgu, bg, gn, *, heads, dk, dv, step):
    bsz, s, _ = qk.shape
    zr = z.shape[-1]
    k_off = heads
    r_off = heads
    return pl.pallas_call(
        functools.partial(_gla_body, chunk=GLA_CHUNK, group=GLA_GROUP,
                          n_groups=step // (GLA_GROUP * GLA_CHUNK), dk=dk),
        grid=(bsz, heads, s // step),
        in_specs=[pl.BlockSpec((1, step, dk), lambda b, h, t: (b, t, h)),
                  pl.BlockSpec((1, step, dk), lambda b, h, t: (b, t, k_off + h)),
                  pl.BlockSpec((1, step, dv), lambda b, h, t: (b, t, h)),
                  pl.BlockSpec((1, step, dv), lambda b, h, t: (b, t, r_off + h)),
                  pl.BlockSpec((1, step, zr), lambda b, h, t: (b, t, 0)),
                  pl.BlockSpec((1, zr, dk), lambda b, h, t: (h, 0, 0)),
                  pl.BlockSpec((1, 1, dk), lambda b, h, t: (h, 0, 0)),
                  pl.BlockSpec((1, dv), lambda b, h, t: (0, 0))],
        out_specs=pl.BlockSpec((1, step, dv), lambda b, h, t: (b, t, h)),
        out_shape=jax.ShapeDtypeStruct((bsz, s, heads * dv), BF16),
        scratch_shapes=[pltpu.VMEM((dv, dk), F32)],
        compiler_params=_cparams(("parallel", "parallel", "arbitrary")),
        name="gla_core",
    )(qk, qk, vr, vr, z, wgu, bg, gn)


MASK_PENALTY = -3.0e38
LOG2_E = math.log2(math.e)
SUM_ROWS = 16


def _moba_body(q_ref, k_ref, v_ref, o_ref, kmean_ref, kaug_ref, vt_ref, rhs_ref, m_ref, acc_ref,
               *, blk, n_blk, rep, hd, topk):
    i = pl.program_id(2)
    nq = rep * blk

    @pl.when(i == 0)
    def _():
        lane = lax.broadcasted_iota(jnp.int32, (blk, hd), 1)
        for j in range(n_blk):
            kj = k_ref[0, j * blk:(j + 1) * blk, :]
            kaug_ref[j, :, :hd] = kj
            kaug_ref[j, :, hd:] = jnp.where(lane == j, 1.0, 0.0).astype(BF16)
            kmean_ref[j:j + 1, :] = jnp.sum(kj.astype(F32), axis=0, keepdims=True) / blk
            vt_ref[j, :hd, :] = v_ref[0, j * blk:(j + 1) * blk, :].astype(F32).T.astype(BF16)
            vt_ref[j, hd:, :] = jnp.ones((SUM_ROWS, blk), BF16)
        rhs_ref[hd + n_blk:, :] = jnp.zeros((hd - n_blk, nq), BF16)

    qt = jnp.concatenate([q_ref[0, :, r * hd:(r + 1) * hd].astype(F32).T for r in range(rep)], axis=1)
    qt = qt.astype(BF16)
    gate = jnp.dot(kmean_ref[...].astype(BF16), qt, preferred_element_type=F32)
    blk_id = lax.broadcasted_iota(jnp.int32, (n_blk, nq), 0)
    past = blk_id < i
    gate = jnp.where(past, gate, -jnp.inf)
    beaten = jnp.zeros((n_blk, nq), jnp.int32)
    for c in range(n_blk):
        gc = gate[c:c + 1, :]
        beaten = beaten + jnp.where(gc > gate, 1, jnp.where(gc == gate, jnp.where(c < blk_id, 1, 0), 0))
    penalty = jnp.where(past, jnp.where(beaten < topk, 0.0, MASK_PENALTY),
                        jnp.where(blk_id == i, 0.0, MASK_PENALTY))
    rhs_ref[:hd, :] = qt
    rhs_ref[hd:hd + n_blk, :] = penalty.astype(BF16)

    m_ref[...] = jnp.full((1, nq), -jnp.inf, F32)
    acc_ref[...] = jnp.zeros_like(acc_ref)
    group = 4

    def attend(j0, n, own):
        keys = kaug_ref[pl.ds(j0, n)].reshape(n * blk, 2 * hd)
        s = jnp.dot(keys, rhs_ref[...], preferred_element_type=F32)
        if own:
            key = lax.broadcasted_iota(jnp.int32, (blk, nq), 0)
            qpos = lax.rem(lax.broadcasted_iota(jnp.int32, (blk, nq), 1), blk)
            tail = jnp.where(key <= qpos, s[(n - 1) * blk:], -jnp.inf)
            s = tail if n == 1 else jnp.concatenate([s[:(n - 1) * blk], tail], axis=0)
        m_old = m_ref[...]
        m_new = jnp.maximum(m_old, jnp.max(s, axis=0, keepdims=True))
        p = jnp.exp2(s - m_new).astype(BF16)
        upd = jnp.exp2(m_old - m_new) * acc_ref[...]
        for c in range(n):
            upd = upd + jnp.dot(vt_ref[j0 + c], p[c * blk:(c + 1) * blk], preferred_element_type=F32)
        acc_ref[...] = upd
        m_ref[...] = m_new

    n_last = i + 1 - (i // group) * group

    def full_group(jj, carry):
        attend(group * jj, group, False)
        return carry

    lax.fori_loop(0, i // group, full_group, 0)
    for n in range(1, group + 1):
        @pl.when(n_last == n)
        def _():
            attend(i + 1 - n, n, True)

    acc = acc_ref[...]
    out = acc[:hd] * (1.0 / acc[hd:hd + 1])
    for r in range(rep):
        o_ref[0, :, r * hd:(r + 1) * hd] = out[:, r * blk:(r + 1) * blk].T.astype(o_ref.dtype)


def moba_attention(q, kv, *, heads, kv_heads, hd, blk, topk):
    bsz, s, _ = q.shape
    rep = heads // kv_heads
    n_blk = s // blk
    nq = rep * blk
    assert n_blk <= hd and n_blk % 16 == 0
    v_off = kv_heads
    return pl.pallas_call(
        functools.partial(_moba_body, blk=blk, n_blk=n_blk, rep=rep, hd=hd, topk=topk),
        grid=(bsz, kv_heads, n_blk),
        in_specs=[pl.BlockSpec((1, blk, rep * hd), lambda b, g, i: (b, i, g)),
                  pl.BlockSpec((1, s, hd), lambda b, g, i: (b, 0, g)),
                  pl.BlockSpec((1, s, hd), lambda b, g, i: (b, 0, v_off + g))],
        out_specs=pl.BlockSpec((1, blk, rep * hd), lambda b, g, i: (b, i, g)),
        out_shape=jax.ShapeDtypeStruct((bsz, s, heads * hd), BF16),
        scratch_shapes=[pltpu.VMEM((n_blk, hd), F32),
                        pltpu.VMEM((n_blk, blk, 2 * hd), BF16),
                        pltpu.VMEM((n_blk, hd + SUM_ROWS, blk), BF16),
                        pltpu.VMEM((2 * hd, nq), BF16),
                        pltpu.VMEM((1, nq), F32),
                        pltpu.VMEM((hd + SUM_ROWS, nq), F32)],
        compiler_params=_cparams(("parallel", "parallel", "arbitrary")),
        name="moba_attention",
    )(q, kv, kv)


def _router_body(x_ref, w_ref, o_ref, *, n_experts):
    logits = jnp.dot(x_ref[...], w_ref[...], preferred_element_type=F32)
    lane = lax.broadcasted_iota(jnp.int32, logits.shape, 1)
    lane_f = lane.astype(F32)
    big = float(logits.shape[1])
    lg = jnp.where(lane < n_experts, logits, -jnp.inf)
    m1 = jnp.max(lg, axis=-1, keepdims=True)
    i1 = jnp.min(jnp.where(lg == m1, lane_f, big), axis=-1, keepdims=True)
    lg2 = jnp.where(lane_f == i1, -jnp.inf, lg)
    m2 = jnp.max(lg2, axis=-1, keepdims=True)
    i2 = jnp.min(jnp.where(lg2 == m2, lane_f, big), axis=-1, keepdims=True)
    t = jnp.exp(m2 - m1)
    g1 = 1.0 / (1.0 + t)
    g2 = t / (1.0 + t)
    o_ref[...] = jnp.where(lane == 0, i1,
                           jnp.where(lane == 1, i2, jnp.where(lane == 2, g1, jnp.where(lane == 3, g2, 0.0))))


def router_top2(x, w_pad, *, tm, n_experts):
    m, k = x.shape
    n = w_pad.shape[1]
    return pl.pallas_call(
        functools.partial(_router_body, n_experts=n_experts),
        grid=(m // tm,),
        in_specs=[pl.BlockSpec((tm, k), lambda i: (i, 0)),
                  pl.BlockSpec((k, n), lambda i: (0, 0))],
        out_specs=pl.BlockSpec((tm, n), lambda i: (i, 0)),
        out_shape=jax.ShapeDtypeStruct((m, n), F32),
        compiler_params=_cparams(("parallel",)),
        name="router_top2",
    )(x, w_pad)


def _row_copy(src_hbm, dst_vmem, sem, src_row, dst_row):
    return pltpu.make_async_copy(src_hbm.at[pl.ds(src_row, 1), :], dst_vmem.at[pl.ds(dst_row, 1), :], sem)


def _token_copy(src_hbm, dst_vmem, sem, src_row, dst_row):
    return pltpu.make_async_copy(src_hbm.at[src_row], dst_vmem.at[dst_row], sem)


def _store_token_major(ref, rows, x):
    for c in range(x.shape[1] // LANES):
        ref[rows, c, :] = x[:, c * LANES:(c + 1) * LANES]


def _load_token_major(x):
    return jnp.concatenate([x[:, c, :] for c in range(x.shape[1])], axis=1)


def _pack_bf16_pairs(x):
    half = x.shape[1] // 2
    lo = pltpu.bitcast(x[:, :half].astype(BF16).astype(F32), jnp.uint32)
    hi = pltpu.bitcast(x[:, half:].astype(BF16).astype(F32), jnp.uint32)
    return (hi & jnp.uint32(0xFFFF0000)) | (lo >> 16)


def _unpack_bf16_pairs(w):
    lo = pltpu.bitcast(w << 16, F32).astype(BF16)
    hi = pltpu.bitcast(w & jnp.uint32(0xFFFF0000), F32).astype(BF16)
    return jnp.concatenate([lo, hi], axis=1)


def _gather_body(tok_ref, base_ref, nrows_ref, hp_hbm, o_ref, buf, sem, *, tm, sub, unroll):
    i = pl.program_id(0)

    def issue_tile(t, slot):
        base = base_ref[t * (tm // sub)]

        def issue(g, c):
            for u in range(unroll):
                r = g * unroll + u
                _token_copy(hp_hbm, buf.at[slot], sem.at[slot], tok_ref[base + r], r).start()
            return c

        lax.fori_loop(0, tm // unroll, issue, 0)

    @pl.when(i == 0)
    def _():
        issue_tile(0, 0)

    nxt = i + 1

    @pl.when(jnp.logical_and(nxt < pl.num_programs(0), nxt * tm < nrows_ref[0]))
    def _():
        issue_tile(nxt, nxt % 2)

    used = i * tm < nrows_ref[0]

    @pl.when(jnp.logical_not(used))
    def _():
        o_ref[...] = jnp.zeros_like(o_ref)

    @pl.when(used)
    def _():
        slot = i % 2
        pltpu.make_async_copy(hp_hbm.at[pl.ds(0, tm)], buf.at[slot], sem.at[slot]).wait()
        o_ref[...] = _unpack_bf16_pairs(_load_token_major(buf[slot]))


def gather_rows(hp, sorted_tok, sub_base, n_used_rows, *, cap, tm, sub):
    half = hp.shape[1] * hp.shape[2]
    return pl.pallas_call(
        functools.partial(_gather_body, tm=tm, sub=sub, unroll=8),
        grid_spec=pltpu.PrefetchScalarGridSpec(
            num_scalar_prefetch=3,
            grid=(cap // tm,),
            in_specs=[pl.BlockSpec(memory_space=pl.ANY)],
            out_specs=pl.BlockSpec((tm, 2 * half), lambda i, tok, base, nused: (i, 0)),
            scratch_shapes=[pltpu.VMEM((2, tm) + hp.shape[1:], jnp.uint32), pltpu.SemaphoreType.DMA((2,))]),
        out_shape=jax.ShapeDtypeStruct((cap, 2 * half), BF16),
        compiler_params=_cparams(("arbitrary",)),
        name="gather_rows",
    )(sorted_tok, sub_base, n_used_rows, hp)


def _grouped_tile(i, n_rows, tm):
    return jnp.minimum(i, (n_rows[0] - 1) // tm)


def _grouped_fresh(i, sub_e, ratio):
    prev = sub_e[(jnp.maximum(i, 1) - 1) * ratio]
    return jnp.logical_or(i == 0, sub_e[i * ratio] != prev)


def _moe_up_body(sub_e, sub_n, n_rows, x_ref, wg_ref, wu_ref, o_ref, wgb_ref, wub_ref, *, tm, sub):
    i = pl.program_id(1)
    ratio = tm // sub
    used = i * tm < n_rows[0]

    @pl.when(jnp.logical_and(used, _grouped_fresh(i, sub_e, ratio)))
    def _():
        wgb_ref[...] = wg_ref[0].astype(BF16)
        wub_ref[...] = wu_ref[0].astype(BF16)

    def chunk(s):
        x = x_ref[s * sub:(s + 1) * sub, :]
        g = jnp.dot(x, wgb_ref[...], preferred_element_type=F32)
        u = jnp.dot(x, wub_ref[...], preferred_element_type=F32)
        o_ref[s * sub:(s + 1) * sub, :] = (_silu(g) * u).astype(o_ref.dtype)

    full = jnp.logical_and(used, sub_n[i * ratio + ratio - 1] > 0)

    @pl.when(full)
    def _():
        for s in range(ratio):
            chunk(s)

    for s in range(ratio):
        live = jnp.logical_and(used, sub_n[i * ratio + s] > 0)

        @pl.when(jnp.logical_and(live, jnp.logical_not(full)))
        def _():
            chunk(s)

        @pl.when(jnp.logical_not(live))
        def _():
            o_ref[s * sub:(s + 1) * sub, :] = jnp.zeros((sub, o_ref.shape[1]), o_ref.dtype)


def moe_swiglu_up(xs, w_gu, sub_e, sub_n, n_rows, *, tm, tf, sub):
    cap, d = xs.shape
    f = w_gu.shape[2] // 2
    nf = f // tf
    ratio = tm // sub

    def x_map(j, i, se, sn, nr):
        return (_grouped_tile(i, nr, tm), 0)

    def wg_map(j, i, se, sn, nr):
        return (se[_grouped_tile(i, nr, tm) * ratio], 0, j)

    def wu_map(j, i, se, sn, nr):
        return (se[_grouped_tile(i, nr, tm) * ratio], 0, nf + j)

    def o_map(j, i, se, sn, nr):
        return (i, j)

    return pl.pallas_call(
        functools.partial(_moe_up_body, tm=tm, sub=sub),
        grid_spec=pltpu.PrefetchScalarGridSpec(
            num_scalar_prefetch=3,
            grid=(nf, cap // tm),
            in_specs=[pl.BlockSpec((tm, d), x_map),
                      pl.BlockSpec((1, d, tf), wg_map),
                      pl.BlockSpec((1, d, tf), wu_map)],
            out_specs=pl.BlockSpec((tm, tf), o_map),
            scratch_shapes=[pltpu.VMEM((d, tf), BF16), pltpu.VMEM((d, tf), BF16)]),
        out_shape=jax.ShapeDtypeStruct((cap, f), BF16),
        compiler_params=_cparams(("arbitrary", "arbitrary")),
        name="moe_swiglu_up",
    )(sub_e, sub_n, n_rows, xs, w_gu, w_gu)


def _moe_down_body(sub_e, sub_n, n_rows, x_ref, w_ref, o_ref, wb_ref, *, tm, sub):
    i = pl.program_id(1)
    ratio = tm // sub
    used = i * tm < n_rows[0]

    @pl.when(jnp.logical_and(used, _grouped_fresh(i, sub_e, ratio)))
    def _():
        wb_ref[...] = w_ref[0].astype(BF16)

    def chunk(s):
        o_ref[s * sub:(s + 1) * sub, :] = jnp.dot(
            x_ref[s * sub:(s + 1) * sub, :], wb_ref[...], preferred_element_type=F32)

    full = jnp.logical_and(used, sub_n[i * ratio + ratio - 1] > 0)

    @pl.when(full)
    def _():
        for s in range(ratio):
            chunk(s)

    for s in range(ratio):
        live = jnp.logical_and(used, sub_n[i * ratio + s] > 0)

        @pl.when(jnp.logical_and(live, jnp.logical_not(full)))
        def _():
            chunk(s)

        @pl.when(jnp.logical_not(live))
        def _():
            o_ref[s * sub:(s + 1) * sub, :] = jnp.zeros((sub, o_ref.shape[1]), o_ref.dtype)


def moe_down(act, w_down, sub_e, sub_n, n_rows, *, tm, tn, sub):
    cap, f = act.shape
    d = w_down.shape[2]
    ratio = tm // sub

    def x_map(j, i, se, sn, nr):
        return (_grouped_tile(i, nr, tm), 0)

    def w_map(j, i, se, sn, nr):
        return (se[_grouped_tile(i, nr, tm) * ratio], 0, j)

    def o_map(j, i, se, sn, nr):
        return (i, j)

    return pl.pallas_call(
        functools.partial(_moe_down_body, tm=tm, sub=sub),
        grid_spec=pltpu.PrefetchScalarGridSpec(
            num_scalar_prefetch=3,
            grid=(d // tn, cap // tm),
            in_specs=[pl.BlockSpec((tm, f), x_map),
                      pl.BlockSpec((1, f, tn), w_map)],
            out_specs=pl.BlockSpec((tm, tn), o_map),
            scratch_shapes=[pltpu.VMEM((f, tn), BF16)]),
        out_shape=jax.ShapeDtypeStruct((cap, d), F32),
        compiler_params=_cparams(("arbitrary", "arbitrary")),
        name="moe_down",
    )(sub_e, sub_n, n_rows, act, w_down)


def _combine_body(dest_ref, y_hbm, gates_ref, res_ref, g_ref, b_ref, o_ref, buf, sem, *, tm, unroll):
    i = pl.program_id(0)

    def issue_tile(t, slot):
        def issue(g, c):
            for u in range(unroll):
                r = g * unroll + u
                for k in range(TOP_K):
                    _row_copy(y_hbm, buf.at[slot, k], sem.at[slot, k],
                              dest_ref[TOP_K * (t * tm + r) + k], r).start()
            return c

        lax.fori_loop(0, tm // unroll, issue, 0)

    @pl.when(i == 0)
    def _():
        issue_tile(0, 0)

    nxt = i + 1

    @pl.when(nxt < pl.num_programs(0))
    def _():
        issue_tile(nxt, nxt % 2)

    slot = i % 2
    gates = gates_ref[...]
    y = None
    for k in range(TOP_K):
        pltpu.make_async_copy(y_hbm.at[pl.ds(0, tm), :], buf.at[slot, k], sem.at[slot, k]).wait()
        term = buf[slot, k] * gates[:, TOP_K + k:TOP_K + k + 1]
        y = term if y is None else y + term
    o_ref[...] = _layer_norm_rows(DEEPNORM_ALPHA * res_ref[...] + y, g_ref[...], b_ref[...])


def moe_combine_post_norm(y_rows, dest, routing, res, g, b, *, tm):
    n_tok, d = res.shape
    lanes = routing.shape[1]
    return pl.pallas_call(
        functools.partial(_combine_body, tm=tm, unroll=4),
        grid_spec=pltpu.PrefetchScalarGridSpec(
            num_scalar_prefetch=1,
            grid=(n_tok // tm,),
            in_specs=[pl.BlockSpec(memory_space=pl.ANY),
                      pl.BlockSpec((tm, lanes), lambda i, dst: (i, 0)),
                      pl.BlockSpec((tm, d), lambda i, dst: (i, 0)),
                      pl.BlockSpec((1, d), lambda i, dst: (0, 0)),
                      pl.BlockSpec((1, d), lambda i, dst: (0, 0))],
            out_specs=pl.BlockSpec((tm, d), lambda i, dst: (i, 0)),
            scratch_shapes=[pltpu.VMEM((2, TOP_K, tm, d), F32), pltpu.SemaphoreType.DMA((2, TOP_K))]),
        out_shape=jax.ShapeDtypeStruct((n_tok, d), F32),
        compiler_params=_cparams(("arbitrary",)),
        name="moe_combine_post_norm",
    )(dest, y_rows, routing, res, g.reshape(1, d), b.reshape(1, d))


def _group_rows(seg, n_seg, tile, sub):
    n_asg = seg.shape[0]
    cap = n_asg + n_seg * tile
    onehot = (seg[:, None] == jnp.arange(n_seg, dtype=jnp.int32)[None, :]).astype(jnp.int32)
    csum = jnp.cumsum(onehot, axis=0)
    rank = jnp.sum(csum * onehot, axis=1) - 1
    counts = csum[-1]
    padded = (counts + tile - 1) // tile * tile
    pad_end = jnp.cumsum(padded)
    pad_start = pad_end - padded
    dest = (pad_start[seg] + rank).astype(jnp.int32)
    order = jnp.sort(seg * n_asg + jnp.arange(n_asg, dtype=jnp.int32))
    sorted_tok = jnp.pad((order % n_asg) // TOP_K, (0, tile)).astype(jnp.int32)
    sub_start = jnp.arange(cap // sub, dtype=jnp.int32) * sub
    sub_e = jnp.minimum(jnp.searchsorted(pad_end, sub_start, side="right"), n_seg - 1).astype(jnp.int32)
    sub_n = jnp.clip(counts[sub_e] - (sub_start - pad_start[sub_e]), 0, sub).astype(jnp.int32)
    start = jnp.cumsum(counts) - counts
    sub_base = (sub_start - (pad_start - start)[sub_e]).astype(jnp.int32)
    n_rows = pad_end[-1:].astype(jnp.int32)
    return dest, sorted_tok, sub_base, sub_e, sub_n, n_rows, cap


def _gla_layer(h, hb, w_in, w_gate_up, b_gate, g_norm, w_o, ln_g, ln_b):
    bsz, s, d = h.shape
    dk = (d // 2) // GLA_HEADS
    dv = d // GLA_HEADS
    n_main = 2 * GLA_HEADS * dk + 2 * GLA_HEADS * dv
    x2 = hb.reshape(bsz * s, d)
    n_qk = 2 * GLA_HEADS * dk
    tm = _pick(bsz * s, (1024, 512, 256, 128, 64))
    tn = _pick(n_qk, (1024, 512, 256, 128))
    qk = matmul(x2, w_in, tm=tm, tn=tn, out_dtype=F32, col_start=0, n_cols=n_qk)
    vr = matmul(x2, w_in, tm=tm, tn=tn, out_dtype=F32, col_start=n_qk, n_cols=n_main - n_qk)
    w_z = jnp.pad(w_in[:, n_main:], ((0, 0), (0, LANES - GLA_GATE_RANK)))
    z = matmul(x2, w_z, tm=tm, tn=LANES, out_dtype=F32)
    wgu = jnp.pad(w_gate_up, ((0, LANES - GLA_GATE_RANK), (0, 0))).astype(BF16)
    wgu = wgu.reshape(LANES, GLA_HEADS, dk).transpose(1, 0, 2)
    og = gla_core(qk.reshape(bsz, s, n_qk), vr.reshape(bsz, s, n_main - n_qk), z.reshape(bsz, s, LANES), wgu,
                  b_gate.reshape(GLA_HEADS, 1, dk), g_norm.reshape(1, dv),
                  heads=GLA_HEADS, dk=dk, dv=dv, step=min(GLA_STEP_TOKENS, s))
    out, outb = matmul_post_norm(og.reshape(bsz * s, d), w_o, h.reshape(bsz * s, d), ln_g, ln_b,
                                 tm=512, sub=256)
    return out.reshape(bsz, s, d), outb.reshape(bsz, s, d)


def _moba_layer(h, hb, w_q, w_kv, w_o, ln_g, ln_b):
    bsz, s, d = h.shape
    hd = d // MOBA_HEADS
    x2 = hb.reshape(bsz * s, d)
    tm = _pick(bsz * s, (1024, 512, 256, 128, 64))
    n_kv = w_kv.shape[1]
    q = matmul(x2, w_q, tm=tm, tn=_pick(d, (1024, 512, 256, 128)), out_dtype=BF16, scale=hd ** -0.5 * LOG2_E)
    kv = matmul(x2, w_kv, tm=tm, tn=_pick(n_kv, (1024, 512, 256, 128)), out_dtype=BF16)
    att = moba_attention(q.reshape(bsz, s, d), kv.reshape(bsz, s, n_kv), heads=MOBA_HEADS,
                         kv_heads=MOBA_KV_HEADS, hd=hd, blk=MOBA_BLOCK, topk=MOBA_TOPK)
    out, outb, outp = matmul_post_norm(att.reshape(bsz * s, d), w_o, h.reshape(bsz * s, d),
                                       ln_g, ln_b, tm=512, sub=256, packed=True)
    return out.reshape(bsz, s, d), outb.reshape(bsz, s, d), outp


def _dense_ffn(h, hb, w_gu, w_down, ln_g, ln_b):
    bsz, s, d = h.shape
    f = w_down.shape[0]
    tm = _pick(bsz * s, (1024, 512, 256, 128, 64))
    act = swiglu_up(hb.reshape(bsz * s, d), w_gu, tm=tm, tf=512, sub=min(512, tm))
    wd = jnp.pad(w_down.astype(BF16), ((0, act.shape[1] - f), (0, 0)))
    out, outb = matmul_post_norm(act, wd, h.reshape(bsz * s, d), ln_g, ln_b, tm=256, sub=128)
    return out.reshape(bsz, s, d), outb.reshape(bsz, s, d)


def _moe_ffn(h, hb, hp, w_router, w_gu, w_down, ln_g, ln_b):
    bsz, s, d = h.shape
    n_tok = bsz * s
    h2 = h.reshape(n_tok, d)
    w_r = jnp.pad(w_router, ((0, 0), (0, LANES - N_EXPERTS))).astype(BF16)
    routing = router_top2(hb.reshape(n_tok, d), w_r, tm=_pick(n_tok, (1024, 512, 256, 128, 64)),
                          n_experts=N_EXPERTS)
    seg = routing[:, :TOP_K].astype(jnp.int32).reshape(-1)
    tile = min(MOE_TILE, n_tok)
    sub = min(MOE_SUB, tile)
    dest, sorted_tok, sub_base, sub_e, sub_n, n_rows, cap = _group_rows(seg, N_EXPERTS, tile, sub)
    xs = gather_rows(hp, sorted_tok, sub_base, n_rows, cap=cap, tm=min(GATHER_TILE, tile), sub=sub)
    f = w_down.shape[1]
    act = moe_swiglu_up(xs, w_gu, sub_e, sub_n, n_rows, tm=tile, tf=_pick(f, (512, 256, 128)), sub=sub)
    y_rows = moe_down(act, w_down, sub_e, sub_n, n_rows, tm=min(MOE_DOWN_TILE, tile),
                      tn=_pick(d, (512, 256, 128)), sub=sub)
    out = moe_combine_post_norm(y_rows, dest, routing, h2, ln_g, ln_b, tm=_pick(n_tok, (256, 128, 64)))
    return out.reshape(bsz, s, d)


def kernel(x, w_in_a, w_gate_up_a, b_gate_a, g_norm_a, w_o_a, w_kv_shared, w_q_b, w_o_b,
           ln_mix_g, ln_mix_b, w_gu_dense, w_down_dense, w_router, w_gu_moe, w_down_moe,
           ln_ffn_g, ln_ffn_b):
    h, hb = _gla_layer(x, x.astype(BF16), w_in_a[0], w_gate_up_a[0], b_gate_a[0], g_norm_a[0], w_o_a[0],
                       ln_mix_g[0], ln_mix_b[0])
    h, hb = _dense_ffn(h, hb, w_gu_dense[0], w_down_dense[0], ln_ffn_g[0], ln_ffn_b[0])
    h, hb, hp = _moba_layer(h, hb, w_q_b[0], w_kv_shared, w_o_b[0], ln_mix_g[1], ln_mix_b[1])
    return _moe_ffn(h, hb, hp, w_router[0], w_gu_moe[0], w_down_moe[0], ln_ffn_g[1], ln_ffn_b[1])
```

```python
import functools
import math

import jax
import jax.numpy as jnp
from jax import lax
from jax.experimental import pallas as pl
from jax.experimental.pallas import tpu as pltpu

F32 = jnp.float32
BF16 = jnp.bfloat16

DEPTH = 2
GLA_HEADS = 4
GLA_GATE_RANK = 16
GLA_TAU = 16.0
GLA_CHUNK = 64
MOBA_HEADS = 16
MOBA_KV_HEADS = 4
MOBA_BLOCK = 256
MOBA_TOPK = 3
N_EXPERTS = 8
TOP_K = 2
LN_EPS = 1e-5
RMS_EPS = 1e-6
DEEPNORM_ALPHA = (2 * DEPTH) ** 0.25

LANES = 128
VMEM_LIMIT_BYTES = 56 * 1024 * 1024

GLA_GROUP = 4
GLA_STEP_TOKENS = 512
MOE_TILE = 1024
MOE_SUB = 256
MOE_DOWN_TILE = 512
GATHER_TILE = 512


def _cparams(sem):
    return pltpu.CompilerParams(dimension_semantics=sem, vmem_limit_bytes=VMEM_LIMIT_BYTES)


def _pick(n, prefs):
    for p in prefs:
        if n % p == 0:
            return p
    return n


def _mm_body(x_ref, w_ref, o_ref, wb_ref, *, scale):
    @pl.when(pl.program_id(1) == 0)
    def _():
        wb_ref[...] = w_ref[...].astype(BF16)

    acc = jnp.dot(x_ref[...], wb_ref[...], preferred_element_type=F32)
    if scale != 1.0:
        acc = acc * scale
    o_ref[...] = acc.astype(o_ref.dtype)


def matmul(x, w, *, tm, tn, out_dtype, col_start=0, n_cols=None, scale=1.0):
    m, k = x.shape
    n_cols = w.shape[1] - col_start if n_cols is None else n_cols
    assert col_start % tn == 0 and n_cols % tn == 0
    first = col_start // tn
    return pl.pallas_call(
        functools.partial(_mm_body, scale=scale),
        grid=(n_cols // tn, m // tm),
        in_specs=[pl.BlockSpec((tm, k), lambda j, i: (i, 0)),
                  pl.BlockSpec((k, tn), lambda j, i: (0, first + j))],
        out_specs=pl.BlockSpec((tm, tn), lambda j, i: (i, j)),
        out_shape=jax.ShapeDtypeStruct((m, n_cols), out_dtype),
        scratch_shapes=[pltpu.VMEM((k, tn), BF16)],
        compiler_params=_cparams(("arbitrary", "arbitrary")),
        name="matmul",
    )(x, w)


def _layer_norm_rows(t, g, b):
    mu = jnp.mean(t, axis=-1, keepdims=True)
    d = t - mu
    var = jnp.mean(d * d, axis=-1, keepdims=True)
    return d * lax.rsqrt(var + LN_EPS) * g + b


def _mm_ln_body(x_ref, w_ref, res_ref, g_ref, b_ref, o_ref, ob_ref, *rest, tm, sub, packed, cast):
    if cast:
        wb_ref = rest[-1]

        @pl.when(pl.program_id(0) == 0)
        def _():
            wb_ref[...] = w_ref[...].astype(BF16)
    else:
        wb_ref = w_ref
    for s in range(tm // sub):
        rows = slice(s * sub, (s + 1) * sub)
        y = jnp.dot(x_ref[rows, :], wb_ref[...], preferred_element_type=F32)
        out = _layer_norm_rows(DEEPNORM_ALPHA * res_ref[rows, :] + y, g_ref[...], b_ref[...])
        o_ref[rows, :] = out
        ob_ref[rows, :] = out.astype(BF16)
        if packed:
            _store_token_major(rest[0], rows, _pack_bf16_pairs(out))


def matmul_post_norm(x, w, res, g, b, *, tm, sub, packed=False):
    m, k = x.shape
    d = w.shape[1]
    cast = w.dtype != BF16
    out_specs = [pl.BlockSpec((tm, d), lambda i: (i, 0)), pl.BlockSpec((tm, d), lambda i: (i, 0))]
    out_shape = [jax.ShapeDtypeStruct((m, d), F32), jax.ShapeDtypeStruct((m, d), BF16)]
    if packed:
        out_specs.append(pl.BlockSpec((tm, d // 2 // LANES, LANES), lambda i: (i, 0, 0)))
        out_shape.append(jax.ShapeDtypeStruct((m, d // 2 // LANES, LANES), jnp.uint32))
    return pl.pallas_call(
        functools.partial(_mm_ln_body, tm=tm, sub=sub, packed=packed, cast=cast),
        grid=(m // tm,),
        scratch_shapes=[pltpu.VMEM((k, d), BF16)] if cast else [],
        in_specs=[pl.BlockSpec((tm, k), lambda i: (i, 0)),
                  pl.BlockSpec((k, d), lambda i: (0, 0), pipeline_mode=pl.Buffered(1)),
                  pl.BlockSpec((tm, d), lambda i: (i, 0)),
                  pl.BlockSpec((1, d), lambda i: (0, 0)),
                  pl.BlockSpec((1, d), lambda i: (0, 0))],
        out_specs=out_specs,
        out_shape=out_shape,
        compiler_params=_cparams(("arbitrary",)),
        name="matmul_post_norm",
    )(x, w, res, g.reshape(1, d), b.reshape(1, d))


def _silu(x):
    return x / (1.0 + jnp.exp(-x))


def _swiglu_body(x_ref, wg_ref, *rest, f, tm, tf, sub):
    n_u = tf // LANES
    wu_refs, o_ref, wgb_ref, wub_ref = rest[:n_u], rest[n_u], rest[n_u + 1], rest[n_u + 2]
    j = pl.program_id(0)

    @pl.when(pl.program_id(1) == 0)
    def _():
        col = j * tf + lax.broadcasted_iota(jnp.int32, wg_ref.shape, 1)
        wgb_ref[...] = jnp.where(col < f, wg_ref[...], 0.0).astype(BF16)
        for c, wu_ref in enumerate(wu_refs):
            keep = j * tf + c * LANES < f
            wub_ref[:, c * LANES:(c + 1) * LANES] = jnp.where(keep, wu_ref[...], 0.0).astype(BF16)

    for s in range(tm // sub):
        x = x_ref[s * sub:(s + 1) * sub, :]
        g = jnp.dot(x, wgb_ref[...], preferred_element_type=F32)
        u = jnp.dot(x, wub_ref[...], preferred_element_type=F32)
        o_ref[s * sub:(s + 1) * sub, :] = (_silu(g) * u).astype(o_ref.dtype)


def swiglu_up(x, w_gu, *, tm, tf, sub):
    m, k = x.shape
    f = w_gu.shape[1] // 2
    assert f % LANES == 0 and tf % LANES == 0
    nf = pl.cdiv(f, tf)
    n_u = tf // LANES
    last_u = 2 * f // LANES - 1

    def u_map(c):
        return lambda j, i: (0, jnp.minimum(f // LANES + j * n_u + c, last_u))

    return pl.pallas_call(
        functools.partial(_swiglu_body, f=f, tm=tm, tf=tf, sub=sub),
        grid=(nf, m // tm),
        in_specs=[pl.BlockSpec((tm, k), lambda j, i: (i, 0)),
                  pl.BlockSpec((k, tf), lambda j, i: (0, j))]
        + [pl.BlockSpec((k, LANES), u_map(c)) for c in range(n_u)],
        out_specs=pl.BlockSpec((tm, tf), lambda j, i: (i, j)),
        out_shape=jax.ShapeDtypeStruct((m, f), BF16),
        scratch_shapes=[pltpu.VMEM((k, tf), BF16), pltpu.VMEM((k, tf), BF16)],
        compiler_params=_cparams(("arbitrary", "arbitrary")),
        name="swiglu_up",
    )(x, w_gu, *([w_gu] * n_u))


def _gla_body(q_ref, k_ref, v_ref, r_ref, z_ref, wgu_ref, bg_ref, gn_ref, o_ref, state_ref,
              *, chunk, group, n_groups, dk):
    @pl.when(pl.program_id(2) == 0)
    def _():
        state_ref[...] = jnp.zeros_like(state_ref)

    gt = group * chunk
    row = lax.broadcasted_iota(jnp.int32, (gt, gt), 0)
    col = lax.broadcasted_iota(jnp.int32, (gt, gt), 1)
    causal = jnp.logical_and(row // chunk == col // chunk, col <= row)
    tril = jnp.where(causal, 1.0, 0.0).astype(BF16)
    nt = (((1,), (1,)), ((), ()))
    tn = (((0,), (0,)), ((), ()))

    def body(g, carry):
        sl = pl.ds(pl.multiple_of(g * gt, gt), gt)
        gate = jnp.dot(z_ref[0, sl, :].astype(BF16), wgu_ref[0], preferred_element_type=F32) + bg_ref[0]
        log_a = (jnp.minimum(gate, 0.0) - jnp.log(1.0 + jnp.exp(-jnp.abs(gate)))) / GLA_TAU
        la_hi = log_a.astype(BF16)
        la_lo = (log_a - la_hi.astype(F32)).astype(BF16)
        b_cum = (jnp.dot(tril, la_hi, preferred_element_type=F32)
                 + jnp.dot(tril, la_lo, preferred_element_type=F32))
        b_last = [b_cum[(c + 1) * chunk - 1:(c + 1) * chunk, :] for c in range(group)]
        b_last_rows = jnp.concatenate([jnp.broadcast_to(b, (chunk, b.shape[1])) for b in b_last], axis=0)
        total = [jnp.zeros_like(b_last[0])]
        for c in range(group):
            total.append(total[c] + b_last[c])

        def per_chunk_rows(rows_of):
            return jnp.concatenate([jnp.broadcast_to(rows_of(c), (chunk, dk)) for c in range(group)], axis=0)

        q = q_ref[0, sl, :] * (dk ** -0.5)
        k = k_ref[0, sl, :]
        q_dec = q * jnp.exp(b_cum)
        q_dec_b = q_dec.astype(BF16)
        k_inv = (k * jnp.exp(-b_cum)).astype(BF16)
        k_end = k * jnp.exp(b_last_rows - b_cum)
        v = v_ref[0, sl, :].astype(BF16)
        att = lax.dot_general(q_dec_b, k_inv, nt, preferred_element_type=F32)
        att = jnp.where(causal, att, 0.0).astype(BF16)
        o = jnp.dot(att, v, preferred_element_type=F32)
        state_t = state_ref[...]
        q_grp = (q_dec * jnp.exp(per_chunk_rows(lambda c: total[c]))).astype(BF16)
        o = o + lax.dot_general(q_grp, state_t.astype(BF16), nt, preferred_element_type=F32)
        outs = [o[:chunk]]
        for c in range(1, group):
            keys = jnp.concatenate([k_end[p * chunk:(p + 1) * chunk] * jnp.exp(total[c] - total[p + 1])
                                    for p in range(c)], axis=0).astype(BF16)
            a_c = lax.dot_general(q_dec_b[c * chunk:(c + 1) * chunk], keys, nt, preferred_element_type=F32)
            outs.append(o[c * chunk:(c + 1) * chunk]
                        + jnp.dot(a_c.astype(BF16), v[:c * chunk], preferred_element_type=F32))
        o = jnp.concatenate(outs, axis=0)
        k_grp = (k_end * jnp.exp(per_chunk_rows(lambda c: total[group] - total[c + 1]))).astype(BF16)
        state_ref[...] = (jnp.exp(total[group]) * state_t
                          + lax.dot_general(v, k_grp, tn, preferred_element_type=F32))
        o = o * lax.rsqrt(jnp.mean(o * o, axis=-1, keepdims=True) + RMS_EPS) * gn_ref[...]
        o_ref[0, sl, :] = (o * _silu(r_ref[0, sl, :].astype(F32))).astype(o_ref.dtype)
        return carry

    lax.fori_loop(0, n_groups, body, 0)


def gla_core(qk, vr, z, wgu, bg, gn, *, heads, dk, dv, step):
    bsz, s, _ = qk.shape
    zr = z.shape[-1]
    k_off = heads
    r_off = heads
    return pl.pallas_call(
        functools.partial(_gla_body, chunk=GLA_CHUNK, group=GLA_GROUP,
                          n_groups=step // (GLA_GROUP * GLA_CHUNK), dk=dk),
        grid=(bsz, heads, s // step),
        in_specs=[pl.BlockSpec((1, step, dk), lambda b, h, t: (b, t, h)),
                  pl.BlockSpec((1, step, dk), lambda b, h, t: (b, t, k_off + h)),
                  pl.BlockSpec((1, step, dv), lambda b, h, t: (b, t, h)),
                  pl.BlockSpec((1, step, dv), lambda b, h, t: (b, t, r_off + h)),
                  pl.BlockSpec((1, step, zr), lambda b, h, t: (b, t, 0)),
                  pl.BlockSpec((1, zr, dk), lambda b, h, t: (h, 0, 0)),
                  pl.BlockSpec((1, 1, dk), lambda b, h, t: (h, 0, 0)),
                  pl.BlockSpec((1, dv), lambda b, h, t: (0, 0))],
        out_specs=pl.BlockSpec((1, step, dv), lambda b, h, t: (b, t, h)),
        out_shape=jax.ShapeDtypeStruct((bsz, s, heads * dv), BF16),
        scratch_shapes=[pltpu.VMEM((dv, dk), F32)],
        compiler_params=_cparams(("parallel", "parallel", "arbitrary")),
        name="gla_core",
    )(qk, qk, vr, vr, z, wgu, bg, gn)


MASK_PENALTY = -3.0e38
LOG2_E = math.log2(math.e)
SUM_ROWS = 16


def _moba_body(q_ref, k_ref, v_ref, o_ref, kmean_ref, kaug_ref, vt_ref, rhs_ref, m_ref, acc_ref,
               *, blk, n_blk, rep, hd, topk):
    i = pl.program_id(2)
    nq = rep * blk

    @pl.when(i == 0)
    def _():
        lane = lax.broadcasted_iota(jnp.int32, (blk, hd), 1)
        for j in range(n_blk):
            kj = k_ref[0, j * blk:(j + 1) * blk, :]
            kaug_ref[j, :, :hd] = kj
            kaug_ref[j, :, hd:] = jnp.where(lane == j, 1.0, 0.0).astype(BF16)
            kmean_ref[j:j + 1, :] = jnp.sum(kj.astype(F32), axis=0, keepdims=True) / blk
            vt_ref[j, :hd, :] = v_ref[0, j * blk:(j + 1) * blk, :].astype(F32).T.astype(BF16)
            vt_ref[j, hd:, :] = jnp.ones((SUM_ROWS, blk), BF16)
        rhs_ref[hd + n_blk:, :] = jnp.zeros((hd - n_blk, nq), BF16)

    qt = jnp.concatenate([q_ref[0, :, r * hd:(r + 1) * hd].astype(F32).T for r in range(rep)], axis=1)
    qt = qt.astype(BF16)
    gate = jnp.dot(kmean_ref[...].astype(BF16), qt, preferred_element_type=F32)
    blk_id = lax.broadcasted_iota(jnp.int32, (n_blk, nq), 0)
    past = blk_id < i
    gate = jnp.where(past, gate, -jnp.inf)
    beaten = jnp.zeros((n_blk, nq), jnp.int32)
    for c in range(n_blk):
        gc = gate[c:c + 1, :]
        beaten = beaten + jnp.where(gc > gate, 1, jnp.where(gc == gate, jnp.where(c < blk_id, 1, 0), 0))
    penalty = jnp.where(past, jnp.where(beaten < topk, 0.0, MASK_PENALTY),
                        jnp.where(blk_id == i, 0.0, MASK_PENALTY))
    rhs_ref[:hd, :] = qt
    rhs_ref[hd:hd + n_blk, :] = penalty.astype(BF16)

    m_ref[...] = jnp.full((1, nq), -jnp.inf, F32)
    acc_ref[...] = jnp.zeros_like(acc_ref)
    group = 4

    def attend(j0, n, own):
        keys = kaug_ref[pl.ds(j0, n)].reshape(n * blk, 2 * hd)
        s = jnp.dot(keys, rhs_ref[...], preferred_element_type=F32)
        if own:
            key = lax.broadcasted_iota(jnp.int32, (blk, nq), 0)
            qpos = lax.rem(lax.broadcasted_iota(jnp.int32, (blk, nq), 1), blk)
            tail = jnp.where(key <= qpos, s[(n - 1) * blk:], -jnp.inf)
            s = tail if n == 1 else jnp.concatenate([s[:(n - 1) * blk], tail], axis=0)
        m_old = m_ref[...]
        m_new = jnp.maximum(m_old, jnp.max(s, axis=0, keepdims=True))
        p = jnp.exp2(s - m_new).astype(BF16)
        upd = jnp.exp2(m_old - m_new) * acc_ref[...]
        for c in range(n):
            upd = upd + jnp.dot(vt_ref[j0 + c], p[c * blk:(c + 1) * blk], preferred_element_type=F32)
        acc_ref[...] = upd
        m_ref[...] = m_new

    n_last = i + 1 - (i // group) * group

    def full_group(jj, carry):
        attend(group * jj, group, False)
        return carry

    lax.fori_loop(0, i // group, full_group, 0)
    for n in range(1, group + 1):
        @pl.when(n_last == n)
        def _():
            attend(i + 1 - n, n, True)

    acc = acc_ref[...]
    out = acc[:hd] * (1.0 / acc[hd:hd + 1])
    for r in range(rep):
        o_ref[0, :, r * hd:(r + 1) * hd] = out[:, r * blk:(r + 1) * blk].T.astype(o_ref.dtype)


def moba_attention(q, kv, *, heads, kv_heads, hd, blk, topk):
    bsz, s, _ = q.shape
    rep = heads // kv_heads
    n_blk = s // blk
    nq = rep * blk
    assert n_blk <= hd and n_blk % 16 == 0
    v_off = kv_heads
    return pl.pallas_call(
        functools.partial(_moba_body, blk=blk, n_blk=n_blk, rep=rep, hd=hd, topk=topk),
        grid=(bsz, kv_heads, n_blk),
        in_specs=[pl.BlockSpec((1, blk, rep * hd), lambda b, g, i: (b, i, g)),
                  pl.BlockSpec((1, s, hd), lambda b, g, i: (b, 0, g)),
                  pl.BlockSpec((1, s, hd), lambda b, g, i: (b, 0, v_off + g))],
        out_specs=pl.BlockSpec((1, blk, rep * hd), lambda b, g, i: (b, i, g)),
        out_shape=jax.ShapeDtypeStruct((bsz, s, heads * hd), BF16),
        scratch_shapes=[pltpu.VMEM((n_blk, hd), F32),
                        pltpu.VMEM((n_blk, blk, 2 * hd), BF16),
                        pltpu.VMEM((n_blk, hd + SUM_ROWS, blk), BF16),
                        pltpu.VMEM((2 * hd, nq), BF16),
                        pltpu.VMEM((1, nq), F32),
                        pltpu.VMEM((hd + SUM_ROWS, nq), F32)],
        compiler_params=_cparams(("parallel", "parallel", "arbitrary")),
        name="moba_attention",
    )(q, kv, kv)


def _router_body(x_ref, w_ref, o_ref, *, n_experts):
    logits = jnp.dot(x_ref[...], w_ref[...], preferred_element_type=F32)
    lane = lax.broadcasted_iota(jnp.int32, logits.shape, 1)
    lane_f = lane.astype(F32)
    big = float(logits.shape[1])
    lg = jnp.where(lane < n_experts, logits, -jnp.inf)
    m1 = jnp.max(lg, axis=-1, keepdims=True)
    i1 = jnp.min(jnp.where(lg == m1, lane_f, big), axis=-1, keepdims=True)
    lg2 = jnp.where(lane_f == i1, -jnp.inf, lg)
    m2 = jnp.max(lg2, axis=-1, keepdims=True)
    i2 = jnp.min(jnp.where(lg2 == m2, lane_f, big), axis=-1, keepdims=True)
    t = jnp.exp(m2 - m1)
    g1 = 1.0 / (1.0 + t)
    g2 = t / (1.0 + t)
    o_ref[...] = jnp.where(lane == 0, i1,
                           jnp.where(lane == 1, i2, jnp.where(lane == 2, g1, jnp.where(lane == 3, g2, 0.0))))


def router_top2(x, w_pad, *, tm, n_experts):
    m, k = x.shape
    n = w_pad.shape[1]
    return pl.pallas_call(
        functools.partial(_router_body, n_experts=n_experts),
        grid=(m // tm,),
        in_specs=[pl.BlockSpec((tm, k), lambda i: (i, 0)),
                  pl.BlockSpec((k, n), lambda i: (0, 0))],
        out_specs=pl.BlockSpec((tm, n), lambda i: (i, 0)),
        out_shape=jax.ShapeDtypeStruct((m, n), F32),
        compiler_params=_cparams(("parallel",)),
        name="router_top2",
    )(x, w_pad)


def _row_copy(src_hbm, dst_vmem, sem, src_row, dst_row):
    return pltpu.make_async_copy(src_hbm.at[pl.ds(src_row, 1), :], dst_vmem.at[pl.ds(dst_row, 1), :], sem)


def _token_copy(src_hbm, dst_vmem, sem, src_row, dst_row):
    return pltpu.make_async_copy(src_hbm.at[src_row], dst_vmem.at[dst_row], sem)


def _store_token_major(ref, rows, x):
    for c in range(x.shape[1] // LANES):
        ref[rows, c, :] = x[:, c * LANES:(c + 1) * LANES]


def _load_token_major(x):
    return jnp.concatenate([x[:, c, :] for c in range(x.shape[1])], axis=1)


def _pack_bf16_pairs(x):
    half = x.shape[1] // 2
    lo = pltpu.bitcast(x[:, :half].astype(BF16).astype(F32), jnp.uint32)
    hi = pltpu.bitcast(x[:, half:].astype(BF16).astype(F32), jnp.uint32)
    return (hi & jnp.uint32(0xFFFF0000)) | (lo >> 16)


def _unpack_bf16_pairs(w):
    lo = pltpu.bitcast(w << 16, F32).astype(BF16)
    hi = pltpu.bitcast(w & jnp.uint32(0xFFFF0000), F32).astype(BF16)
    return jnp.concatenate([lo, hi], axis=1)


def _gather_body(tok_ref, base_ref, nrows_ref, hp_hbm, o_ref, buf, sem, *, tm, sub, unroll):
    i = pl.program_id(0)

    def issue_tile(t, slot):
        base = base_ref[t * (tm // sub)]

        def issue(g, c):
            for u in range(unroll):
                r = g * unroll + u
                _token_copy(hp_hbm, buf.at[slot], sem.at[slot], tok_ref[base + r], r).start()
            return c

        lax.fori_loop(0, tm // unroll, issue, 0)

    @pl.when(i == 0)
    def _():
        issue_tile(0, 0)

    nxt = i + 1

    @pl.when(jnp.logical_and(nxt < pl.num_programs(0), nxt * tm < nrows_ref[0]))
    def _():
        issue_tile(nxt, nxt % 2)

    used = i * tm < nrows_ref[0]

    @pl.when(jnp.logical_not(used))
    def _():
        o_ref[...] = jnp.zeros_like(o_ref)

    @pl.when(used)
    def _():
        slot = i % 2
        pltpu.make_async_copy(hp_hbm.at[pl.ds(0, tm)], buf.at[slot], sem.at[slot]).wait()
        o_ref[...] = _unpack_bf16_pairs(_load_token_major(buf[slot]))


def gather_rows(hp, sorted_tok, sub_base, n_used_rows, *, cap, tm, sub):
    half = hp.shape[1] * hp.shape[2]
    return pl.pallas_call(
        functools.partial(_gather_body, tm=tm, sub=sub, unroll=8),
        grid_spec=pltpu.PrefetchScalarGridSpec(
            num_scalar_prefetch=3,
            grid=(cap // tm,),
            in_specs=[pl.BlockSpec(memory_space=pl.ANY)],
            out_specs=pl.BlockSpec((tm, 2 * half), lambda i, tok, base, nused: (i, 0)),
            scratch_shapes=[pltpu.VMEM((2, tm) + hp.shape[1:], jnp.uint32), pltpu.SemaphoreType.DMA((2,))]),
        out_shape=jax.ShapeDtypeStruct((cap, 2 * half), BF16),
        compiler_params=_cparams(("arbitrary",)),
        name="gather_rows",
    )(sorted_tok, sub_base, n_used_rows, hp)


def _grouped_tile(i, n_rows, tm):
    return jnp.minimum(i, (n_rows[0] - 1) // tm)


def _grouped_fresh(i, sub_e, ratio):
    prev = sub_e[(jnp.maximum(i, 1) - 1) * ratio]
    return jnp.logical_or(i == 0, sub_e[i * ratio] != prev)


def _moe_up_body(sub_e, sub_n, n_rows, x_ref, wg_ref, wu_ref, o_ref, wgb_ref, wub_ref, *, tm, sub):
    i = pl.program_id(1)
    ratio = tm // sub
    used = i * tm < n_rows[0]

    @pl.when(jnp.logical_and(used, _grouped_fresh(i, sub_e, ratio)))
    def _():
        wgb_ref[...] = wg_ref[0].astype(BF16)
        wub_ref[...] = wu_ref[0].astype(BF16)

    def chunk(s):
        x = x_ref[s * sub:(s + 1) * sub, :]
        g = jnp.dot(x, wgb_ref[...], preferred_element_type=F32)
        u = jnp.dot(x, wub_ref[...], preferred_element_type=F32)
        o_ref[s * sub:(s + 1) * sub, :] = (_silu(g) * u).astype(o_ref.dtype)

    full = jnp.logical_and(used, sub_n[i * ratio + ratio - 1] > 0)

    @pl.when(full)
    def _():
        for s in range(ratio):
            chunk(s)

    for s in range(ratio):
        live = jnp.logical_and(used, sub_n[i * ratio + s] > 0)

        @pl.when(jnp.logical_and(live, jnp.logical_not(full)))
        def _():
            chunk(s)

        @pl.when(jnp.logical_not(live))
        def _():
            o_ref[s * sub:(s + 1) * sub, :] = jnp.zeros((sub, o_ref.shape[1]), o_ref.dtype)


def moe_swiglu_up(xs, w_gu, sub_e, sub_n, n_rows, *, tm, tf, sub):
    cap, d = xs.shape
    f = w_gu.shape[2] // 2
    nf = f // tf
    ratio = tm // sub

    def x_map(j, i, se, sn, nr):
        return (_grouped_tile(i, nr, tm), 0)

    def wg_map(j, i, se, sn, nr):
        return (se[_grouped_tile(i, nr, tm) * ratio], 0, j)

    def wu_map(j, i, se, sn, nr):
        return (se[_grouped_tile(i, nr, tm) * ratio], 0, nf + j)

    def o_map(j, i, se, sn, nr):
        return (i, j)

    return pl.pallas_call(
        functools.partial(_moe_up_body, tm=tm, sub=sub),
        grid_spec=pltpu.PrefetchScalarGridSpec(
            num_scalar_prefetch=3,
            grid=(nf, cap // tm),
            in_specs=[pl.BlockSpec((tm, d), x_map),
                      pl.BlockSpec((1, d, tf), wg_map),
                      pl.BlockSpec((1, d, tf), wu_map)],
            out_specs=pl.BlockSpec((tm, tf), o_map),
            scratch_shapes=[pltpu.VMEM((d, tf), BF16), pltpu.VMEM((d, tf), BF16)]),
        out_shape=jax.ShapeDtypeStruct((cap, f), BF16),
        compiler_params=_cparams(("arbitrary", "arbitrary")),
        name="moe_swiglu_up",
    )(sub_e, sub_n, n_rows, xs, w_gu, w_gu)


def _moe_down_body(sub_e, sub_n, n_rows, x_ref, w_ref, o_ref, wb_ref, *, tm, sub):
    i = pl.program_id(1)
    ratio = tm // sub
    used = i * tm < n_rows[0]

    @pl.when(jnp.logical_and(used, _grouped_fresh(i, sub_e, ratio)))
    def _():
        wb_ref[...] = w_ref[0].astype(BF16)

    def chunk(s):
        o_ref[s * sub:(s + 1) * sub, :] = jnp.dot(
            x_ref[s * sub:(s + 1) * sub, :], wb_ref[...], preferred_element_type=F32)

    full = jnp.logical_and(used, sub_n[i * ratio + ratio - 1] > 0)

    @pl.when(full)
    def _():
        for s in range(ratio):
            chunk(s)

    for s in range(ratio):
        live = jnp.logical_and(used, sub_n[i * ratio + s] > 0)

        @pl.when(jnp.logical_and(live, jnp.logical_not(full)))
        def _():
            chunk(s)

        @pl.when(jnp.logical_not(live))
        def _():
            o_ref[s * sub:(s + 1) * sub, :] = jnp.zeros((sub, o_ref.shape[1]), o_ref.dtype)


def moe_down(act, w_down, sub_e, sub_n, n_rows, *, tm, tn, sub):
    cap, f = act.shape
    d = w_down.shape[2]
    ratio = tm // sub

    def x_map(j, i, se, sn, nr):
        return (_grouped_tile(i, nr, tm), 0)

    def w_map(j, i, se, sn, nr):
        return (se[_grouped_tile(i, nr, tm) * ratio], 0, j)

    def o_map(j, i, se, sn, nr):
        return (i, j)

    return pl.pallas_call(
        functools.partial(_moe_down_body, tm=tm, sub=sub),
        grid_spec=pltpu.PrefetchScalarGridSpec(
            num_scalar_prefetch=3,
            grid=(d // tn, cap // tm),
            in_specs=[pl.BlockSpec((tm, f), x_map),
                      pl.BlockSpec((1, f, tn), w_map)],
            out_specs=pl.BlockSpec((tm, tn), o_map),
            scratch_shapes=[pltpu.VMEM((f, tn), BF16)]),
        out_shape=jax.ShapeDtypeStruct((cap, d), F32),
        compiler_params=_cparams(("arbitrary", "arbitrary")),
        name="moe_down",
    )(sub_e, sub_n, n_rows, act, w_down)


def _combine_body(dest_ref, y_hbm, gates_ref, res_ref, g_ref, b_ref, o_ref, buf, sem, *, tm, unroll):
    i = pl.program_id(0)

    def issue_tile(t, slot):
        def issue(g, c):
            for u in range(unroll):
                r = g * unroll + u
                for k in range(TOP_K):
                    _row_copy(y_hbm, buf.at[slot, k], sem.at[slot, k],
                              dest_ref[TOP_K * (t * tm + r) + k], r).start()
            return c

        lax.fori_loop(0, tm // unroll, issue, 0)

    @pl.when(i == 0)
    def _():
        issue_tile(0, 0)

    nxt = i + 1

    @pl.when(nxt < pl.num_programs(0))
    def _():
        issue_tile(nxt, nxt % 2)

    slot = i % 2
    gates = gates_ref[...]
    y = None
    for k in range(TOP_K):
        pltpu.make_async_copy(y_hbm.at[pl.ds(0, tm), :], buf.at[slot, k], sem.at[slot, k]).wait()
        term = buf[slot, k] * gates[:, TOP_K + k:TOP_K + k + 1]
        y = term if y is None else y + term
    o_ref[...] = _layer_norm_rows(DEEPNORM_ALPHA * res_ref[...] + y, g_ref[...], b_ref[...])


def moe_combine_post_norm(y_rows, dest, routing, res, g, b, *, tm):
    n_tok, d = res.shape
    lanes = routing.shape[1]
    return pl.pallas_call(
        functools.partial(_combine_body, tm=tm, unroll=4),
        grid_spec=pltpu.PrefetchScalarGridSpec(
            num_scalar_prefetch=1,
            grid=(n_tok // tm,),
            in_specs=[pl.BlockSpec(memory_space=pl.ANY),
                      pl.BlockSpec((tm, lanes), lambda i, dst: (i, 0)),
                      pl.BlockSpec((tm, d), lambda i, dst: (i, 0)),
                      pl.BlockSpec((1, d), lambda i, dst: (0, 0)),
                      pl.BlockSpec((1, d), lambda i, dst: (0, 0))],
            out_specs=pl.BlockSpec((tm, d), lambda i, dst: (i, 0)),
            scratch_shapes=[pltpu.VMEM((2, TOP_K, tm, d), F32), pltpu.SemaphoreType.DMA((2, TOP_K))]),
        out_shape=jax.ShapeDtypeStruct((n_tok, d), F32),
        compiler_params=_cparams(("arbitrary",)),
        name="moe_combine_post_norm",
    )(dest, y_rows, routing, res, g.reshape(1, d), b.reshape(1, d))


def _group_rows(seg, n_seg, tile, sub):
    n_asg = seg.shape[0]
    cap = n_asg + n_seg * tile
    onehot = (seg[:, None] == jnp.arange(n_seg, dtype=jnp.int32)[None, :]).astype(jnp.int32)
    csum = jnp.cumsum(onehot, axis=0)
    rank = jnp.sum(csum * onehot, axis=1) - 1
    counts = csum[-1]
    padded = (counts + tile - 1) // tile * tile
    pad_end = jnp.cumsum(padded)
    pad_start = pad_end - padded
    dest = (pad_start[seg] + rank).astype(jnp.int32)
    order = jnp.sort(seg * n_asg + jnp.arange(n_asg, dtype=jnp.int32))
    sorted_tok = jnp.pad((order % n_asg) // TOP_K, (0, tile)).astype(jnp.int32)
    sub_start = jnp.arange(cap // sub, dtype=jnp.int32) * sub
    sub_e = jnp.sum((pad_end[None, :] <= sub_start[:, None]).astype(jnp.int32), axis=1)
    sub_e = jnp.minimum(sub_e, n_seg - 1)
    sub_n = jnp.clip(counts[sub_e] - (sub_start - pad_start[sub_e]), 0, sub).astype(jnp.int32)
    start = jnp.cumsum(counts) - counts
    sub_base = (sub_start - (pad_start - start)[sub_e]).astype(jnp.int32)
    n_rows = pad_end[-1:].astype(jnp.int32)
    return dest, sorted_tok, sub_base, sub_e, sub_n, n_rows, cap


def _gla_layer(h, hb, w_in, w_gate_up, b_gate, g_norm, w_o, ln_g, ln_b):
    bsz, s, d = h.shape
    dk = (d // 2) // GLA_HEADS
    dv = d // GLA_HEADS
    n_main = 2 * GLA_HEADS * dk + 2 * GLA_HEADS * dv
    x2 = hb.reshape(bsz * s, d)
    n_qk = 2 * GLA_HEADS * dk
    tm = _pick(bsz * s, (1024, 512, 256, 128, 64))
    tn = _pick(n_qk, (1024, 512, 256, 128))
    qk = matmul(x2, w_in, tm=tm, tn=tn, out_dtype=F32, col_start=0, n_cols=n_qk)
    vr = matmul(x2, w_in, tm=tm, tn=tn, out_dtype=F32, col_start=n_qk, n_cols=n_main - n_qk)
    w_z = jnp.pad(w_in[:, n_main:], ((0, 0), (0, LANES - GLA_GATE_RANK)))
    z = matmul(x2, w_z, tm=tm, tn=LANES, out_dtype=F32)
    wgu = jnp.pad(w_gate_up, ((0, LANES - GLA_GATE_RANK), (0, 0))).astype(BF16)
    wgu = wgu.reshape(LANES, GLA_HEADS, dk).transpose(1, 0, 2)
    og = gla_core(qk.reshape(bsz, s, n_qk), vr.reshape(bsz, s, n_main - n_qk), z.reshape(bsz, s, LANES), wgu,
                  b_gate.reshape(GLA_HEADS, 1, dk), g_norm.reshape(1, dv),
                  heads=GLA_HEADS, dk=dk, dv=dv, step=min(GLA_STEP_TOKENS, s))
    out, outb = matmul_post_norm(og.reshape(bsz * s, d), w_o, h.reshape(bsz * s, d), ln_g, ln_b,
                                 tm=512, sub=256)
    return out.reshape(bsz, s, d), outb.reshape(bsz, s, d)


def _moba_layer(h, hb, w_q, w_kv, w_o, ln_g, ln_b):
    bsz, s, d = h.shape
    hd = d // MOBA_HEADS
    x2 = hb.reshape(bsz * s, d)
    tm = _pick(bsz * s, (1024, 512, 256, 128, 64))
    n_kv = w_kv.shape[1]
    q = matmul(x2, w_q, tm=tm, tn=_pick(d, (1024, 512, 256, 128)), out_dtype=BF16, scale=hd ** -0.5 * LOG2_E)
    kv = matmul(x2, w_kv, tm=tm, tn=_pick(n_kv, (1024, 512, 256, 128)), out_dtype=BF16)
    att = moba_attention(q.reshape(bsz, s, d), kv.reshape(bsz, s, n_kv), heads=MOBA_HEADS,
                         kv_heads=MOBA_KV_HEADS, hd=hd, blk=MOBA_BLOCK, topk=MOBA_TOPK)
    out, outb, outp = matmul_post_norm(att.reshape(bsz * s, d), w_o, h.reshape(bsz * s, d),
                                       ln_g, ln_b, tm=512, sub=256, packed=True)
    return out.reshape(bsz, s, d), outb.reshape(bsz, s, d), outp


def _dense_ffn(h, hb, w_gu, w_down, ln_g, ln_b):
    bsz, s, d = h.shape
    f = w_down.shape[0]
    tm = _pick(bsz * s, (1024, 512, 256, 128, 64))
    act = swiglu_up(hb.reshape(bsz * s, d), w_gu, tm=tm, tf=512, sub=tm)
    out, outb = matmul_post_norm(act, w_down.astype(BF16), h.reshape(bsz * s, d), ln_g, ln_b, tm=256, sub=128)
    return out.reshape(bsz, s, d), outb.reshape(bsz, s, d)


def _moe_ffn(h, hb, hp, w_router, w_gu, w_down, ln_g, ln_b):
    bsz, s, d = h.shape
    n_tok = bsz * s
    h2 = h.reshape(n_tok, d)
    w_r = jnp.pad(w_router, ((0, 0), (0, LANES - N_EXPERTS))).astype(BF16)
    routing = router_top2(hb.reshape(n_tok, d), w_r, tm=_pick(n_tok, (1024, 512, 256, 128, 64)),
                          n_experts=N_EXPERTS)
    seg = routing[:, :TOP_K].astype(jnp.int32).reshape(-1)
    tile = min(MOE_TILE, n_tok)
    sub = min(MOE_SUB, tile)
    dest, sorted_tok, sub_base, sub_e, sub_n, n_rows, cap = _group_rows(seg, N_EXPERTS, tile, sub)
    xs = gather_rows(hp, sorted_tok, sub_base, n_rows, cap=cap, tm=min(GATHER_TILE, tile), sub=sub)
    f = w_down.shape[1]
    act = moe_swiglu_up(xs, w_gu, sub_e, sub_n, n_rows, tm=tile, tf=_pick(f, (512, 256, 128)), sub=sub)
    y_rows = moe_down(act, w_down, sub_e, sub_n, n_rows, tm=min(MOE_DOWN_TILE, tile),
                      tn=_pick(d, (512, 256, 128)), sub=sub)
    out = moe_combine_post_norm(y_rows, dest, routing, h2, ln_g, ln_b, tm=_pick(n_tok, (256, 128, 64)))
    return out.reshape(bsz, s, d)


def kernel(x, w_in_a, w_gate_up_a, b_gate_a, g_norm_a, w_o_a, w_kv_shared, w_q_b, w_o_b,
           ln_mix_g, ln_mix_b, w_gu_dense, w_down_dense, w_router, w_gu_moe, w_down_moe,
           ln_ffn_g, ln_ffn_b):
    h, hb = _gla_layer(x, x.astype(BF16), w_in_a[0], w_gate_up_a[0], b_gate_a[0], g_norm_a[0], w_o_a[0],
                       ln_mix_g[0], ln_mix_b[0])
    h, hb = _dense_ffn(h, hb, w_gu_dense[0], w_down_dense[0], ln_ffn_g[0], ln_ffn_b[0])
    h, hb, hp = _moba_layer(h, hb, w_q_b[0], w_kv_shared, w_o_b[0], ln_mix_g[1], ln_mix_b[1])
    return _moe_ffn(h, hb, hp, w_router[0], w_gu_moe[0], w_down_moe[0], ln_ffn_g[1], ln_ffn_b[1])
```

```python
import functools
import math

import jax
import jax.numpy as jnp
from jax import lax
from jax.experimental import pallas as pl
from jax.experimental.pallas import tpu as pltpu

F32 = jnp.float32
BF16 = jnp.bfloat16

DEPTH = 2
GLA_HEADS = 4
GLA_GATE_RANK = 16
GLA_TAU = 16.0
GLA_CHUNK = 64
MOBA_HEADS = 16
MOBA_KV_HEADS = 4
MOBA_BLOCK = 256
MOBA_TOPK = 3
N_EXPERTS = 8
TOP_K = 2
LN_EPS = 1e-5
RMS_EPS = 1e-6
DEEPNORM_ALPHA = (2 * DEPTH) ** 0.25

LANES = 128
VMEM_LIMIT_BYTES = 56 * 1024 * 1024

GLA_GROUP = 4
GLA_STEP_TOKENS = 1024
MOE_TILE = 1024
MOE_SUB = 256
MOE_DOWN_TILE = 512
GATHER_TILE = 1024


def _cparams(sem):
    return pltpu.CompilerParams(dimension_semantics=sem, vmem_limit_bytes=VMEM_LIMIT_BYTES)


def _pick(n, prefs):
    for p in prefs:
        if n % p == 0:
            return p
    return n


def _mm_body(x_ref, w_ref, o_ref, wb_ref, *, scale):
    @pl.when(pl.program_id(1) == 0)
    def _():
        wb_ref[...] = w_ref[...].astype(BF16)

    acc = jnp.dot(x_ref[...], wb_ref[...], preferred_element_type=F32)
    if scale != 1.0:
        acc = acc * scale
    o_ref[...] = acc.astype(o_ref.dtype)


def matmul(x, w, *, tm, tn, out_dtype, col_start=0, n_cols=None, scale=1.0):
    m, k = x.shape
    n_cols = w.shape[1] - col_start if n_cols is None else n_cols
    assert col_start % tn == 0 and n_cols % tn == 0
    first = col_start // tn
    return pl.pallas_call(
        functools.partial(_mm_body, scale=scale),
        grid=(n_cols // tn, m // tm),
        in_specs=[pl.BlockSpec((tm, k), lambda j, i: (i, 0)),
                  pl.BlockSpec((k, tn), lambda j, i: (0, first + j))],
        out_specs=pl.BlockSpec((tm, tn), lambda j, i: (i, j)),
        out_shape=jax.ShapeDtypeStruct((m, n_cols), out_dtype),
        scratch_shapes=[pltpu.VMEM((k, tn), BF16)],
        compiler_params=_cparams(("arbitrary", "arbitrary")),
        name="matmul",
    )(x, w)


def _layer_norm_rows(t, g, b):
    mu = jnp.mean(t, axis=-1, keepdims=True)
    d = t - mu
    var = jnp.mean(d * d, axis=-1, keepdims=True)
    return d * lax.rsqrt(var + LN_EPS) * g + b


def _mm_ln_body(x_ref, w_ref, res_ref, g_ref, b_ref, o_ref, ob_ref, *rest, tm, sub, packed, cast):
    if cast:
        wb_ref = rest[-1]

        @pl.when(pl.program_id(0) == 0)
        def _():
            wb_ref[...] = w_ref[...].astype(BF16)
    else:
        wb_ref = w_ref
    for s in range(tm // sub):
        rows = slice(s * sub, (s + 1) * sub)
        y = jnp.dot(x_ref[rows, :], wb_ref[...], preferred_element_type=F32)
        out = _layer_norm_rows(DEEPNORM_ALPHA * res_ref[rows, :] + y, g_ref[...], b_ref[...])
        o_ref[rows, :] = out
        ob_ref[rows, :] = out.astype(BF16)
        if packed:
            _store_token_major(rest[0], rows, _pack_bf16_pairs(out))


def matmul_post_norm(x, w, res, g, b, *, tm, sub, packed=False):
    m, k = x.shape
    d = w.shape[1]
    cast = w.dtype != BF16
    out_specs = [pl.BlockSpec((tm, d), lambda i: (i, 0)), pl.BlockSpec((tm, d), lambda i: (i, 0))]
    out_shape = [jax.ShapeDtypeStruct((m, d), F32), jax.ShapeDtypeStruct((m, d), BF16)]
    if packed:
        out_specs.append(pl.BlockSpec((tm, d // 2 // LANES, LANES), lambda i: (i, 0, 0)))
        out_shape.append(jax.ShapeDtypeStruct((m, d // 2 // LANES, LANES), jnp.uint32))
    return pl.pallas_call(
        functools.partial(_mm_ln_body, tm=tm, sub=sub, packed=packed, cast=cast),
        grid=(m // tm,),
        scratch_shapes=[pltpu.VMEM((k, d), BF16)] if cast else [],
        in_specs=[pl.BlockSpec((tm, k), lambda i: (i, 0)),
                  pl.BlockSpec((k, d), lambda i: (0, 0), pipeline_mode=pl.Buffered(1)),
                  pl.BlockSpec((tm, d), lambda i: (i, 0)),
                  pl.BlockSpec((1, d), lambda i: (0, 0)),
                  pl.BlockSpec((1, d), lambda i: (0, 0))],
        out_specs=out_specs,
        out_shape=out_shape,
        compiler_params=_cparams(("arbitrary",)),
        name="matmul_post_norm",
    )(x, w, res, g.reshape(1, d), b.reshape(1, d))


def _silu(x):
    return x / (1.0 + jnp.exp(-x))


def _swiglu_body(x_ref, wg_ref, *rest, f, tm, tf, sub):
    n_u = tf // LANES
    wu_refs, o_ref, wgb_ref, wub_ref = rest[:n_u], rest[n_u], rest[n_u + 1], rest[n_u + 2]
    j = pl.program_id(0)

    @pl.when(pl.program_id(1) == 0)
    def _():
        col = j * tf + lax.broadcasted_iota(jnp.int32, wg_ref.shape, 1)
        wgb_ref[...] = jnp.where(col < f, wg_ref[...], 0.0).astype(BF16)
        for c, wu_ref in enumerate(wu_refs):
            keep = j * tf + c * LANES < f
            wub_ref[:, c * LANES:(c + 1) * LANES] = jnp.where(keep, wu_ref[...], 0.0).astype(BF16)

    for s in range(tm // sub):
        x = x_ref[s * sub:(s + 1) * sub, :]
        g = jnp.dot(x, wgb_ref[...], preferred_element_type=F32)
        u = jnp.dot(x, wub_ref[...], preferred_element_type=F32)
        o_ref[s * sub:(s + 1) * sub, :] = (_silu(g) * u).astype(o_ref.dtype)


def swiglu_up(x, w_gu, *, tm, tf, sub):
    m, k = x.shape
    f = w_gu.shape[1] // 2
    assert f % LANES == 0 and tf % LANES == 0
    nf = pl.cdiv(f, tf)
    n_u = tf // LANES
    last_u = 2 * f // LANES - 1

    def u_map(c):
        return lambda j, i: (0, jnp.minimum(f // LANES + j * n_u + c, last_u))

    return pl.pallas_call(
        functools.partial(_swiglu_body, f=f, tm=tm, tf=tf, sub=sub),
        grid=(nf, m // tm),
        in_specs=[pl.BlockSpec((tm, k), lambda j, i: (i, 0)),
                  pl.BlockSpec((k, tf), lambda j, i: (0, j))]
        + [pl.BlockSpec((k, LANES), u_map(c)) for c in range(n_u)],
        out_specs=pl.BlockSpec((tm, tf), lambda j, i: (i, j)),
        out_shape=jax.ShapeDtypeStruct((m, f), BF16),
        scratch_shapes=[pltpu.VMEM((k, tf), BF16), pltpu.VMEM((k, tf), BF16)],
        compiler_params=_cparams(("arbitrary", "arbitrary")),
        name="swiglu_up",
    )(x, w_gu, *([w_gu] * n_u))


def _gla_body(q_ref, k_ref, v_ref, r_ref, z_ref, wgu_ref, bg_ref, gn_ref, o_ref, state_ref,
              *, chunk, group, n_groups, dk):
    @pl.when(pl.program_id(2) == 0)
    def _():
        state_ref[...] = jnp.zeros_like(state_ref)

    gt = group * chunk
    row = lax.broadcasted_iota(jnp.int32, (gt, gt), 0)
    col = lax.broadcasted_iota(jnp.int32, (gt, gt), 1)
    causal = jnp.logical_and(row // chunk == col // chunk, col <= row)
    tril = jnp.where(causal, 1.0, 0.0).astype(BF16)
    nt = (((1,), (1,)), ((), ()))
    tn = (((0,), (0,)), ((), ()))

    def body(g, carry):
        sl = pl.ds(pl.multiple_of(g * gt, gt), gt)
        gate = jnp.dot(z_ref[0, sl, :].astype(BF16), wgu_ref[0], preferred_element_type=F32) + bg_ref[0]
        log_a = (jnp.minimum(gate, 0.0) - jnp.log(1.0 + jnp.exp(-jnp.abs(gate)))) / GLA_TAU
        la_hi = log_a.astype(BF16)
        la_lo = (log_a - la_hi.astype(F32)).astype(BF16)
        b_cum = (jnp.dot(tril, la_hi, preferred_element_type=F32)
                 + jnp.dot(tril, la_lo, preferred_element_type=F32))
        b_last = [b_cum[(c + 1) * chunk - 1:(c + 1) * chunk, :] for c in range(group)]
        b_last_rows = jnp.concatenate([jnp.broadcast_to(b, (chunk, b.shape[1])) for b in b_last], axis=0)
        total = [jnp.zeros_like(b_last[0])]
        for c in range(group):
            total.append(total[c] + b_last[c])

        def per_chunk_rows(rows_of):
            return jnp.concatenate([jnp.broadcast_to(rows_of(c), (chunk, dk)) for c in range(group)], axis=0)

        q = q_ref[0, sl, :] * (dk ** -0.5)
        k = k_ref[0, sl, :]
        q_dec = q * jnp.exp(b_cum)
        q_dec_b = q_dec.astype(BF16)
        k_inv = (k * jnp.exp(-b_cum)).astype(BF16)
        k_end = k * jnp.exp(b_last_rows - b_cum)
        v = v_ref[0, sl, :].astype(BF16)
        att = lax.dot_general(q_dec_b, k_inv, nt, preferred_element_type=F32)
        att = jnp.where(causal, att, 0.0).astype(BF16)
        o = jnp.dot(att, v, preferred_element_type=F32)
        state_t = state_ref[...]
        q_grp = (q_dec * jnp.exp(per_chunk_rows(lambda c: total[c]))).astype(BF16)
        o = o + lax.dot_general(q_grp, state_t.astype(BF16), nt, preferred_element_type=F32)
        outs = [o[:chunk]]
        for c in range(1, group):
            keys = jnp.concatenate([k_end[p * chunk:(p + 1) * chunk] * jnp.exp(total[c] - total[p + 1])
                                    for p in range(c)], axis=0).astype(BF16)
            a_c = lax.dot_general(q_dec_b[c * chunk:(c + 1) * chunk], keys, nt, preferred_element_type=F32)
            outs.append(o[c * chunk:(c + 1) * chunk]
                        + jnp.dot(a_c.astype(BF16), v[:c * chunk], preferred_element_type=F32))
        o = jnp.concatenate(outs, axis=0)
        k_grp = (k_end * jnp.exp(per_chunk_rows(lambda c: total[group] - total[c + 1]))).astype(BF16)
        state_ref[...] = (jnp.exp(total[group]) * state_t
                          + lax.dot_general(v, k_grp, tn, preferred_element_type=F32))
        o = o * lax.rsqrt(jnp.mean(o * o, axis=-1, keepdims=True) + RMS_EPS) * gn_ref[...]
        o_ref[0, sl, :] = (o * _silu(r_ref[0, sl, :].astype(F32))).astype(o_ref.dtype)
        return carry

    lax.fori_loop(0, n_groups, body, 0)


def gla_core(qk, vr, z, wgu, bg, gn, *, heads, dk, dv, step):
    bsz, s, _ = qk.shape
    zr = z.shape[-1]
    k_off = heads
    r_off = heads
    return pl.pallas_call(
        functools.partial(_gla_body, chunk=GLA_CHUNK, group=GLA_GROUP,
                          n_groups=step // (GLA_GROUP * GLA_CHUNK), dk=dk),
        grid=(bsz, heads, s // step),
        in_specs=[pl.BlockSpec((1, step, dk), lambda b, h, t: (b, t, h)),
                  pl.BlockSpec((1, step, dk), lambda b, h, t: (b, t, k_off + h)),
                  pl.BlockSpec((1, step, dv), lambda b, h, t: (b, t, h)),
                  pl.BlockSpec((1, step, dv), lambda b, h, t: (b, t, r_off + h)),
                  pl.BlockSpec((1, step, zr), lambda b, h, t: (b, t, 0)),
                  pl.BlockSpec((1, zr, dk), lambda b, h, t: (h, 0, 0)),
                  pl.BlockSpec((1, 1, dk), lambda b, h, t: (h, 0, 0)),
                  pl.BlockSpec((1, dv), lambda b, h, t: (0, 0))],
        out_specs=pl.BlockSpec((1, step, dv), lambda b, h, t: (b, t, h)),
        out_shape=jax.ShapeDtypeStruct((bsz, s, heads * dv), BF16),
        scratch_shapes=[pltpu.VMEM((dv, dk), F32)],
        compiler_params=_cparams(("parallel", "parallel", "arbitrary")),
        name="gla_core",
    )(qk, qk, vr, vr, z, wgu, bg, gn)


MASK_PENALTY = -3.0e38
LOG2_E = math.log2(math.e)
SUM_ROWS = 16


def _moba_body(q_ref, k_ref, v_ref, o_ref, kmean_ref, kaug_ref, vt_ref, rhs_ref, m_ref, acc_ref,
               *, blk, n_blk, rep, hd, topk):
    i = pl.program_id(2)
    nq = rep * blk

    @pl.when(i == 0)
    def _():
        lane = lax.broadcasted_iota(jnp.int32, (blk, hd), 1)
        for j in range(n_blk):
            kj = k_ref[0, j * blk:(j + 1) * blk, :]
            kaug_ref[j, :, :hd] = kj
            kaug_ref[j, :, hd:] = jnp.where(lane == j, 1.0, 0.0).astype(BF16)
            kmean_ref[j:j + 1, :] = jnp.sum(kj.astype(F32), axis=0, keepdims=True) / blk
            vt_ref[j, :hd, :] = v_ref[0, j * blk:(j + 1) * blk, :].astype(F32).T.astype(BF16)
            vt_ref[j, hd:, :] = jnp.ones((SUM_ROWS, blk), BF16)
        rhs_ref[hd + n_blk:, :] = jnp.zeros((hd - n_blk, nq), BF16)

    qt = jnp.concatenate([q_ref[0, :, r * hd:(r + 1) * hd].astype(F32).T for r in range(rep)], axis=1)
    qt = qt.astype(BF16)
    gate = jnp.dot(kmean_ref[...].astype(BF16), qt, preferred_element_type=F32)
    blk_id = lax.broadcasted_iota(jnp.int32, (n_blk, nq), 0)
    past = blk_id < i
    gate = jnp.where(past, gate, -jnp.inf)
    beaten = jnp.zeros((n_blk, nq), jnp.int32)
    for c in range(n_blk):
        gc = gate[c:c + 1, :]
        beaten = beaten + jnp.where(gc > gate, 1, jnp.where(gc == gate, jnp.where(c < blk_id, 1, 0), 0))
    penalty = jnp.where(past, jnp.where(beaten < topk, 0.0, MASK_PENALTY),
                        jnp.where(blk_id == i, 0.0, MASK_PENALTY))
    rhs_ref[:hd, :] = qt
    rhs_ref[hd:hd + n_blk, :] = penalty.astype(BF16)

    m_ref[...] = jnp.full((1, nq), -jnp.inf, F32)
    acc_ref[...] = jnp.zeros_like(acc_ref)
    group = 4

    def attend(j0, n, own):
        keys = kaug_ref[pl.ds(j0, n)].reshape(n * blk, 2 * hd)
        s = jnp.dot(keys, rhs_ref[...], preferred_element_type=F32)
        if own:
            key = lax.broadcasted_iota(jnp.int32, (blk, nq), 0)
            qpos = lax.rem(lax.broadcasted_iota(jnp.int32, (blk, nq), 1), blk)
            tail = jnp.where(key <= qpos, s[(n - 1) * blk:], -jnp.inf)
            s = tail if n == 1 else jnp.concatenate([s[:(n - 1) * blk], tail], axis=0)
        m_old = m_ref[...]
        m_new = jnp.maximum(m_old, jnp.max(s, axis=0, keepdims=True))
        p = jnp.exp2(s - m_new).astype(BF16)
        upd = jnp.exp2(m_old - m_new) * acc_ref[...]
        for c in range(n):
            upd = upd + jnp.dot(vt_ref[j0 + c], p[c * blk:(c + 1) * blk], preferred_element_type=F32)
        acc_ref[...] = upd
        m_ref[...] = m_new

    n_last = i + 1 - (i // group) * group

    def full_group(jj, carry):
        attend(group * jj, group, False)
        return carry

    lax.fori_loop(0, i // group, full_group, 0)
    for n in range(1, group + 1):
        @pl.when(n_last == n)
        def _():
            attend(i + 1 - n, n, True)

    acc = acc_ref[...]
    out = acc[:hd] * (1.0 / acc[hd:hd + 1])
    for r in range(rep):
        o_ref[0, :, r * hd:(r + 1) * hd] = out[:, r * blk:(r + 1) * blk].T.astype(o_ref.dtype)


def moba_attention(q, kv, *, heads, kv_heads, hd, blk, topk):
    bsz, s, _ = q.shape
    rep = heads // kv_heads
    n_blk = s // blk
    nq = rep * blk
    assert n_blk <= hd and n_blk % 16 == 0
    v_off = kv_heads
    return pl.pallas_call(
        functools.partial(_moba_body, blk=blk, n_blk=n_blk, rep=rep, hd=hd, topk=topk),
        grid=(bsz, kv_heads, n_blk),
        in_specs=[pl.BlockSpec((1, blk, rep * hd), lambda b, g, i: (b, i, g)),
                  pl.BlockSpec((1, s, hd), lambda b, g, i: (b, 0, g)),
                  pl.BlockSpec((1, s, hd), lambda b, g, i: (b, 0, v_off + g))],
        out_specs=pl.BlockSpec((1, blk, rep * hd), lambda b, g, i: (b, i, g)),
        out_shape=jax.ShapeDtypeStruct((bsz, s, heads * hd), BF16),
        scratch_shapes=[pltpu.VMEM((n_blk, hd), F32),
                        pltpu.VMEM((n_blk, blk, 2 * hd), BF16),
                        pltpu.VMEM((n_blk, hd + SUM_ROWS, blk), BF16),
                        pltpu.VMEM((2 * hd, nq), BF16),
                        pltpu.VMEM((1, nq), F32),
                        pltpu.VMEM((hd + SUM_ROWS, nq), F32)],
        compiler_params=_cparams(("parallel", "parallel", "arbitrary")),
        name="moba_attention",
    )(q, kv, kv)


def _router_body(x_ref, w_ref, o_ref, *, n_experts):
    logits = jnp.dot(x_ref[...], w_ref[...], preferred_element_type=F32)
    lane = lax.broadcasted_iota(jnp.int32, logits.shape, 1)
    lane_f = lane.astype(F32)
    big = float(logits.shape[1])
    lg = jnp.where(lane < n_experts, logits, -jnp.inf)
    m1 = jnp.max(lg, axis=-1, keepdims=True)
    i1 = jnp.min(jnp.where(lg == m1, lane_f, big), axis=-1, keepdims=True)
    lg2 = jnp.where(lane_f == i1, -jnp.inf, lg)
    m2 = jnp.max(lg2, axis=-1, keepdims=True)
    i2 = jnp.min(jnp.where(lg2 == m2, lane_f, big), axis=-1, keepdims=True)
    t = jnp.exp(m2 - m1)
    g1 = 1.0 / (1.0 + t)
    g2 = t / (1.0 + t)
    o_ref[...] = jnp.where(lane == 0, i1,
                           jnp.where(lane == 1, i2, jnp.where(lane == 2, g1, jnp.where(lane == 3, g2, 0.0))))


def router_top2(x, w_pad, *, tm, n_experts):
    m, k = x.shape
    n = w_pad.shape[1]
    return pl.pallas_call(
        functools.partial(_router_body, n_experts=n_experts),
        grid=(m // tm,),
        in_specs=[pl.BlockSpec((tm, k), lambda i: (i, 0)),
                  pl.BlockSpec((k, n), lambda i: (0, 0))],
        out_specs=pl.BlockSpec((tm, n), lambda i: (i, 0)),
        out_shape=jax.ShapeDtypeStruct((m, n), F32),
        compiler_params=_cparams(("parallel",)),
        name="router_top2",
    )(x, w_pad)


def _row_copy(src_hbm, dst_vmem, sem, src_row, dst_row):
    return pltpu.make_async_copy(src_hbm.at[pl.ds(src_row, 1), :], dst_vmem.at[pl.ds(dst_row, 1), :], sem)


def _token_copy(src_hbm, dst_vmem, sem, src_row, dst_row):
    return pltpu.make_async_copy(src_hbm.at[src_row], dst_vmem.at[dst_row], sem)


def _store_token_major(ref, rows, x):
    for c in range(x.shape[1] // LANES):
        ref[rows, c, :] = x[:, c * LANES:(c + 1) * LANES]


def _load_token_major(x):
    return jnp.concatenate([x[:, c, :] for c in range(x.shape[1])], axis=1)


def _pack_bf16_pairs(x):
    half = x.shape[1] // 2
    lo = pltpu.bitcast(x[:, :half].astype(BF16).astype(F32), jnp.uint32)
    hi = pltpu.bitcast(x[:, half:].astype(BF16).astype(F32), jnp.uint32)
    return (hi & jnp.uint32(0xFFFF0000)) | (lo >> 16)


def _unpack_bf16_pairs(w):
    lo = pltpu.bitcast(w << 16, F32).astype(BF16)
    hi = pltpu.bitcast(w & jnp.uint32(0xFFFF0000), F32).astype(BF16)
    return jnp.concatenate([lo, hi], axis=1)


def _gather_body(tok_ref, base_ref, nrows_ref, hp_hbm, o_ref, buf, sem, *, tm, sub, unroll):
    i = pl.program_id(0)

    def issue_tile(t, slot):
        base = base_ref[t * (tm // sub)]

        def issue(g, c):
            for u in range(unroll):
                r = g * unroll + u
                _token_copy(hp_hbm, buf.at[slot], sem.at[slot], tok_ref[base + r], r).start()
            return c

        lax.fori_loop(0, tm // unroll, issue, 0)

    @pl.when(i == 0)
    def _():
        issue_tile(0, 0)

    nxt = i + 1

    @pl.when(jnp.logical_and(nxt < pl.num_programs(0), nxt * tm < nrows_ref[0]))
    def _():
        issue_tile(nxt, nxt % 2)

    used = i * tm < nrows_ref[0]

    @pl.when(jnp.logical_not(used))
    def _():
        o_ref[...] = jnp.zeros_like(o_ref)

    @pl.when(used)
    def _():
        slot = i % 2
        pltpu.make_async_copy(hp_hbm.at[pl.ds(0, tm)], buf.at[slot], sem.at[slot]).wait()
        o_ref[...] = _unpack_bf16_pairs(_load_token_major(buf[slot]))


def gather_rows(hp, sorted_tok, sub_base, n_used_rows, *, cap, tm, sub):
    half = hp.shape[1] * hp.shape[2]
    return pl.pallas_call(
        functools.partial(_gather_body, tm=tm, sub=sub, unroll=8),
        grid_spec=pltpu.PrefetchScalarGridSpec(
            num_scalar_prefetch=3,
            grid=(cap // tm,),
            in_specs=[pl.BlockSpec(memory_space=pl.ANY)],
            out_specs=pl.BlockSpec((tm, 2 * half), lambda i, tok, base, nused: (i, 0)),
            scratch_shapes=[pltpu.VMEM((2, tm) + hp.shape[1:], jnp.uint32), pltpu.SemaphoreType.DMA((2,))]),
        out_shape=jax.ShapeDtypeStruct((cap, 2 * half), BF16),
        compiler_params=_cparams(("arbitrary",)),
        name="gather_rows",
    )(sorted_tok, sub_base, n_used_rows, hp)


def _grouped_tile(i, n_rows, tm):
    return jnp.minimum(i, (n_rows[0] - 1) // tm)


def _grouped_fresh(i, sub_e, ratio):
    prev = sub_e[(jnp.maximum(i, 1) - 1) * ratio]
    return jnp.logical_or(i == 0, sub_e[i * ratio] != prev)


def _moe_up_body(sub_e, sub_n, n_rows, x_ref, wg_ref, wu_ref, o_ref, wgb_ref, wub_ref, *, tm, sub):
    i = pl.program_id(1)
    ratio = tm // sub
    used = i * tm < n_rows[0]

    @pl.when(jnp.logical_and(used, _grouped_fresh(i, sub_e, ratio)))
    def _():
        wgb_ref[...] = wg_ref[0].astype(BF16)
        wub_ref[...] = wu_ref[0].astype(BF16)

    def chunk(s):
        x = x_ref[s * sub:(s + 1) * sub, :]
        g = jnp.dot(x, wgb_ref[...], preferred_element_type=F32)
        u = jnp.dot(x, wub_ref[...], preferred_element_type=F32)
        o_ref[s * sub:(s + 1) * sub, :] = (_silu(g) * u).astype(o_ref.dtype)

    full = jnp.logical_and(used, sub_n[i * ratio + ratio - 1] > 0)

    @pl.when(full)
    def _():
        for s in range(ratio):
            chunk(s)

    for s in range(ratio):
        live = jnp.logical_and(used, sub_n[i * ratio + s] > 0)

        @pl.when(jnp.logical_and(live, jnp.logical_not(full)))
        def _():
            chunk(s)

        @pl.when(jnp.logical_not(live))
        def _():
            o_ref[s * sub:(s + 1) * sub, :] = jnp.zeros((sub, o_ref.shape[1]), o_ref.dtype)


def moe_swiglu_up(xs, w_gu, sub_e, sub_n, n_rows, *, tm, tf, sub):
    cap, d = xs.shape
    f = w_gu.shape[2] // 2
    nf = f // tf
    ratio = tm // sub

    def x_map(j, i, se, sn, nr):
        return (_grouped_tile(i, nr, tm), 0)

    def wg_map(j, i, se, sn, nr):
        return (se[_grouped_tile(i, nr, tm) * ratio], 0, j)

    def wu_map(j, i, se, sn, nr):
        return (se[_grouped_tile(i, nr, tm) * ratio], 0, nf + j)

    def o_map(j, i, se, sn, nr):
        return (i, j)

    return pl.pallas_call(
        functools.partial(_moe_up_body, tm=tm, sub=sub),
        grid_spec=pltpu.PrefetchScalarGridSpec(
            num_scalar_prefetch=3,
            grid=(nf, cap // tm),
            in_specs=[pl.BlockSpec((tm, d), x_map),
                      pl.BlockSpec((1, d, tf), wg_map),
                      pl.BlockSpec((1, d, tf), wu_map)],
            out_specs=pl.BlockSpec((tm, tf), o_map),
            scratch_shapes=[pltpu.VMEM((d, tf), BF16), pltpu.VMEM((d, tf), BF16)]),
        out_shape=jax.ShapeDtypeStruct((cap, f), BF16),
        compiler_params=_cparams(("arbitrary", "arbitrary")),
        name="moe_swiglu_up",
    )(sub_e, sub_n, n_rows, xs, w_gu, w_gu)


def _moe_down_body(sub_e, sub_n, n_rows, x_ref, w_ref, o_ref, wb_ref, *, tm, sub):
    i = pl.program_id(1)
    ratio = tm // sub
    used = i * tm < n_rows[0]

    @pl.when(jnp.logical_and(used, _grouped_fresh(i, sub_e, ratio)))
    def _():
        wb_ref[...] = w_ref[0].astype(BF16)

    def chunk(s):
        o_ref[s * sub:(s + 1) * sub, :] = jnp.dot(
            x_ref[s * sub:(s + 1) * sub, :], wb_ref[...], preferred_element_type=F32)

    full = jnp.logical_and(used, sub_n[i * ratio + ratio - 1] > 0)

    @pl.when(full)
    def _():
        for s in range(ratio):
            chunk(s)

    for s in range(ratio):
        live = jnp.logical_and(used, sub_n[i * ratio + s] > 0)

        @pl.when(jnp.logical_and(live, jnp.logical_not(full)))
        def _():
            chunk(s)

        @pl.when(jnp.logical_not(live))
        def _():
            o_ref[s * sub:(s + 1) * sub, :] = jnp.zeros((sub, o_ref.shape[1]), o_ref.dtype)


def moe_down(act, w_down, sub_e, sub_n, n_rows, *, tm, tn, sub):
    cap, f = act.shape
    d = w_down.shape[2]
    ratio = tm // sub

    def x_map(j, i, se, sn, nr):
        return (_grouped_tile(i, nr, tm), 0)

    def w_map(j, i, se, sn, nr):
        return (se[_grouped_tile(i, nr, tm) * ratio], 0, j)

    def o_map(j, i, se, sn, nr):
        return (i, j)

    return pl.pallas_call(
        functools.partial(_moe_down_body, tm=tm, sub=sub),
        grid_spec=pltpu.PrefetchScalarGridSpec(
            num_scalar_prefetch=3,
            grid=(d // tn, cap // tm),
            in_specs=[pl.BlockSpec((tm, f), x_map),
                      pl.BlockSpec((1, f, tn), w_map)],
            out_specs=pl.BlockSpec((tm, tn), o_map),
            scratch_shapes=[pltpu.VMEM((f, tn), BF16)]),
        out_shape=jax.ShapeDtypeStruct((cap, d), F32),
        compiler_params=_cparams(("arbitrary", "arbitrary")),
        name="moe_down",
    )(sub_e, sub_n, n_rows, act, w_down)


def _combine_body(dest_ref, y_hbm, gates_ref, res_ref, g_ref, b_ref, o_ref, buf, sem, *, tm, unroll):
    i = pl.program_id(0)

    def issue_tile(t, slot):
        def issue(g, c):
            for u in range(unroll):
                r = g * unroll + u
                for k in range(TOP_K):
                    _row_copy(y_hbm, buf.at[slot, k], sem.at[slot, k],
                              dest_ref[TOP_K * (t * tm + r) + k], r).start()
            return c

        lax.fori_loop(0, tm // unroll, issue, 0)

    @pl.when(i == 0)
    def _():
        issue_tile(0, 0)

    nxt = i + 1

    @pl.when(nxt < pl.num_programs(0))
    def _():
        issue_tile(nxt, nxt % 2)

    slot = i % 2
    gates = gates_ref[...]
    y = None
    for k in range(TOP_K):
        pltpu.make_async_copy(y_hbm.at[pl.ds(0, tm), :], buf.at[slot, k], sem.at[slot, k]).wait()
        term = buf[slot, k] * gates[:, TOP_K + k:TOP_K + k + 1]
        y = term if y is None else y + term
    o_ref[...] = _layer_norm_rows(DEEPNORM_ALPHA * res_ref[...] + y, g_ref[...], b_ref[...])


def moe_combine_post_norm(y_rows, dest, routing, res, g, b, *, tm):
    n_tok, d = res.shape
    lanes = routing.shape[1]
    return pl.pallas_call(
        functools.partial(_combine_body, tm=tm, unroll=4),
        grid_spec=pltpu.PrefetchScalarGridSpec(
            num_scalar_prefetch=1,
            grid=(n_tok // tm,),
            in_specs=[pl.BlockSpec(memory_space=pl.ANY),
                      pl.BlockSpec((tm, lanes), lambda i, dst: (i, 0)),
                      pl.BlockSpec((tm, d), lambda i, dst: (i, 0)),
                      pl.BlockSpec((1, d), lambda i, dst: (0, 0)),
                      pl.BlockSpec((1, d), lambda i, dst: (0, 0))],
            out_specs=pl.BlockSpec((tm, d), lambda i, dst: (i, 0)),
            scratch_shapes=[pltpu.VMEM((2, TOP_K, tm, d), F32), pltpu.SemaphoreType.DMA((2, TOP_K))]),
        out_shape=jax.ShapeDtypeStruct((n_tok, d), F32),
        compiler_params=_cparams(("arbitrary",)),
        name="moe_combine_post_norm",
    )(dest, y_rows, routing, res, g.reshape(1, d), b.reshape(1, d))


def _group_rows(seg, n_seg, tile, sub):
    n_asg = seg.shape[0]
    cap = n_asg + n_seg * tile
    onehot = (seg[:, None] == jnp.arange(n_seg, dtype=jnp.int32)[None, :]).astype(jnp.int32)
    csum = jnp.cumsum(onehot, axis=0)
    rank = jnp.sum(csum * onehot, axis=1) - 1
    counts = csum[-1]
    padded = (counts + tile - 1) // tile * tile
    pad_end = jnp.cumsum(padded)
    pad_start = pad_end - padded
    dest = (pad_start[seg] + rank).astype(jnp.int32)
    order = jnp.sort(seg * n_asg + jnp.arange(n_asg, dtype=jnp.int32))
    sorted_tok = jnp.pad((order % n_asg) // TOP_K, (0, tile)).astype(jnp.int32)
    sub_start = jnp.arange(cap // sub, dtype=jnp.int32) * sub
    sub_e = jnp.sum((pad_end[None, :] <= sub_start[:, None]).astype(jnp.int32), axis=1)
    sub_e = jnp.minimum(sub_e, n_seg - 1)
    sub_n = jnp.clip(counts[sub_e] - (sub_start - pad_start[sub_e]), 0, sub).astype(jnp.int32)
    start = jnp.cumsum(counts) - counts
    sub_base = (sub_start - (pad_start - start)[sub_e]).astype(jnp.int32)
    n_rows = pad_end[-1:].astype(jnp.int32)
    return dest, sorted_tok, sub_base, sub_e, sub_n, n_rows, cap


def _gla_layer(h, hb, w_in, w_gate_up, b_gate, g_norm, w_o, ln_g, ln_b):
    bsz, s, d = h.shape
    dk = (d // 2) // GLA_HEADS
    dv = d // GLA_HEADS
    n_main = 2 * GLA_HEADS * dk + 2 * GLA_HEADS * dv
    x2 = hb.reshape(bsz * s, d)
    n_qk = 2 * GLA_HEADS * dk
    tm = _pick(bsz * s, (1024, 512, 256, 128, 64))
    tn = _pick(n_qk, (1024, 512, 256, 128))
    qk = matmul(x2, w_in, tm=tm, tn=tn, out_dtype=F32, col_start=0, n_cols=n_qk)
    vr = matmul(x2, w_in, tm=tm, tn=tn, out_dtype=F32, col_start=n_qk, n_cols=n_main - n_qk)
    w_z = jnp.pad(w_in[:, n_main:], ((0, 0), (0, LANES - GLA_GATE_RANK)))
    z = matmul(x2, w_z, tm=tm, tn=LANES, out_dtype=F32)
    wgu = jnp.pad(w_gate_up, ((0, LANES - GLA_GATE_RANK), (0, 0))).astype(BF16)
    wgu = wgu.reshape(LANES, GLA_HEADS, dk).transpose(1, 0, 2)
    og = gla_core(qk.reshape(bsz, s, n_qk), vr.reshape(bsz, s, n_main - n_qk), z.reshape(bsz, s, LANES), wgu,
                  b_gate.reshape(GLA_HEADS, 1, dk), g_norm.reshape(1, dv),
                  heads=GLA_HEADS, dk=dk, dv=dv, step=min(GLA_STEP_TOKENS, s))
    out, outb = matmul_post_norm(og.reshape(bsz * s, d), w_o, h.reshape(bsz * s, d), ln_g, ln_b,
                                 tm=512, sub=256)
    return out.reshape(bsz, s, d), outb.reshape(bsz, s, d)


def _moba_layer(h, hb, w_q, w_kv, w_o, ln_g, ln_b):
    bsz, s, d = h.shape
    hd = d // MOBA_HEADS
    x2 = hb.reshape(bsz * s, d)
    tm = _pick(bsz * s, (1024, 512, 256, 128, 64))
    n_kv = w_kv.shape[1]
    q = matmul(x2, w_q, tm=tm, tn=_pick(d, (1024, 512, 256, 128)), out_dtype=BF16, scale=hd ** -0.5 * LOG2_E)
    kv = matmul(x2, w_kv, tm=tm, tn=_pick(n_kv, (1024, 512, 256, 128)), out_dtype=BF16)
    att = moba_attention(q.reshape(bsz, s, d), kv.reshape(bsz, s, n_kv), heads=MOBA_HEADS,
                         kv_heads=MOBA_KV_HEADS, hd=hd, blk=MOBA_BLOCK, topk=MOBA_TOPK)
    out, outb, outp = matmul_post_norm(att.reshape(bsz * s, d), w_o, h.reshape(bsz * s, d),
                                       ln_g, ln_b, tm=512, sub=256, packed=True)
    return out.reshape(bsz, s, d), outb.reshape(bsz, s, d), outp


def _dense_ffn(h, hb, w_gu, w_down, ln_g, ln_b):
    bsz, s, d = h.shape
    f = w_down.shape[0]
    tm = _pick(bsz * s, (1024, 512, 256, 128, 64))
    act = swiglu_up(hb.reshape(bsz * s, d), w_gu, tm=tm, tf=512, sub=tm)
    out, outb = matmul_post_norm(act, w_down.astype(BF16), h.reshape(bsz * s, d), ln_g, ln_b, tm=512, sub=256)
    return out.reshape(bsz, s, d), outb.reshape(bsz, s, d)


def _moe_ffn(h, hb, hp, w_router, w_gu, w_down, ln_g, ln_b):
    bsz, s, d = h.shape
    n_tok = bsz * s
    h2 = h.reshape(n_tok, d)
    w_r = jnp.pad(w_router, ((0, 0), (0, LANES - N_EXPERTS))).astype(BF16)
    routing = router_top2(hb.reshape(n_tok, d), w_r, tm=_pick(n_tok, (1024, 512, 256, 128, 64)),
                          n_experts=N_EXPERTS)
    seg = routing[:, :TOP_K].astype(jnp.int32).reshape(-1)
    tile = min(MOE_TILE, n_tok)
    sub = min(MOE_SUB, tile)
    dest, sorted_tok, sub_base, sub_e, sub_n, n_rows, cap = _group_rows(seg, N_EXPERTS, tile, sub)
    xs = gather_rows(hp, sorted_tok, sub_base, n_rows, cap=cap, tm=min(GATHER_TILE, tile), sub=sub)
    f = w_down.shape[1]
    act = moe_swiglu_up(xs, w_gu, sub_e, sub_n, n_rows, tm=tile, tf=_pick(f, (512, 256, 128)), sub=sub)
    y_rows = moe_down(act, w_down, sub_e, sub_n, n_rows, tm=min(MOE_DOWN_TILE, tile),
                      tn=_pick(d, (512, 256, 128)), sub=sub)
    out = moe_combine_post_norm(y_rows, dest, routing, h2, ln_g, ln_b, tm=_pick(n_tok, (512, 256, 128, 64)))
    return out.reshape(bsz, s, d)


def kernel(x, w_in_a, w_gate_up_a, b_gate_a, g_norm_a, w_o_a, w_kv_shared, w_q_b, w_o_b,
           ln_mix_g, ln_mix_b, w_gu_dense, w_down_dense, w_router, w_gu_moe, w_down_moe,
           ln_ffn_g, ln_ffn_b):
    h, hb = _gla_layer(x, x.astype(BF16), w_in_a[0], w_gate_up_a[0], b_gate_a[0], g_norm_a[0], w_o_a[0],
                       ln_mix_g[0], ln_mix_b[0])
    h, hb = _dense_ffn(h, hb, w_gu_dense[0], w_down_dense[0], ln_ffn_g[0], ln_ffn_b[0])
    h, hb, hp = _moba_layer(h, hb, w_q_b[0], w_kv_shared, w_o_b[0], ln_mix_g[1], ln_mix_b[1])
    return _moe_ffn(h, hb, hp, w_router[0], w_gu_moe[0], w_down_moe[0], ln_ffn_g[1], ln_ffn_b[1])
```

```python
import functools
import math

import jax
import jax.numpy as jnp
from jax import lax
from jax.experimental import pallas as pl
from jax.experimental.pallas import tpu as pltpu

F32 = jnp.float32
BF16 = jnp.bfloat16

DEPTH = 2
GLA_HEADS = 4
GLA_GATE_RANK = 16
GLA_TAU = 16.0
GLA_CHUNK = 64
MOBA_HEADS = 16
MOBA_KV_HEADS = 4
MOBA_BLOCK = 256
MOBA_TOPK = 3
N_EXPERTS = 8
TOP_K = 2
LN_EPS = 1e-5
RMS_EPS = 1e-6
DEEPNORM_ALPHA = (2 * DEPTH) ** 0.25

LANES = 128
VMEM_LIMIT_BYTES = 56 * 1024 * 1024

GLA_GROUP = 8
GLA_STEP_TOKENS = 1024
MOE_TILE = 1024
MOE_SUB = 256
MOE_DOWN_TILE = 512
GATHER_TILE = 1024


def _cparams(sem):
    return pltpu.CompilerParams(dimension_semantics=sem, vmem_limit_bytes=VMEM_LIMIT_BYTES)


def _pick(n, prefs):
    for p in prefs:
        if n % p == 0:
            return p
    return n


def _mm_body(x_ref, w_ref, o_ref, wb_ref, *, scale):
    @pl.when(pl.program_id(1) == 0)
    def _():
        wb_ref[...] = w_ref[...].astype(BF16)

    acc = jnp.dot(x_ref[...], wb_ref[...], preferred_element_type=F32)
    if scale != 1.0:
        acc = acc * scale
    o_ref[...] = acc.astype(o_ref.dtype)


def matmul(x, w, *, tm, tn, out_dtype, col_start=0, n_cols=None, scale=1.0):
    m, k = x.shape
    n_cols = w.shape[1] - col_start if n_cols is None else n_cols
    assert col_start % tn == 0 and n_cols % tn == 0
    first = col_start // tn
    return pl.pallas_call(
        functools.partial(_mm_body, scale=scale),
        grid=(n_cols // tn, m // tm),
        in_specs=[pl.BlockSpec((tm, k), lambda j, i: (i, 0)),
                  pl.BlockSpec((k, tn), lambda j, i: (0, first + j))],
        out_specs=pl.BlockSpec((tm, tn), lambda j, i: (i, j)),
        out_shape=jax.ShapeDtypeStruct((m, n_cols), out_dtype),
        scratch_shapes=[pltpu.VMEM((k, tn), BF16)],
        compiler_params=_cparams(("arbitrary", "arbitrary")),
        name="matmul",
    )(x, w)


def _layer_norm_rows(t, g, b):
    mu = jnp.mean(t, axis=-1, keepdims=True)
    d = t - mu
    var = jnp.mean(d * d, axis=-1, keepdims=True)
    return d * lax.rsqrt(var + LN_EPS) * g + b


def _mm_ln_body(x_ref, w_ref, res_ref, g_ref, b_ref, o_ref, ob_ref, *rest, tm, sub, packed, cast):
    if cast:
        wb_ref = rest[-1]

        @pl.when(pl.program_id(0) == 0)
        def _():
            wb_ref[...] = w_ref[...].astype(BF16)
    else:
        wb_ref = w_ref
    for s in range(tm // sub):
        rows = slice(s * sub, (s + 1) * sub)
        y = jnp.dot(x_ref[rows, :], wb_ref[...], preferred_element_type=F32)
        out = _layer_norm_rows(DEEPNORM_ALPHA * res_ref[rows, :] + y, g_ref[...], b_ref[...])
        o_ref[rows, :] = out
        ob_ref[rows, :] = out.astype(BF16)
        if packed:
            _store_token_major(rest[0], rows, _pack_bf16_pairs(out))


def matmul_post_norm(x, w, res, g, b, *, tm, sub, packed=False):
    m, k = x.shape
    d = w.shape[1]
    cast = w.dtype != BF16
    out_specs = [pl.BlockSpec((tm, d), lambda i: (i, 0)), pl.BlockSpec((tm, d), lambda i: (i, 0))]
    out_shape = [jax.ShapeDtypeStruct((m, d), F32), jax.ShapeDtypeStruct((m, d), BF16)]
    if packed:
        out_specs.append(pl.BlockSpec((tm, d // 2 // LANES, LANES), lambda i: (i, 0, 0)))
        out_shape.append(jax.ShapeDtypeStruct((m, d // 2 // LANES, LANES), jnp.uint32))
    return pl.pallas_call(
        functools.partial(_mm_ln_body, tm=tm, sub=sub, packed=packed, cast=cast),
        grid=(m // tm,),
        scratch_shapes=[pltpu.VMEM((k, d), BF16)] if cast else [],
        in_specs=[pl.BlockSpec((tm, k), lambda i: (i, 0)),
                  pl.BlockSpec((k, d), lambda i: (0, 0), pipeline_mode=pl.Buffered(1)),
                  pl.BlockSpec((tm, d), lambda i: (i, 0)),
                  pl.BlockSpec((1, d), lambda i: (0, 0)),
                  pl.BlockSpec((1, d), lambda i: (0, 0))],
        out_specs=out_specs,
        out_shape=out_shape,
        compiler_params=_cparams(("arbitrary",)),
        name="matmul_post_norm",
    )(x, w, res, g.reshape(1, d), b.reshape(1, d))


def _silu(x):
    return x / (1.0 + jnp.exp(-x))


def _swiglu_body(x_ref, wg_ref, *rest, f, tm, tf, sub):
    n_u = tf // LANES
    wu_refs, o_ref, wgb_ref, wub_ref = rest[:n_u], rest[n_u], rest[n_u + 1], rest[n_u + 2]
    j = pl.program_id(0)

    @pl.when(pl.program_id(1) == 0)
    def _():
        col = j * tf + lax.broadcasted_iota(jnp.int32, wg_ref.shape, 1)
        wgb_ref[...] = jnp.where(col < f, wg_ref[...], 0.0).astype(BF16)
        for c, wu_ref in enumerate(wu_refs):
            keep = j * tf + c * LANES < f
            wub_ref[:, c * LANES:(c + 1) * LANES] = jnp.where(keep, wu_ref[...], 0.0).astype(BF16)

    for s in range(tm // sub):
        x = x_ref[s * sub:(s + 1) * sub, :]
        g = jnp.dot(x, wgb_ref[...], preferred_element_type=F32)
        u = jnp.dot(x, wub_ref[...], preferred_element_type=F32)
        o_ref[s * sub:(s + 1) * sub, :] = (_silu(g) * u).astype(o_ref.dtype)


def swiglu_up(x, w_gu, *, tm, tf, sub):
    m, k = x.shape
    f = w_gu.shape[1] // 2
    assert f % LANES == 0 and tf % LANES == 0
    nf = pl.cdiv(f, tf)
    n_u = tf // LANES
    last_u = 2 * f // LANES - 1

    def u_map(c):
        return lambda j, i: (0, jnp.minimum(f // LANES + j * n_u + c, last_u))

    return pl.pallas_call(
        functools.partial(_swiglu_body, f=f, tm=tm, tf=tf, sub=sub),
        grid=(nf, m // tm),
        in_specs=[pl.BlockSpec((tm, k), lambda j, i: (i, 0)),
                  pl.BlockSpec((k, tf), lambda j, i: (0, j))]
        + [pl.BlockSpec((k, LANES), u_map(c)) for c in range(n_u)],
        out_specs=pl.BlockSpec((tm, tf), lambda j, i: (i, j)),
        out_shape=jax.ShapeDtypeStruct((m, f), BF16),
        scratch_shapes=[pltpu.VMEM((k, tf), BF16), pltpu.VMEM((k, tf), BF16)],
        compiler_params=_cparams(("arbitrary", "arbitrary")),
        name="swiglu_up",
    )(x, w_gu, *([w_gu] * n_u))


def _gla_body(q_ref, k_ref, v_ref, r_ref, z_ref, wgu_ref, bg_ref, gn_ref, o_ref, state_ref,
              *, chunk, group, n_groups, dk):
    @pl.when(pl.program_id(2) == 0)
    def _():
        state_ref[...] = jnp.zeros_like(state_ref)

    gt = group * chunk
    row = lax.broadcasted_iota(jnp.int32, (gt, gt), 0)
    col = lax.broadcasted_iota(jnp.int32, (gt, gt), 1)
    causal = jnp.logical_and(row // chunk == col // chunk, col <= row)
    tril = jnp.where(causal, 1.0, 0.0).astype(BF16)
    nt = (((1,), (1,)), ((), ()))
    tn = (((0,), (0,)), ((), ()))

    def body(g, carry):
        sl = pl.ds(pl.multiple_of(g * gt, gt), gt)
        gate = jnp.dot(z_ref[0, sl, :].astype(BF16), wgu_ref[0], preferred_element_type=F32) + bg_ref[0]
        log_a = (jnp.minimum(gate, 0.0) - jnp.log(1.0 + jnp.exp(-jnp.abs(gate)))) / GLA_TAU
        la_hi = log_a.astype(BF16)
        la_lo = (log_a - la_hi.astype(F32)).astype(BF16)
        b_cum = (jnp.dot(tril, la_hi, preferred_element_type=F32)
                 + jnp.dot(tril, la_lo, preferred_element_type=F32))
        b_last = [b_cum[(c + 1) * chunk - 1:(c + 1) * chunk, :] for c in range(group)]
        b_last_rows = jnp.concatenate([jnp.broadcast_to(b, (chunk, b.shape[1])) for b in b_last], axis=0)
        total = [jnp.zeros_like(b_last[0])]
        for c in range(group):
            total.append(total[c] + b_last[c])

        def per_chunk_rows(rows_of):
            return jnp.concatenate([jnp.broadcast_to(rows_of(c), (chunk, dk)) for c in range(group)], axis=0)

        q = q_ref[0, sl, :] * (dk ** -0.5)
        k = k_ref[0, sl, :]
        q_dec = q * jnp.exp(b_cum)
        q_dec_b = q_dec.astype(BF16)
        k_inv = (k * jnp.exp(-b_cum)).astype(BF16)
        k_end = k * jnp.exp(b_last_rows - b_cum)
        v = v_ref[0, sl, :].astype(BF16)
        att = lax.dot_general(q_dec_b, k_inv, nt, preferred_element_type=F32)
        att = jnp.where(causal, att, 0.0).astype(BF16)
        o = jnp.dot(att, v, preferred_element_type=F32)
        state_t = state_ref[...]
        q_grp = (q_dec * jnp.exp(per_chunk_rows(lambda c: total[c]))).astype(BF16)
        o = o + lax.dot_general(q_grp, state_t.astype(BF16), nt, preferred_element_type=F32)
        outs = [o[:chunk]]
        for c in range(1, group):
            keys = jnp.concatenate([k_end[p * chunk:(p + 1) * chunk] * jnp.exp(total[c] - total[p + 1])
                                    for p in range(c)], axis=0).astype(BF16)
            a_c = lax.dot_general(q_dec_b[c * chunk:(c + 1) * chunk], keys, nt, preferred_element_type=F32)
            outs.append(o[c * chunk:(c + 1) * chunk]
                        + jnp.dot(a_c.astype(BF16), v[:c * chunk], preferred_element_type=F32))
        o = jnp.concatenate(outs, axis=0)
        k_grp = (k_end * jnp.exp(per_chunk_rows(lambda c: total[group] - total[c + 1]))).astype(BF16)
        state_ref[...] = (jnp.exp(total[group]) * state_t
                          + lax.dot_general(v, k_grp, tn, preferred_element_type=F32))
        o = o * lax.rsqrt(jnp.mean(o * o, axis=-1, keepdims=True) + RMS_EPS) * gn_ref[...]
        o_ref[0, sl, :] = (o * _silu(r_ref[0, sl, :].astype(F32))).astype(o_ref.dtype)
        return carry

    lax.fori_loop(0, n_groups, body, 0)


def gla_core(qk, vr, z, wgu, bg, gn, *, heads, dk, dv, step):
    bsz, s, _ = qk.shape
    zr = z.shape[-1]
    k_off = heads
    r_off = heads
    return pl.pallas_call(
        functools.partial(_gla_body, chunk=GLA_CHUNK, group=GLA_GROUP,
                          n_groups=step // (GLA_GROUP * GLA_CHUNK), dk=dk),
        grid=(bsz, heads, s // step),
        in_specs=[pl.BlockSpec((1, step, dk), lambda b, h, t: (b, t, h)),
                  pl.BlockSpec((1, step, dk), lambda b, h, t: (b, t, k_off + h)),
                  pl.BlockSpec((1, step, dv), lambda b, h, t: (b, t, h)),
                  pl.BlockSpec((1, step, dv), lambda b, h, t: (b, t, r_off + h)),
                  pl.BlockSpec((1, step, zr), lambda b, h, t: (b, t, 0)),
                  pl.BlockSpec((1, zr, dk), lambda b, h, t: (h, 0, 0)),
                  pl.BlockSpec((1, 1, dk), lambda b, h, t: (h, 0, 0)),
                  pl.BlockSpec((1, dv), lambda b, h, t: (0, 0))],
        out_specs=pl.BlockSpec((1, step, dv), lambda b, h, t: (b, t, h)),
        out_shape=jax.ShapeDtypeStruct((bsz, s, heads * dv), BF16),
        scratch_shapes=[pltpu.VMEM((dv, dk), F32)],
        compiler_params=_cparams(("parallel", "parallel", "arbitrary")),
        name="gla_core",
    )(qk, qk, vr, vr, z, wgu, bg, gn)


MASK_PENALTY = -3.0e38
LOG2_E = math.log2(math.e)
SUM_ROWS = 16


def _moba_body(q_ref, k_ref, v_ref, o_ref, kmean_ref, kaug_ref, vt_ref, rhs_ref, m_ref, acc_ref,
               *, blk, n_blk, rep, hd, topk):
    i = pl.program_id(2)
    nq = rep * blk

    @pl.when(i == 0)
    def _():
        lane = lax.broadcasted_iota(jnp.int32, (blk, hd), 1)
        for j in range(n_blk):
            kj = k_ref[0, j * blk:(j + 1) * blk, :]
            kaug_ref[j, :, :hd] = kj
            kaug_ref[j, :, hd:] = jnp.where(lane == j, 1.0, 0.0).astype(BF16)
            kmean_ref[j:j + 1, :] = jnp.sum(kj.astype(F32), axis=0, keepdims=True) / blk
            vt_ref[j, :hd, :] = v_ref[0, j * blk:(j + 1) * blk, :].astype(F32).T.astype(BF16)
            vt_ref[j, hd:, :] = jnp.ones((SUM_ROWS, blk), BF16)
        rhs_ref[hd + n_blk:, :] = jnp.zeros((hd - n_blk, nq), BF16)

    qt = jnp.concatenate([q_ref[0, :, r * hd:(r + 1) * hd].astype(F32).T for r in range(rep)], axis=1)
    qt = qt.astype(BF16)
    gate = jnp.dot(kmean_ref[...].astype(BF16), qt, preferred_element_type=F32)
    blk_id = lax.broadcasted_iota(jnp.int32, (n_blk, nq), 0)
    past = blk_id < i
    gate = jnp.where(past, gate, -jnp.inf)
    beaten = jnp.zeros((n_blk, nq), jnp.int32)
    for c in range(n_blk):
        gc = gate[c:c + 1, :]
        beaten = beaten + jnp.where(gc > gate, 1, jnp.where(gc == gate, jnp.where(c < blk_id, 1, 0), 0))
    penalty = jnp.where(past, jnp.where(beaten < topk, 0.0, MASK_PENALTY),
                        jnp.where(blk_id == i, 0.0, MASK_PENALTY))
    rhs_ref[:hd, :] = qt
    rhs_ref[hd:hd + n_blk, :] = penalty.astype(BF16)

    m_ref[...] = jnp.full((1, nq), -jnp.inf, F32)
    acc_ref[...] = jnp.zeros_like(acc_ref)
    group = 4

    def attend(j0, n, own):
        keys = kaug_ref[pl.ds(j0, n)].reshape(n * blk, 2 * hd)
        s = jnp.dot(keys, rhs_ref[...], preferred_element_type=F32)
        if own:
            key = lax.broadcasted_iota(jnp.int32, (blk, nq), 0)
            qpos = lax.rem(lax.broadcasted_iota(jnp.int32, (blk, nq), 1), blk)
            tail = jnp.where(key <= qpos, s[(n - 1) * blk:], -jnp.inf)
            s = tail if n == 1 else jnp.concatenate([s[:(n - 1) * blk], tail], axis=0)
        m_old = m_ref[...]
        m_new = jnp.maximum(m_old, jnp.max(s, axis=0, keepdims=True))
        p = jnp.exp2(s - m_new).astype(BF16)
        upd = jnp.exp2(m_old - m_new) * acc_ref[...]
        for c in range(n):
            upd = upd + jnp.dot(vt_ref[j0 + c], p[c * blk:(c + 1) * blk], preferred_element_type=F32)
        acc_ref[...] = upd
        m_ref[...] = m_new

    n_last = i + 1 - (i // group) * group

    def full_group(jj, carry):
        attend(group * jj, group, False)
        return carry

    lax.fori_loop(0, i // group, full_group, 0)
    for n in range(1, group + 1):
        @pl.when(n_last == n)
        def _():
            attend(i + 1 - n, n, True)

    acc = acc_ref[...]
    out = acc[:hd] * (1.0 / acc[hd:hd + 1])
    for r in range(rep):
        o_ref[0, :, r * hd:(r + 1) * hd] = out[:, r * blk:(r + 1) * blk].T.astype(o_ref.dtype)


def moba_attention(q, kv, *, heads, kv_heads, hd, blk, topk):
    bsz, s, _ = q.shape
    rep = heads // kv_heads
    n_blk = s // blk
    nq = rep * blk
    assert n_blk <= hd and n_blk % 16 == 0
    v_off = kv_heads
    return pl.pallas_call(
        functools.partial(_moba_body, blk=blk, n_blk=n_blk, rep=rep, hd=hd, topk=topk),
        grid=(bsz, kv_heads, n_blk),
        in_specs=[pl.BlockSpec((1, blk, rep * hd), lambda b, g, i: (b, i, g)),
                  pl.BlockSpec((1, s, hd), lambda b, g, i: (b, 0, g)),
                  pl.BlockSpec((1, s, hd), lambda b, g, i: (b, 0, v_off + g))],
        out_specs=pl.BlockSpec((1, blk, rep * hd), lambda b, g, i: (b, i, g)),
        out_shape=jax.ShapeDtypeStruct((bsz, s, heads * hd), BF16),
        scratch_shapes=[pltpu.VMEM((n_blk, hd), F32),
                        pltpu.VMEM((n_blk, blk, 2 * hd), BF16),
                        pltpu.VMEM((n_blk, hd + SUM_ROWS, blk), BF16),
                        pltpu.VMEM((2 * hd, nq), BF16),
                        pltpu.VMEM((1, nq), F32),
                        pltpu.VMEM((hd + SUM_ROWS, nq), F32)],
        compiler_params=_cparams(("parallel", "parallel", "arbitrary")),
        name="moba_attention",
    )(q, kv, kv)


def _router_body(x_ref, w_ref, o_ref, *, n_experts):
    logits = jnp.dot(x_ref[...], w_ref[...], preferred_element_type=F32)
    lane = lax.broadcasted_iota(jnp.int32, logits.shape, 1)
    lane_f = lane.astype(F32)
    big = float(logits.shape[1])
    lg = jnp.where(lane < n_experts, logits, -jnp.inf)
    m1 = jnp.max(lg, axis=-1, keepdims=True)
    i1 = jnp.min(jnp.where(lg == m1, lane_f, big), axis=-1, keepdims=True)
    lg2 = jnp.where(lane_f == i1, -jnp.inf, lg)
    m2 = jnp.max(lg2, axis=-1, keepdims=True)
    i2 = jnp.min(jnp.where(lg2 == m2, lane_f, big), axis=-1, keepdims=True)
    t = jnp.exp(m2 - m1)
    g1 = 1.0 / (1.0 + t)
    g2 = t / (1.0 + t)
    o_ref[...] = jnp.where(lane == 0, i1,
                           jnp.where(lane == 1, i2, jnp.where(lane == 2, g1, jnp.where(lane == 3, g2, 0.0))))


def router_top2(x, w_pad, *, tm, n_experts):
    m, k = x.shape
    n = w_pad.shape[1]
    return pl.pallas_call(
        functools.partial(_router_body, n_experts=n_experts),
        grid=(m // tm,),
        in_specs=[pl.BlockSpec((tm, k), lambda i: (i, 0)),
                  pl.BlockSpec((k, n), lambda i: (0, 0))],
        out_specs=pl.BlockSpec((tm, n), lambda i: (i, 0)),
        out_shape=jax.ShapeDtypeStruct((m, n), F32),
        compiler_params=_cparams(("parallel",)),
        name="router_top2",
    )(x, w_pad)


def _row_copy(src_hbm, dst_vmem, sem, src_row, dst_row):
    return pltpu.make_async_copy(src_hbm.at[pl.ds(src_row, 1), :], dst_vmem.at[pl.ds(dst_row, 1), :], sem)


def _token_copy(src_hbm, dst_vmem, sem, src_row, dst_row):
    return pltpu.make_async_copy(src_hbm.at[src_row], dst_vmem.at[dst_row], sem)


def _store_token_major(ref, rows, x):
    for c in range(x.shape[1] // LANES):
        ref[rows, c, :] = x[:, c * LANES:(c + 1) * LANES]


def _load_token_major(x):
    return jnp.concatenate([x[:, c, :] for c in range(x.shape[1])], axis=1)


def _pack_bf16_pairs(x):
    half = x.shape[1] // 2
    lo = pltpu.bitcast(x[:, :half].astype(BF16).astype(F32), jnp.uint32)
    hi = pltpu.bitcast(x[:, half:].astype(BF16).astype(F32), jnp.uint32)
    return (hi & jnp.uint32(0xFFFF0000)) | (lo >> 16)


def _unpack_bf16_pairs(w):
    lo = pltpu.bitcast(w << 16, F32).astype(BF16)
    hi = pltpu.bitcast(w & jnp.uint32(0xFFFF0000), F32).astype(BF16)
    return jnp.concatenate([lo, hi], axis=1)


def _gather_body(tok_ref, base_ref, nrows_ref, hp_hbm, o_ref, buf, sem, *, tm, sub, unroll):
    i = pl.program_id(0)

    def issue_tile(t, slot):
        base = base_ref[t * (tm // sub)]

        def issue(g, c):
            for u in range(unroll):
                r = g * unroll + u
                _token_copy(hp_hbm, buf.at[slot], sem.at[slot], tok_ref[base + r], r).start()
            return c

        lax.fori_loop(0, tm // unroll, issue, 0)

    @pl.when(i == 0)
    def _():
        issue_tile(0, 0)

    nxt = i + 1

    @pl.when(jnp.logical_and(nxt < pl.num_programs(0), nxt * tm < nrows_ref[0]))
    def _():
        issue_tile(nxt, nxt % 2)

    used = i * tm < nrows_ref[0]

    @pl.when(jnp.logical_not(used))
    def _():
        o_ref[...] = jnp.zeros_like(o_ref)

    @pl.when(used)
    def _():
        slot = i % 2
        pltpu.make_async_copy(hp_hbm.at[pl.ds(0, tm)], buf.at[slot], sem.at[slot]).wait()
        o_ref[...] = _unpack_bf16_pairs(_load_token_major(buf[slot]))


def gather_rows(hp, sorted_tok, sub_base, n_used_rows, *, cap, tm, sub):
    half = hp.shape[1] * hp.shape[2]
    return pl.pallas_call(
        functools.partial(_gather_body, tm=tm, sub=sub, unroll=8),
        grid_spec=pltpu.PrefetchScalarGridSpec(
            num_scalar_prefetch=3,
            grid=(cap // tm,),
            in_specs=[pl.BlockSpec(memory_space=pl.ANY)],
            out_specs=pl.BlockSpec((tm, 2 * half), lambda i, tok, base, nused: (i, 0)),
            scratch_shapes=[pltpu.VMEM((2, tm) + hp.shape[1:], jnp.uint32), pltpu.SemaphoreType.DMA((2,))]),
        out_shape=jax.ShapeDtypeStruct((cap, 2 * half), BF16),
        compiler_params=_cparams(("arbitrary",)),
        name="gather_rows",
    )(sorted_tok, sub_base, n_used_rows, hp)


def _grouped_tile(i, n_rows, tm):
    return jnp.minimum(i, (n_rows[0] - 1) // tm)


def _grouped_fresh(i, sub_e, ratio):
    prev = sub_e[(jnp.maximum(i, 1) - 1) * ratio]
    return jnp.logical_or(i == 0, sub_e[i * ratio] != prev)


def _moe_up_body(sub_e, sub_n, n_rows, x_ref, wg_ref, wu_ref, o_ref, wgb_ref, wub_ref, *, tm, sub):
    i = pl.program_id(1)
    ratio = tm // sub
    used = i * tm < n_rows[0]

    @pl.when(jnp.logical_and(used, _grouped_fresh(i, sub_e, ratio)))
    def _():
        wgb_ref[...] = wg_ref[0].astype(BF16)
        wub_ref[...] = wu_ref[0].astype(BF16)

    def chunk(s):
        x = x_ref[s * sub:(s + 1) * sub, :]
        g = jnp.dot(x, wgb_ref[...], preferred_element_type=F32)
        u = jnp.dot(x, wub_ref[...], preferred_element_type=F32)
        o_ref[s * sub:(s + 1) * sub, :] = (_silu(g) * u).astype(o_ref.dtype)

    full = jnp.logical_and(used, sub_n[i * ratio + ratio - 1] > 0)

    @pl.when(full)
    def _():
        for s in range(ratio):
            chunk(s)

    for s in range(ratio):
        live = jnp.logical_and(used, sub_n[i * ratio + s] > 0)

        @pl.when(jnp.logical_and(live, jnp.logical_not(full)))
        def _():
            chunk(s)

        @pl.when(jnp.logical_not(live))
        def _():
            o_ref[s * sub:(s + 1) * sub, :] = jnp.zeros((sub, o_ref.shape[1]), o_ref.dtype)


def moe_swiglu_up(xs, w_gu, sub_e, sub_n, n_rows, *, tm, tf, sub):
    cap, d = xs.shape
    f = w_gu.shape[2] // 2
    nf = f // tf
    ratio = tm // sub

    def x_map(j, i, se, sn, nr):
        return (_grouped_tile(i, nr, tm), 0)

    def wg_map(j, i, se, sn, nr):
        return (se[_grouped_tile(i, nr, tm) * ratio], 0, j)

    def wu_map(j, i, se, sn, nr):
        return (se[_grouped_tile(i, nr, tm) * ratio], 0, nf + j)

    def o_map(j, i, se, sn, nr):
        return (i, j)

    return pl.pallas_call(
        functools.partial(_moe_up_body, tm=tm, sub=sub),
        grid_spec=pltpu.PrefetchScalarGridSpec(
            num_scalar_prefetch=3,
            grid=(nf, cap // tm),
            in_specs=[pl.BlockSpec((tm, d), x_map),
                      pl.BlockSpec((1, d, tf), wg_map),
                      pl.BlockSpec((1, d, tf), wu_map)],
            out_specs=pl.BlockSpec((tm, tf), o_map),
            scratch_shapes=[pltpu.VMEM((d, tf), BF16), pltpu.VMEM((d, tf), BF16)]),
        out_shape=jax.ShapeDtypeStruct((cap, f), BF16),
        compiler_params=_cparams(("arbitrary", "arbitrary")),
        name="moe_swiglu_up",
    )(sub_e, sub_n, n_rows, xs, w_gu, w_gu)


def _moe_down_body(sub_e, sub_n, n_rows, x_ref, w_ref, o_ref, wb_ref, *, tm, sub):
    i = pl.program_id(1)
    ratio = tm // sub
    used = i * tm < n_rows[0]

    @pl.when(jnp.logical_and(used, _grouped_fresh(i, sub_e, ratio)))
    def _():
        wb_ref[...] = w_ref[0].astype(BF16)

    def chunk(s):
        o_ref[s * sub:(s + 1) * sub, :] = jnp.dot(
            x_ref[s * sub:(s + 1) * sub, :], wb_ref[...], preferred_element_type=F32)

    full = jnp.logical_and(used, sub_n[i * ratio + ratio - 1] > 0)

    @pl.when(full)
    def _():
        for s in range(ratio):
            chunk(s)

    for s in range(ratio):
        live = jnp.logical_and(used, sub_n[i * ratio + s] > 0)

        @pl.when(jnp.logical_and(live, jnp.logical_not(full)))
        def _():
            chunk(s)

        @pl.when(jnp.logical_not(live))
        def _():
            o_ref[s * sub:(s + 1) * sub, :] = jnp.zeros((sub, o_ref.shape[1]), o_ref.dtype)


def moe_down(act, w_down, sub_e, sub_n, n_rows, *, tm, tn, sub):
    cap, f = act.shape
    d = w_down.shape[2]
    ratio = tm // sub

    def x_map(j, i, se, sn, nr):
        return (_grouped_tile(i, nr, tm), 0)

    def w_map(j, i, se, sn, nr):
        return (se[_grouped_tile(i, nr, tm) * ratio], 0, j)

    def o_map(j, i, se, sn, nr):
        return (i, j)

    return pl.pallas_call(
        functools.partial(_moe_down_body, tm=tm, sub=sub),
        grid_spec=pltpu.PrefetchScalarGridSpec(
            num_scalar_prefetch=3,
            grid=(d // tn, cap // tm),
            in_specs=[pl.BlockSpec((tm, f), x_map),
                      pl.BlockSpec((1, f, tn), w_map)],
            out_specs=pl.BlockSpec((tm, tn), o_map),
            scratch_shapes=[pltpu.VMEM((f, tn), BF16)]),
        out_shape=jax.ShapeDtypeStruct((cap, d), F32),
        compiler_params=_cparams(("arbitrary", "arbitrary")),
        name="moe_down",
    )(sub_e, sub_n, n_rows, act, w_down)


def _combine_body(dest_ref, y_hbm, gates_ref, res_ref, g_ref, b_ref, o_ref, buf, sem, *, tm, unroll):
    i = pl.program_id(0)

    def issue_tile(t, slot):
        def issue(g, c):
            for u in range(unroll):
                r = g * unroll + u
                for k in range(TOP_K):
                    _row_copy(y_hbm, buf.at[slot, k], sem.at[slot, k],
                              dest_ref[TOP_K * (t * tm + r) + k], r).start()
            return c

        lax.fori_loop(0, tm // unroll, issue, 0)

    @pl.when(i == 0)
    def _():
        issue_tile(0, 0)

    nxt = i + 1

    @pl.when(nxt < pl.num_programs(0))
    def _():
        issue_tile(nxt, nxt % 2)

    slot = i % 2
    gates = gates_ref[...]
    y = None
    for k in range(TOP_K):
        pltpu.make_async_copy(y_hbm.at[pl.ds(0, tm), :], buf.at[slot, k], sem.at[slot, k]).wait()
        term = buf[slot, k] * gates[:, TOP_K + k:TOP_K + k + 1]
        y = term if y is None else y + term
    o_ref[...] = _layer_norm_rows(DEEPNORM_ALPHA * res_ref[...] + y, g_ref[...], b_ref[...])


def moe_combine_post_norm(y_rows, dest, routing, res, g, b, *, tm):
    n_tok, d = res.shape
    lanes = routing.shape[1]
    return pl.pallas_call(
        functools.partial(_combine_body, tm=tm, unroll=4),
        grid_spec=pltpu.PrefetchScalarGridSpec(
            num_scalar_prefetch=1,
            grid=(n_tok // tm,),
            in_specs=[pl.BlockSpec(memory_space=pl.ANY),
                      pl.BlockSpec((tm, lanes), lambda i, dst: (i, 0)),
                      pl.BlockSpec((tm, d), lambda i, dst: (i, 0)),
                      pl.BlockSpec((1, d), lambda i, dst: (0, 0)),
                      pl.BlockSpec((1, d), lambda i, dst: (0, 0))],
            out_specs=pl.BlockSpec((tm, d), lambda i, dst: (i, 0)),
            scratch_shapes=[pltpu.VMEM((2, TOP_K, tm, d), F32), pltpu.SemaphoreType.DMA((2, TOP_K))]),
        out_shape=jax.ShapeDtypeStruct((n_tok, d), F32),
        compiler_params=_cparams(("arbitrary",)),
        name="moe_combine_post_norm",
    )(dest, y_rows, routing, res, g.reshape(1, d), b.reshape(1, d))


def _group_rows(seg, n_seg, tile, sub):
    n_asg = seg.shape[0]
    cap = n_asg + n_seg * tile
    onehot = (seg[:, None] == jnp.arange(n_seg, dtype=jnp.int32)[None, :]).astype(jnp.int32)
    csum = jnp.cumsum(onehot, axis=0)
    rank = jnp.sum(csum * onehot, axis=1) - 1
    counts = csum[-1]
    padded = (counts + tile - 1) // tile * tile
    pad_end = jnp.cumsum(padded)
    pad_start = pad_end - padded
    dest = (pad_start[seg] + rank).astype(jnp.int32)
    order = jnp.sort(seg * n_asg + jnp.arange(n_asg, dtype=jnp.int32))
    sorted_tok = jnp.pad((order % n_asg) // TOP_K, (0, tile)).astype(jnp.int32)
    sub_start = jnp.arange(cap // sub, dtype=jnp.int32) * sub
    sub_e = jnp.sum((pad_end[None, :] <= sub_start[:, None]).astype(jnp.int32), axis=1)
    sub_e = jnp.minimum(sub_e, n_seg - 1)
    sub_n = jnp.clip(counts[sub_e] - (sub_start - pad_start[sub_e]), 0, sub).astype(jnp.int32)
    start = jnp.cumsum(counts) - counts
    sub_base = (sub_start - (pad_start - start)[sub_e]).astype(jnp.int32)
    n_rows = pad_end[-1:].astype(jnp.int32)
    return dest, sorted_tok, sub_base, sub_e, sub_n, n_rows, cap


def _gla_layer(h, hb, w_in, w_gate_up, b_gate, g_norm, w_o, ln_g, ln_b):
    bsz, s, d = h.shape
    dk = (d // 2) // GLA_HEADS
    dv = d // GLA_HEADS
    n_main = 2 * GLA_HEADS * dk + 2 * GLA_HEADS * dv
    x2 = hb.reshape(bsz * s, d)
    n_qk = 2 * GLA_HEADS * dk
    tm = _pick(bsz * s, (1024, 512, 256, 128, 64))
    tn = _pick(n_qk, (1024, 512, 256, 128))
    qk = matmul(x2, w_in, tm=tm, tn=tn, out_dtype=F32, col_start=0, n_cols=n_qk)
    vr = matmul(x2, w_in, tm=tm, tn=tn, out_dtype=F32, col_start=n_qk, n_cols=n_main - n_qk)
    w_z = jnp.pad(w_in[:, n_main:], ((0, 0), (0, LANES - GLA_GATE_RANK)))
    z = matmul(x2, w_z, tm=tm, tn=LANES, out_dtype=F32)
    wgu = jnp.pad(w_gate_up, ((0, LANES - GLA_GATE_RANK), (0, 0))).astype(BF16)
    wgu = wgu.reshape(LANES, GLA_HEADS, dk).transpose(1, 0, 2)
    og = gla_core(qk.reshape(bsz, s, n_qk), vr.reshape(bsz, s, n_main - n_qk), z.reshape(bsz, s, LANES), wgu,
                  b_gate.reshape(GLA_HEADS, 1, dk), g_norm.reshape(1, dv),
                  heads=GLA_HEADS, dk=dk, dv=dv, step=min(GLA_STEP_TOKENS, s))
    out, outb = matmul_post_norm(og.reshape(bsz * s, d), w_o, h.reshape(bsz * s, d), ln_g, ln_b,
                                 tm=512, sub=256)
    return out.reshape(bsz, s, d), outb.reshape(bsz, s, d)


def _moba_layer(h, hb, w_q, w_kv, w_o, ln_g, ln_b):
    bsz, s, d = h.shape
    hd = d // MOBA_HEADS
    x2 = hb.reshape(bsz * s, d)
    tm = _pick(bsz * s, (1024, 512, 256, 128, 64))
    n_kv = w_kv.shape[1]
    q = matmul(x2, w_q, tm=tm, tn=_pick(d, (1024, 512, 256, 128)), out_dtype=BF16, scale=hd ** -0.5 * LOG2_E)
    kv = matmul(x2, w_kv, tm=tm, tn=_pick(n_kv, (1024, 512, 256, 128)), out_dtype=BF16)
    att = moba_attention(q.reshape(bsz, s, d), kv.reshape(bsz, s, n_kv), heads=MOBA_HEADS,
                         kv_heads=MOBA_KV_HEADS, hd=hd, blk=MOBA_BLOCK, topk=MOBA_TOPK)
    out, outb, outp = matmul_post_norm(att.reshape(bsz * s, d), w_o, h.reshape(bsz * s, d),
                                       ln_g, ln_b, tm=512, sub=256, packed=True)
    return out.reshape(bsz, s, d), outb.reshape(bsz, s, d), outp


def _dense_ffn(h, hb, w_gu, w_down, ln_g, ln_b):
    bsz, s, d = h.shape
    f = w_down.shape[0]
    tm = _pick(bsz * s, (1024, 512, 256, 128, 64))
    act = swiglu_up(hb.reshape(bsz * s, d), w_gu, tm=tm, tf=512, sub=tm)
    out, outb = matmul_post_norm(act, w_down.astype(BF16), h.reshape(bsz * s, d), ln_g, ln_b, tm=512, sub=256)
    return out.reshape(bsz, s, d), outb.reshape(bsz, s, d)


def _moe_ffn(h, hb, hp, w_router, w_gu, w_down, ln_g, ln_b):
    bsz, s, d = h.shape
    n_tok = bsz * s
    h2 = h.reshape(n_tok, d)
    w_r = jnp.pad(w_router, ((0, 0), (0, LANES - N_EXPERTS))).astype(BF16)
    routing = router_top2(hb.reshape(n_tok, d), w_r, tm=_pick(n_tok, (1024, 512, 256, 128, 64)),
                          n_experts=N_EXPERTS)
    seg = routing[:, :TOP_K].astype(jnp.int32).reshape(-1)
    tile = min(MOE_TILE, n_tok)
    sub = min(MOE_SUB, tile)
    dest, sorted_tok, sub_base, sub_e, sub_n, n_rows, cap = _group_rows(seg, N_EXPERTS, tile, sub)
    xs = gather_rows(hp, sorted_tok, sub_base, n_rows, cap=cap, tm=min(GATHER_TILE, tile), sub=sub)
    f = w_down.shape[1]
    act = moe_swiglu_up(xs, w_gu, sub_e, sub_n, n_rows, tm=tile, tf=_pick(f, (512, 256, 128)), sub=sub)
    y_rows = moe_down(act, w_down, sub_e, sub_n, n_rows, tm=min(MOE_DOWN_TILE, tile),
                      tn=_pick(d, (512, 256, 128)), sub=sub)
    out = moe_combine_post_norm(y_rows, dest, routing, h2, ln_g, ln_b, tm=_pick(n_tok, (512, 256, 128, 64)))
    return out.reshape(bsz, s, d)


def kernel(x, w_in_a, w_gate_up_a, b_gate_a, g_norm_a, w_o_a, w_kv_shared, w_q_b, w_o_b,
           ln_mix_g, ln_mix_b, w_gu_dense, w_down_dense, w_router, w_gu_moe, w_down_moe,
           ln_ffn_g, ln_ffn_b):
    h, hb = _gla_layer(x, x.astype(BF16), w_in_a[0], w_gate_up_a[0], b_gate_a[0], g_norm_a[0], w_o_a[0],
                       ln_mix_g[0], ln_mix_b[0])
    h, hb = _dense_ffn(h, hb, w_gu_dense[0], w_down_dense[0], ln_ffn_g[0], ln_ffn_b[0])
    h, hb, hp = _moba_layer(h, hb, w_q_b[0], w_kv_shared, w_o_b[0], ln_mix_g[1], ln_mix_b[1])
    return _moe_ffn(h, hb, hp, w_router[0], w_gu_moe[0], w_down_moe[0], ln_ffn_g[1], ln_ffn_b[1])
```

```python
import functools
import math

import jax
import jax.numpy as jnp
from jax import lax
from jax.experimental import pallas as pl
from jax.experimental.pallas import tpu as pltpu

F32 = jnp.float32
BF16 = jnp.bfloat16

DEPTH = 2
GLA_HEADS = 4
GLA_GATE_RANK = 16
GLA_TAU = 16.0
GLA_CHUNK = 64
MOBA_HEADS = 16
MOBA_KV_HEADS = 4
MOBA_BLOCK = 256
MOBA_TOPK = 3
N_EXPERTS = 8
TOP_K = 2
LN_EPS = 1e-5
RMS_EPS = 1e-6
DEEPNORM_ALPHA = (2 * DEPTH) ** 0.25

LANES = 128
VMEM_LIMIT_BYTES = 56 * 1024 * 1024

GLA_GROUP = 8
GLA_STEP_TOKENS = 1024
MOE_TILE = 1024
MOE_SUB = 256
MOE_DOWN_TILE = 512
GATHER_TILE = 1024


def _cparams(sem):
    return pltpu.CompilerParams(dimension_semantics=sem, vmem_limit_bytes=VMEM_LIMIT_BYTES)


def _pick(n, prefs):
    for p in prefs:
        if n % p == 0:
            return p
    return n


def _mm_body(x_ref, w_ref, o_ref, wb_ref, *, scale):
    @pl.when(pl.program_id(1) == 0)
    def _():
        wb_ref[...] = w_ref[...].astype(BF16)

    acc = jnp.dot(x_ref[...], wb_ref[...], preferred_element_type=F32)
    if scale != 1.0:
        acc = acc * scale
    o_ref[...] = acc.astype(o_ref.dtype)


def matmul(x, w, *, tm, tn, out_dtype, col_start=0, n_cols=None, scale=1.0):
    m, k = x.shape
    n_cols = w.shape[1] - col_start if n_cols is None else n_cols
    assert col_start % tn == 0 and n_cols % tn == 0
    first = col_start // tn
    return pl.pallas_call(
        functools.partial(_mm_body, scale=scale),
        grid=(n_cols // tn, m // tm),
        in_specs=[pl.BlockSpec((tm, k), lambda j, i: (i, 0)),
                  pl.BlockSpec((k, tn), lambda j, i: (0, first + j))],
        out_specs=pl.BlockSpec((tm, tn), lambda j, i: (i, j)),
        out_shape=jax.ShapeDtypeStruct((m, n_cols), out_dtype),
        scratch_shapes=[pltpu.VMEM((k, tn), BF16)],
        compiler_params=_cparams(("arbitrary", "arbitrary")),
        name="matmul",
    )(x, w)


def _layer_norm_rows(t, g, b):
    mu = jnp.mean(t, axis=-1, keepdims=True)
    d = t - mu
    var = jnp.mean(d * d, axis=-1, keepdims=True)
    return d * lax.rsqrt(var + LN_EPS) * g + b


def _mm_ln_body(x_ref, w_ref, res_ref, g_ref, b_ref, o_ref, ob_ref, *rest, tm, sub, packed, cast):
    if cast:
        wb_ref = rest[-1]

        @pl.when(pl.program_id(0) == 0)
        def _():
            wb_ref[...] = w_ref[...].astype(BF16)
    else:
        wb_ref = w_ref
    for s in range(tm // sub):
        rows = slice(s * sub, (s + 1) * sub)
        y = jnp.dot(x_ref[rows, :], wb_ref[...], preferred_element_type=F32)
        out = _layer_norm_rows(DEEPNORM_ALPHA * res_ref[rows, :] + y, g_ref[...], b_ref[...])
        o_ref[rows, :] = out
        ob_ref[rows, :] = out.astype(BF16)
        if packed:
            _store_token_major(rest[0], rows, _pack_bf16_pairs(out))


def matmul_post_norm(x, w, res, g, b, *, tm, sub, packed=False):
    m, k = x.shape
    d = w.shape[1]
    cast = w.dtype != BF16
    out_specs = [pl.BlockSpec((tm, d), lambda i: (i, 0)), pl.BlockSpec((tm, d), lambda i: (i, 0))]
    out_shape = [jax.ShapeDtypeStruct((m, d), F32), jax.ShapeDtypeStruct((m, d), BF16)]
    if packed:
        out_specs.append(pl.BlockSpec((tm, d // 2 // LANES, LANES), lambda i: (i, 0, 0)))
        out_shape.append(jax.ShapeDtypeStruct((m, d // 2 // LANES, LANES), jnp.uint32))
    return pl.pallas_call(
        functools.partial(_mm_ln_body, tm=tm, sub=sub, packed=packed, cast=cast),
        grid=(m // tm,),
        scratch_shapes=[pltpu.VMEM((k, d), BF16)] if cast else [],
        in_specs=[pl.BlockSpec((tm, k), lambda i: (i, 0)),
                  pl.BlockSpec((k, d), lambda i: (0, 0), pipeline_mode=pl.Buffered(1)),
                  pl.BlockSpec((tm, d), lambda i: (i, 0)),
                  pl.BlockSpec((1, d), lambda i: (0, 0)),
                  pl.BlockSpec((1, d), lambda i: (0, 0))],
        out_specs=out_specs,
        out_shape=out_shape,
        compiler_params=_cparams(("arbitrary",)),
        name="matmul_post_norm",
    )(x, w, res, g.reshape(1, d), b.reshape(1, d))


def _silu(x):
    return x / (1.0 + jnp.exp(-x))


def _swiglu_body(x_ref, wg_ref, *rest, f, tm, tf, sub):
    n_u = tf // LANES
    wu_refs, o_ref, wgb_ref, wub_ref = rest[:n_u], rest[n_u], rest[n_u + 1], rest[n_u + 2]
    j = pl.program_id(0)

    @pl.when(pl.program_id(1) == 0)
    def _():
        col = j * tf + lax.broadcasted_iota(jnp.int32, wg_ref.shape, 1)
        wgb_ref[...] = jnp.where(col < f, wg_ref[...], 0.0).astype(BF16)
        for c, wu_ref in enumerate(wu_refs):
            keep = j * tf + c * LANES < f
            wub_ref[:, c * LANES:(c + 1) * LANES] = jnp.where(keep, wu_ref[...], 0.0).astype(BF16)

    for s in range(tm // sub):
        x = x_ref[s * sub:(s + 1) * sub, :]
        g = jnp.dot(x, wgb_ref[...], preferred_element_type=F32)
        u = jnp.dot(x, wub_ref[...], preferred_element_type=F32)
        o_ref[s * sub:(s + 1) * sub, :] = (_silu(g) * u).astype(o_ref.dtype)


def swiglu_up(x, w_gu, *, tm, tf, sub):
    m, k = x.shape
    f = w_gu.shape[1] // 2
    assert f % LANES == 0 and tf % LANES == 0
    nf = pl.cdiv(f, tf)
    n_u = tf // LANES
    last_u = 2 * f // LANES - 1

    def u_map(c):
        return lambda j, i: (0, jnp.minimum(f // LANES + j * n_u + c, last_u))

    return pl.pallas_call(
        functools.partial(_swiglu_body, f=f, tm=tm, tf=tf, sub=sub),
        grid=(nf, m // tm),
        in_specs=[pl.BlockSpec((tm, k), lambda j, i: (i, 0)),
                  pl.BlockSpec((k, tf), lambda j, i: (0, j))]
        + [pl.BlockSpec((k, LANES), u_map(c)) for c in range(n_u)],
        out_specs=pl.BlockSpec((tm, tf), lambda j, i: (i, j)),
        out_shape=jax.ShapeDtypeStruct((m, f), BF16),
        scratch_shapes=[pltpu.VMEM((k, tf), BF16), pltpu.VMEM((k, tf), BF16)],
        compiler_params=_cparams(("arbitrary", "arbitrary")),
        name="swiglu_up",
    )(x, w_gu, *([w_gu] * n_u))


def _gla_body(q_ref, k_ref, v_ref, r_ref, z_ref, wgu_ref, bg_ref, gn_ref, o_ref, state_ref,
              *, chunk, group, n_groups, dk):
    @pl.when(pl.program_id(2) == 0)
    def _():
        state_ref[...] = jnp.zeros_like(state_ref)

    gt = group * chunk
    row = lax.broadcasted_iota(jnp.int32, (gt, gt), 0)
    col = lax.broadcasted_iota(jnp.int32, (gt, gt), 1)
    causal = jnp.logical_and(row // chunk == col // chunk, col <= row)
    tril = jnp.where(causal, 1.0, 0.0).astype(BF16)
    nt = (((1,), (1,)), ((), ()))
    tn = (((0,), (0,)), ((), ()))

    def body(g, carry):
        sl = pl.ds(pl.multiple_of(g * gt, gt), gt)
        gate = jnp.dot(z_ref[0, sl, :].astype(BF16), wgu_ref[0], preferred_element_type=F32) + bg_ref[0]
        log_a = (jnp.minimum(gate, 0.0) - jnp.log(1.0 + jnp.exp(-jnp.abs(gate)))) / GLA_TAU
        la_hi = log_a.astype(BF16)
        la_lo = (log_a - la_hi.astype(F32)).astype(BF16)
        b_cum = (jnp.dot(tril, la_hi, preferred_element_type=F32)
                 + jnp.dot(tril, la_lo, preferred_element_type=F32))
        b_last = [b_cum[(c + 1) * chunk - 1:(c + 1) * chunk, :] for c in range(group)]
        b_last_rows = jnp.concatenate([jnp.broadcast_to(b, (chunk, b.shape[1])) for b in b_last], axis=0)
        total = [jnp.zeros_like(b_last[0])]
        for c in range(group):
            total.append(total[c] + b_last[c])

        def per_chunk_rows(rows_of):
            return jnp.concatenate([jnp.broadcast_to(rows_of(c), (chunk, dk)) for c in range(group)], axis=0)

        q = q_ref[0, sl, :] * (dk ** -0.5)
        k = k_ref[0, sl, :]
        q_dec = q * jnp.exp(b_cum)
        q_dec_b = q_dec.astype(BF16)
        k_inv = (k * jnp.exp(-b_cum)).astype(BF16)
        k_end = k * jnp.exp(b_last_rows - b_cum)
        v = v_ref[0, sl, :].astype(BF16)
        att = lax.dot_general(q_dec_b, k_inv, nt, preferred_element_type=F32)
        att = jnp.where(causal, att, 0.0).astype(BF16)
        o = jnp.dot(att, v, preferred_element_type=F32)
        state_t = state_ref[...]
        q_grp = (q_dec * jnp.exp(per_chunk_rows(lambda c: total[c]))).astype(BF16)
        o = o + lax.dot_general(q_grp, state_t.astype(BF16), nt, preferred_element_type=F32)
        outs = [o[:chunk]]
        for c in range(1, group):
            keys = jnp.concatenate([k_end[p * chunk:(p + 1) * chunk] * jnp.exp(total[c] - total[p + 1])
                                    for p in range(c)], axis=0).astype(BF16)
            a_c = lax.dot_general(q_dec_b[c * chunk:(c + 1) * chunk], keys, nt, preferred_element_type=F32)
            outs.append(o[c * chunk:(c + 1) * chunk]
                        + jnp.dot(a_c.astype(BF16), v[:c * chunk], preferred_element_type=F32))
        o = jnp.concatenate(outs, axis=0)
        k_grp = (k_end * jnp.exp(per_chunk_rows(lambda c: total[group] - total[c + 1]))).astype(BF16)
        state_ref[...] = (jnp.exp(total[group]) * state_t
                          + lax.dot_general(v, k_grp, tn, preferred_element_type=F32))
        o = o * lax.rsqrt(jnp.mean(o * o, axis=-1, keepdims=True) + RMS_EPS) * gn_ref[...]
        o_ref[0, sl, :] = (o * _silu(r_ref[0, sl, :].astype(F32))).astype(o_ref.dtype)
        return carry

    lax.fori_loop(0, n_groups, body, 0)


def gla_core(qk, vr, z, wgu, bg, gn, *, heads, dk, dv, step):
    bsz, s, _ = qk.shape
    zr = z.shape[-1]
    k_off = heads
    r_off = heads
    return pl.pallas_call(
        functools.partial(_gla_body, chunk=GLA_CHUNK, group=GLA_GROUP,
                          n_groups=step // (GLA_GROUP * GLA_CHUNK), dk=dk),
        grid=(bsz, heads, s // step),
        in_specs=[pl.BlockSpec((1, step, dk), lambda b, h, t: (b, t, h)),
                  pl.BlockSpec((1, step, dk), lambda b, h, t: (b, t, k_off + h)),
                  pl.BlockSpec((1, step, dv), lambda b, h, t: (b, t, h)),
                  pl.BlockSpec((1, step, dv), lambda b, h, t: (b, t, r_off + h)),
                  pl.BlockSpec((1, step, zr), lambda b, h, t: (b, t, 0)),
                  pl.BlockSpec((1, zr, dk), lambda b, h, t: (h, 0, 0)),
                  pl.BlockSpec((1, 1, dk), lambda b, h, t: (h, 0, 0)),
                  pl.BlockSpec((1, dv), lambda b, h, t: (0, 0))],
        out_specs=pl.BlockSpec((1, step, dv), lambda b, h, t: (b, t, h)),
        out_shape=jax.ShapeDtypeStruct((bsz, s, heads * dv), BF16),
        scratch_shapes=[pltpu.VMEM((dv, dk), F32)],
        compiler_params=_cparams(("parallel", "parallel", "arbitrary")),
        name="gla_core",
    )(qk, qk, vr, vr, z, wgu, bg, gn)


MASK_PENALTY = -3.0e38
LOG2_E = math.log2(math.e)
SUM_ROWS = 16


def _moba_body(q_ref, k_ref, v_ref, o_ref, kmean_ref, kaug_ref, vt_ref, rhs_ref, m_ref, acc_ref,
               *, blk, n_blk, rep, hd, topk):
    i = pl.program_id(2)
    nq = rep * blk

    @pl.when(i == 0)
    def _():
        lane = lax.broadcasted_iota(jnp.int32, (blk, hd), 1)
        for j in range(n_blk):
            kj = k_ref[0, j * blk:(j + 1) * blk, :]
            kaug_ref[j, :, :hd] = kj
            kaug_ref[j, :, hd:] = jnp.where(lane == j, 1.0, 0.0).astype(BF16)
            kmean_ref[j:j + 1, :] = jnp.sum(kj.astype(F32), axis=0, keepdims=True) / blk
            vt_ref[j, :hd, :] = v_ref[0, j * blk:(j + 1) * blk, :].astype(F32).T.astype(BF16)
            vt_ref[j, hd:, :] = jnp.ones((SUM_ROWS, blk), BF16)
        rhs_ref[hd + n_blk:, :] = jnp.zeros((hd - n_blk, nq), BF16)

    qt = jnp.concatenate([q_ref[0, :, r * hd:(r + 1) * hd].astype(F32).T for r in range(rep)], axis=1)
    qt = qt.astype(BF16)
    gate = jnp.dot(kmean_ref[...].astype(BF16), qt, preferred_element_type=F32)
    blk_id = lax.broadcasted_iota(jnp.int32, (n_blk, nq), 0)
    past = blk_id < i
    gate = jnp.where(past, gate, -jnp.inf)
    beaten = jnp.zeros((n_blk, nq), jnp.int32)
    for c in range(n_blk):
        gc = gate[c:c + 1, :]
        beaten = beaten + jnp.where(gc > gate, 1, jnp.where(gc == gate, jnp.where(c < blk_id, 1, 0), 0))
    penalty = jnp.where(past, jnp.where(beaten < topk, 0.0, MASK_PENALTY),
                        jnp.where(blk_id == i, 0.0, MASK_PENALTY))
    rhs_ref[:hd, :] = qt
    rhs_ref[hd:hd + n_blk, :] = penalty.astype(BF16)

    m_ref[...] = jnp.full((1, nq), -jnp.inf, F32)
    acc_ref[...] = jnp.zeros_like(acc_ref)
    group = 8

    def attend(j0, n, own):
        keys = kaug_ref[pl.ds(j0, n)].reshape(n * blk, 2 * hd)
        s = jnp.dot(keys, rhs_ref[...], preferred_element_type=F32)
        if own:
            key = lax.broadcasted_iota(jnp.int32, (blk, nq), 0)
            qpos = lax.rem(lax.broadcasted_iota(jnp.int32, (blk, nq), 1), blk)
            tail = jnp.where(key <= qpos, s[(n - 1) * blk:], -jnp.inf)
            s = tail if n == 1 else jnp.concatenate([s[:(n - 1) * blk], tail], axis=0)
        m_old = m_ref[...]
        m_new = jnp.maximum(m_old, jnp.max(s, axis=0, keepdims=True))
        p = jnp.exp2(s - m_new).astype(BF16)
        upd = jnp.exp2(m_old - m_new) * acc_ref[...]
        for c in range(n):
            upd = upd + jnp.dot(vt_ref[j0 + c], p[c * blk:(c + 1) * blk], preferred_element_type=F32)
        acc_ref[...] = upd
        m_ref[...] = m_new

    n_last = i + 1 - (i // group) * group

    def full_group(jj, carry):
        attend(group * jj, group, False)
        return carry

    lax.fori_loop(0, i // group, full_group, 0)
    for n in range(1, group + 1):
        @pl.when(n_last == n)
        def _():
            attend(i + 1 - n, n, True)

    acc = acc_ref[...]
    out = acc[:hd] * (1.0 / acc[hd:hd + 1])
    for r in range(rep):
        o_ref[0, :, r * hd:(r + 1) * hd] = out[:, r * blk:(r + 1) * blk].T.astype(o_ref.dtype)


def moba_attention(q, kv, *, heads, kv_heads, hd, blk, topk):
    bsz, s, _ = q.shape
    rep = heads // kv_heads
    n_blk = s // blk
    nq = rep * blk
    assert n_blk <= hd and n_blk % 16 == 0
    v_off = kv_heads
    return pl.pallas_call(
        functools.partial(_moba_body, blk=blk, n_blk=n_blk, rep=rep, hd=hd, topk=topk),
        grid=(bsz, kv_heads, n_blk),
        in_specs=[pl.BlockSpec((1, blk, rep * hd), lambda b, g, i: (b, i, g)),
                  pl.BlockSpec((1, s, hd), lambda b, g, i: (b, 0, g)),
                  pl.BlockSpec((1, s, hd), lambda b, g, i: (b, 0, v_off + g))],
        out_specs=pl.BlockSpec((1, blk, rep * hd), lambda b, g, i: (b, i, g)),
        out_shape=jax.ShapeDtypeStruct((bsz, s, heads * hd), BF16),
        scratch_shapes=[pltpu.VMEM((n_blk, hd), F32),
                        pltpu.VMEM((n_blk, blk, 2 * hd), BF16),
                        pltpu.VMEM((n_blk, hd + SUM_ROWS, blk), BF16),
                        pltpu.VMEM((2 * hd, nq), BF16),
                        pltpu.VMEM((1, nq), F32),
                        pltpu.VMEM((hd + SUM_ROWS, nq), F32)],
        compiler_params=_cparams(("parallel", "parallel", "arbitrary")),
        name="moba_attention",
    )(q, kv, kv)


def _router_body(x_ref, w_ref, o_ref, *, n_experts):
    logits = jnp.dot(x_ref[...], w_ref[...], preferred_element_type=F32)
    lane = lax.broadcasted_iota(jnp.int32, logits.shape, 1)
    lane_f = lane.astype(F32)
    big = float(logits.shape[1])
    lg = jnp.where(lane < n_experts, logits, -jnp.inf)
    m1 = jnp.max(lg, axis=-1, keepdims=True)
    i1 = jnp.min(jnp.where(lg == m1, lane_f, big), axis=-1, keepdims=True)
    lg2 = jnp.where(lane_f == i1, -jnp.inf, lg)
    m2 = jnp.max(lg2, axis=-1, keepdims=True)
    i2 = jnp.min(jnp.where(lg2 == m2, lane_f, big), axis=-1, keepdims=True)
    t = jnp.exp(m2 - m1)
    g1 = 1.0 / (1.0 + t)
    g2 = t / (1.0 + t)
    o_ref[...] = jnp.where(lane == 0, i1,
                           jnp.where(lane == 1, i2, jnp.where(lane == 2, g1, jnp.where(lane == 3, g2, 0.0))))


def router_top2(x, w_pad, *, tm, n_experts):
    m, k = x.shape
    n = w_pad.shape[1]
    return pl.pallas_call(
        functools.partial(_router_body, n_experts=n_experts),
        grid=(m // tm,),
        in_specs=[pl.BlockSpec((tm, k), lambda i: (i, 0)),
                  pl.BlockSpec((k, n), lambda i: (0, 0))],
        out_specs=pl.BlockSpec((tm, n), lambda i: (i, 0)),
        out_shape=jax.ShapeDtypeStruct((m, n), F32),
        compiler_params=_cparams(("parallel",)),
        name="router_top2",
    )(x, w_pad)


def _row_copy(src_hbm, dst_vmem, sem, src_row, dst_row):
    return pltpu.make_async_copy(src_hbm.at[pl.ds(src_row, 1), :], dst_vmem.at[pl.ds(dst_row, 1), :], sem)


def _token_copy(src_hbm, dst_vmem, sem, src_row, dst_row):
    return pltpu.make_async_copy(src_hbm.at[src_row], dst_vmem.at[dst_row], sem)


def _store_token_major(ref, rows, x):
    for c in range(x.shape[1] // LANES):
        ref[rows, c, :] = x[:, c * LANES:(c + 1) * LANES]


def _load_token_major(x):
    return jnp.concatenate([x[:, c, :] for c in range(x.shape[1])], axis=1)


def _pack_bf16_pairs(x):
    half = x.shape[1] // 2
    lo = pltpu.bitcast(x[:, :half].astype(BF16).astype(F32), jnp.uint32)
    hi = pltpu.bitcast(x[:, half:].astype(BF16).astype(F32), jnp.uint32)
    return (hi & jnp.uint32(0xFFFF0000)) | (lo >> 16)


def _unpack_bf16_pairs(w):
    lo = pltpu.bitcast(w << 16, F32).astype(BF16)
    hi = pltpu.bitcast(w & jnp.uint32(0xFFFF0000), F32).astype(BF16)
    return jnp.concatenate([lo, hi], axis=1)


def _gather_body(tok_ref, base_ref, nrows_ref, hp_hbm, o_ref, buf, sem, *, tm, sub, unroll):
    i = pl.program_id(0)

    def issue_tile(t, slot):
        base = base_ref[t * (tm // sub)]

        def issue(g, c):
            for u in range(unroll):
                r = g * unroll + u
                _token_copy(hp_hbm, buf.at[slot], sem.at[slot], tok_ref[base + r], r).start()
            return c

        lax.fori_loop(0, tm // unroll, issue, 0)

    @pl.when(i == 0)
    def _():
        issue_tile(0, 0)

    nxt = i + 1

    @pl.when(jnp.logical_and(nxt < pl.num_programs(0), nxt * tm < nrows_ref[0]))
    def _():
        issue_tile(nxt, nxt % 2)

    used = i * tm < nrows_ref[0]

    @pl.when(jnp.logical_not(used))
    def _():
        o_ref[...] = jnp.zeros_like(o_ref)

    @pl.when(used)
    def _():
        slot = i % 2
        pltpu.make_async_copy(hp_hbm.at[pl.ds(0, tm)], buf.at[slot], sem.at[slot]).wait()
        o_ref[...] = _unpack_bf16_pairs(_load_token_major(buf[slot]))


def gather_rows(hp, sorted_tok, sub_base, n_used_rows, *, cap, tm, sub):
    half = hp.shape[1] * hp.shape[2]
    return pl.pallas_call(
        functools.partial(_gather_body, tm=tm, sub=sub, unroll=8),
        grid_spec=pltpu.PrefetchScalarGridSpec(
            num_scalar_prefetch=3,
            grid=(cap // tm,),
            in_specs=[pl.BlockSpec(memory_space=pl.ANY)],
            out_specs=pl.BlockSpec((tm, 2 * half), lambda i, tok, base, nused: (i, 0)),
            scratch_shapes=[pltpu.VMEM((2, tm) + hp.shape[1:], jnp.uint32), pltpu.SemaphoreType.DMA((2,))]),
        out_shape=jax.ShapeDtypeStruct((cap, 2 * half), BF16),
        compiler_params=_cparams(("arbitrary",)),
        name="gather_rows",
    )(sorted_tok, sub_base, n_used_rows, hp)


def _grouped_tile(i, n_rows, tm):
    return jnp.minimum(i, (n_rows[0] - 1) // tm)


def _grouped_fresh(i, sub_e, ratio):
    prev = sub_e[(jnp.maximum(i, 1) - 1) * ratio]
    return jnp.logical_or(i == 0, sub_e[i * ratio] != prev)


def _moe_up_body(sub_e, sub_n, n_rows, x_ref, wg_ref, wu_ref, o_ref, wgb_ref, wub_ref, *, tm, sub):
    i = pl.program_id(1)
    ratio = tm // sub
    used = i * tm < n_rows[0]

    @pl.when(jnp.logical_and(used, _grouped_fresh(i, sub_e, ratio)))
    def _():
        wgb_ref[...] = wg_ref[0].astype(BF16)
        wub_ref[...] = wu_ref[0].astype(BF16)

    def chunk(s):
        x = x_ref[s * sub:(s + 1) * sub, :]
        g = jnp.dot(x, wgb_ref[...], preferred_element_type=F32)
        u = jnp.dot(x, wub_ref[...], preferred_element_type=F32)
        o_ref[s * sub:(s + 1) * sub, :] = (_silu(g) * u).astype(o_ref.dtype)

    full = jnp.logical_and(used, sub_n[i * ratio + ratio - 1] > 0)

    @pl.when(full)
    def _():
        for s in range(ratio):
            chunk(s)

    for s in range(ratio):
        live = jnp.logical_and(used, sub_n[i * ratio + s] > 0)

        @pl.when(jnp.logical_and(live, jnp.logical_not(full)))
        def _():
            chunk(s)

        @pl.when(jnp.logical_not(live))
        def _():
            o_ref[s * sub:(s + 1) * sub, :] = jnp.zeros((sub, o_ref.shape[1]), o_ref.dtype)


def moe_swiglu_up(xs, w_gu, sub_e, sub_n, n_rows, *, tm, tf, sub):
    cap, d = xs.shape
    f = w_gu.shape[2] // 2
    nf = f // tf
    ratio = tm // sub

    def x_map(j, i, se, sn, nr):
        return (_grouped_tile(i, nr, tm), 0)

    def wg_map(j, i, se, sn, nr):
        return (se[_grouped_tile(i, nr, tm) * ratio], 0, j)

    def wu_map(j, i, se, sn, nr):
        return (se[_grouped_tile(i, nr, tm) * ratio], 0, nf + j)

    def o_map(j, i, se, sn, nr):
        return (i, j)

    return pl.pallas_call(
        functools.partial(_moe_up_body, tm=tm, sub=sub),
        grid_spec=pltpu.PrefetchScalarGridSpec(
            num_scalar_prefetch=3,
            grid=(nf, cap // tm),
            in_specs=[pl.BlockSpec((tm, d), x_map),
                      pl.BlockSpec((1, d, tf), wg_map),
                      pl.BlockSpec((1, d, tf), wu_map)],
            out_specs=pl.BlockSpec((tm, tf), o_map),
            scratch_shapes=[pltpu.VMEM((d, tf), BF16), pltpu.VMEM((d, tf), BF16)]),
        out_shape=jax.ShapeDtypeStruct((cap, f), BF16),
        compiler_params=_cparams(("arbitrary", "arbitrary")),
        name="moe_swiglu_up",
    )(sub_e, sub_n, n_rows, xs, w_gu, w_gu)


def _moe_down_body(sub_e, sub_n, n_rows, x_ref, w_ref, o_ref, wb_ref, *, tm, sub):
    i = pl.program_id(1)
    ratio = tm // sub
    used = i * tm < n_rows[0]

    @pl.when(jnp.logical_and(used, _grouped_fresh(i, sub_e, ratio)))
    def _():
        wb_ref[...] = w_ref[0].astype(BF16)

    def chunk(s):
        o_ref[s * sub:(s + 1) * sub, :] = jnp.dot(
            x_ref[s * sub:(s + 1) * sub, :], wb_ref[...], preferred_element_type=F32)

    full = jnp.logical_and(used, sub_n[i * ratio + ratio - 1] > 0)

    @pl.when(full)
    def _():
        for s in range(ratio):
            chunk(s)

    for s in range(ratio):
        live = jnp.logical_and(used, sub_n[i * ratio + s] > 0)

        @pl.when(jnp.logical_and(live, jnp.logical_not(full)))
        def _():
            chunk(s)

        @pl.when(jnp.logical_not(live))
        def _():
            o_ref[s * sub:(s + 1) * sub, :] = jnp.zeros((sub, o_ref.shape[1]), o_ref.dtype)


def moe_down(act, w_down, sub_e, sub_n, n_rows, *, tm, tn, sub):
    cap, f = act.shape
    d = w_down.shape[2]
    ratio = tm // sub

    def x_map(j, i, se, sn, nr):
        return (_grouped_tile(i, nr, tm), 0)

    def w_map(j, i, se, sn, nr):
        return (se[_grouped_tile(i, nr, tm) * ratio], 0, j)

    def o_map(j, i, se, sn, nr):
        return (i, j)

    return pl.pallas_call(
        functools.partial(_moe_down_body, tm=tm, sub=sub),
        grid_spec=pltpu.PrefetchScalarGridSpec(
            num_scalar_prefetch=3,
            grid=(d // tn, cap // tm),
            in_specs=[pl.BlockSpec((tm, f), x_map),
                      pl.BlockSpec((1, f, tn), w_map)],
            out_specs=pl.BlockSpec((tm, tn), o_map),
            scratch_shapes=[pltpu.VMEM((f, tn), BF16)]),
        out_shape=jax.ShapeDtypeStruct((cap, d), F32),
        compiler_params=_cparams(("arbitrary", "arbitrary")),
        name="moe_down",
    )(sub_e, sub_n, n_rows, act, w_down)


def _combine_body(dest_ref, y_hbm, gates_ref, res_ref, g_ref, b_ref, o_ref, buf, sem, *, tm, unroll):
    i = pl.program_id(0)

    def issue_tile(t, slot):
        def issue(g, c):
            for u in range(unroll):
                r = g * unroll + u
                for k in range(TOP_K):
                    _row_copy(y_hbm, buf.at[slot, k], sem.at[slot, k],
                              dest_ref[TOP_K * (t * tm + r) + k], r).start()
            return c

        lax.fori_loop(0, tm // unroll, issue, 0)

    @pl.when(i == 0)
    def _():
        issue_tile(0, 0)

    nxt = i + 1

    @pl.when(nxt < pl.num_programs(0))
    def _():
        issue_tile(nxt, nxt % 2)

    slot = i % 2
    gates = gates_ref[...]
    y = None
    for k in range(TOP_K):
        pltpu.make_async_copy(y_hbm.at[pl.ds(0, tm), :], buf.at[slot, k], sem.at[slot, k]).wait()
        term = buf[slot, k] * gates[:, TOP_K + k:TOP_K + k + 1]
        y = term if y is None else y + term
    o_ref[...] = _layer_norm_rows(DEEPNORM_ALPHA * res_ref[...] + y, g_ref[...], b_ref[...])


def moe_combine_post_norm(y_rows, dest, routing, res, g, b, *, tm):
    n_tok, d = res.shape
    lanes = routing.shape[1]
    return pl.pallas_call(
        functools.partial(_combine_body, tm=tm, unroll=4),
        grid_spec=pltpu.PrefetchScalarGridSpec(
            num_scalar_prefetch=1,
            grid=(n_tok // tm,),
            in_specs=[pl.BlockSpec(memory_space=pl.ANY),
                      pl.BlockSpec((tm, lanes), lambda i, dst: (i, 0)),
                      pl.BlockSpec((tm, d), lambda i, dst: (i, 0)),
                      pl.BlockSpec((1, d), lambda i, dst: (0, 0)),
                      pl.BlockSpec((1, d), lambda i, dst: (0, 0))],
            out_specs=pl.BlockSpec((tm, d), lambda i, dst: (i, 0)),
            scratch_shapes=[pltpu.VMEM((2, TOP_K, tm, d), F32), pltpu.SemaphoreType.DMA((2, TOP_K))]),
        out_shape=jax.ShapeDtypeStruct((n_tok, d), F32),
        compiler_params=_cparams(("arbitrary",)),
        name="moe_combine_post_norm",
    )(dest, y_rows, routing, res, g.reshape(1, d), b.reshape(1, d))


def _group_rows(seg, n_seg, tile, sub):
    n_asg = seg.shape[0]
    cap = n_asg + n_seg * tile
    onehot = (seg[:, None] == jnp.arange(n_seg, dtype=jnp.int32)[None, :]).astype(jnp.int32)
    csum = jnp.cumsum(onehot, axis=0)
    rank = jnp.sum(csum * onehot, axis=1) - 1
    counts = csum[-1]
    padded = (counts + tile - 1) // tile * tile
    pad_end = jnp.cumsum(padded)
    pad_start = pad_end - padded
    dest = (pad_start[seg] + rank).astype(jnp.int32)
    order = jnp.sort(seg * n_asg + jnp.arange(n_asg, dtype=jnp.int32))
    sorted_tok = jnp.pad((order % n_asg) // TOP_K, (0, tile)).astype(jnp.int32)
    sub_start = jnp.arange(cap // sub, dtype=jnp.int32) * sub
    sub_e = jnp.sum((pad_end[None, :] <= sub_start[:, None]).astype(jnp.int32), axis=1)
    sub_e = jnp.minimum(sub_e, n_seg - 1)
    sub_n = jnp.clip(counts[sub_e] - (sub_start - pad_start[sub_e]), 0, sub).astype(jnp.int32)
    start = jnp.cumsum(counts) - counts
    sub_base = (sub_start - (pad_start - start)[sub_e]).astype(jnp.int32)
    n_rows = pad_end[-1:].astype(jnp.int32)
    return dest, sorted_tok, sub_base, sub_e, sub_n, n_rows, cap


def _gla_layer(h, hb, w_in, w_gate_up, b_gate, g_norm, w_o, ln_g, ln_b):
    bsz, s, d = h.shape
    dk = (d // 2) // GLA_HEADS
    dv = d // GLA_HEADS
    n_main = 2 * GLA_HEADS * dk + 2 * GLA_HEADS * dv
    x2 = hb.reshape(bsz * s, d)
    n_qk = 2 * GLA_HEADS * dk
    tm = _pick(bsz * s, (1024, 512, 256, 128, 64))
    tn = _pick(n_qk, (1024, 512, 256, 128))
    qk = matmul(x2, w_in, tm=tm, tn=tn, out_dtype=F32, col_start=0, n_cols=n_qk)
    vr = matmul(x2, w_in, tm=tm, tn=tn, out_dtype=F32, col_start=n_qk, n_cols=n_main - n_qk)
    w_z = jnp.pad(w_in[:, n_main:], ((0, 0), (0, LANES - GLA_GATE_RANK)))
    z = matmul(x2, w_z, tm=tm, tn=LANES, out_dtype=F32)
    wgu = jnp.pad(w_gate_up, ((0, LANES - GLA_GATE_RANK), (0, 0))).astype(BF16)
    wgu = wgu.reshape(LANES, GLA_HEADS, dk).transpose(1, 0, 2)
    og = gla_core(qk.reshape(bsz, s, n_qk), vr.reshape(bsz, s, n_main - n_qk), z.reshape(bsz, s, LANES), wgu,
                  b_gate.reshape(GLA_HEADS, 1, dk), g_norm.reshape(1, dv),
                  heads=GLA_HEADS, dk=dk, dv=dv, step=min(GLA_STEP_TOKENS, s))
    out, outb = matmul_post_norm(og.reshape(bsz * s, d), w_o, h.reshape(bsz * s, d), ln_g, ln_b,
                                 tm=512, sub=256)
    return out.reshape(bsz, s, d), outb.reshape(bsz, s, d)


def _moba_layer(h, hb, w_q, w_kv, w_o, ln_g, ln_b):
    bsz, s, d = h.shape
    hd = d // MOBA_HEADS
    x2 = hb.reshape(bsz * s, d)
    tm = _pick(bsz * s, (1024, 512, 256, 128, 64))
    n_kv = w_kv.shape[1]
    q = matmul(x2, w_q, tm=tm, tn=_pick(d, (1024, 512, 256, 128)), out_dtype=BF16, scale=hd ** -0.5 * LOG2_E)
    kv = matmul(x2, w_kv, tm=tm, tn=_pick(n_kv, (1024, 512, 256, 128)), out_dtype=BF16)
    att = moba_attention(q.reshape(bsz, s, d), kv.reshape(bsz, s, n_kv), heads=MOBA_HEADS,
                         kv_heads=MOBA_KV_HEADS, hd=hd, blk=MOBA_BLOCK, topk=MOBA_TOPK)
    out, outb, outp = matmul_post_norm(att.reshape(bsz * s, d), w_o, h.reshape(bsz * s, d),
                                       ln_g, ln_b, tm=512, sub=256, packed=True)
    return out.reshape(bsz, s, d), outb.reshape(bsz, s, d), outp


def _dense_ffn(h, hb, w_gu, w_down, ln_g, ln_b):
    bsz, s, d = h.shape
    f = w_down.shape[0]
    tm = _pick(bsz * s, (1024, 512, 256, 128, 64))
    act = swiglu_up(hb.reshape(bsz * s, d), w_gu, tm=tm, tf=512, sub=tm)
    out, outb = matmul_post_norm(act, w_down.astype(BF16), h.reshape(bsz * s, d), ln_g, ln_b, tm=512, sub=256)
    return out.reshape(bsz, s, d), outb.reshape(bsz, s, d)


def _moe_ffn(h, hb, hp, w_router, w_gu, w_down, ln_g, ln_b):
    bsz, s, d = h.shape
    n_tok = bsz * s
    h2 = h.reshape(n_tok, d)
    w_r = jnp.pad(w_router, ((0, 0), (0, LANES - N_EXPERTS))).astype(BF16)
    routing = router_top2(hb.reshape(n_tok, d), w_r, tm=_pick(n_tok, (1024, 512, 256, 128, 64)),
                          n_experts=N_EXPERTS)
    seg = routing[:, :TOP_K].astype(jnp.int32).reshape(-1)
    tile = min(MOE_TILE, n_tok)
    sub = min(MOE_SUB, tile)
    dest, sorted_tok, sub_base, sub_e, sub_n, n_rows, cap = _group_rows(seg, N_EXPERTS, tile, sub)
    xs = gather_rows(hp, sorted_tok, sub_base, n_rows, cap=cap, tm=min(GATHER_TILE, tile), sub=sub)
    f = w_down.shape[1]
    act = moe_swiglu_up(xs, w_gu, sub_e, sub_n, n_rows, tm=tile, tf=_pick(f, (512, 256, 128)), sub=sub)
    y_rows = moe_down(act, w_down, sub_e, sub_n, n_rows, tm=min(MOE_DOWN_TILE, tile),
                      tn=_pick(d, (512, 256, 128)), sub=sub)
    out = moe_combine_post_norm(y_rows, dest, routing, h2, ln_g, ln_b, tm=_pick(n_tok, (512, 256, 128, 64)))
    return out.reshape(bsz, s, d)


def kernel(x, w_in_a, w_gate_up_a, b_gate_a, g_norm_a, w_o_a, w_kv_shared, w_q_b, w_o_b,
           ln_mix_g, ln_mix_b, w_gu_dense, w_down_dense, w_router, w_gu_moe, w_down_moe,
           ln_ffn_g, ln_ffn_b):
    h, hb = _gla_layer(x, x.astype(BF16), w_in_a[0], w_gate_up_a[0], b_gate_a[0], g_norm_a[0], w_o_a[0],
                       ln_mix_g[0], ln_mix_b[0])
    h, hb = _dense_ffn(h, hb, w_gu_dense[0], w_down_dense[0], ln_ffn_g[0], ln_ffn_b[0])
    h, hb, hp = _moba_layer(h, hb, w_q_b[0], w_kv_shared, w_o_b[0], ln_mix_g[1], ln_mix_b[1])
    return _moe_ffn(h, hb, hp, w_router[0], w_gu_moe[0], w_down_moe[0], ln_ffn_g[1], ln_ffn_b[1])
```
